```python
import math
import jax
import jax.numpy as jnp
from jax import lax
import numpy as np

D_MODEL = 1024
BATCH = 2
SEQ = 8192
DEPTH = 4
DEC_BATCH = 128
DEC_SEQ = 8
PAST_LEN = 2048
PAGE_SIZE = 128

N_MIXERS = 4
N_LAYERS_A = (DEPTH + 3) // N_MIXERS
N_LAYERS_B = (DEPTH + 2) // N_MIXERS
N_LAYERS_C = (DEPTH + 1) // N_MIXERS
N_LAYERS_D = DEPTH // N_MIXERS
EPS = 1e-6
QBLOCK = 128
OUT_SCALE = 0.5

N_HEADS_A = 8
N_KV_A = 4
GROUP_A = N_HEADS_A // N_KV_A
HEAD_DIM_A = D_MODEL // N_HEADS_A
Q_WIDTH_A = N_HEADS_A * HEAD_DIM_A
KV_WIDTH_A = N_KV_A * HEAD_DIM_A
N_IDX_HEADS = 8
IDX_DIM = 64
IN_WIDTH_A = Q_WIDTH_A + 2 * KV_WIDTH_A + N_IDX_HEADS * IDX_DIM + IDX_DIM + N_IDX_HEADS
TOP_K_MAX = 256

DILATED = ((128, 1), (512, 4), (2048, 16))
N_GROUPS_B = len(DILATED)
HEADS_PER_GROUP_B = 4
HEAD_DIM_B = 128
OUT_WIDTH_B = HEADS_PER_GROUP_B * HEAD_DIM_B
IN_WIDTH_B = 3 * N_GROUPS_B * OUT_WIDTH_B

POOL_WINDOWS = (2, 4, 8, 16)
POOL_GROUP = D_MODEL // len(POOL_WINDOWS)
POOL_STATE = max(POOL_WINDOWS) - 1

N_HEADS_D = 4
HEAD_DIM_D = D_MODEL // N_HEADS_D
WIDTH_D = N_HEADS_D * HEAD_DIM_D
IN_WIDTH_D = 4 * WIDTH_D + 2 * N_HEADS_D
CHUNK_D = 64

N_MEM = 256
N_HEADS_MEM = 4
HEAD_DIM_MEM = D_MODEL // N_HEADS_MEM
WIDTH_MEM = N_HEADS_MEM * HEAD_DIM_MEM
D_FF = 4 * D_MODEL

kernel_name = 'hybrid_dsa_dilated_pool_mlstm_decoder_step'


def rmsnorm(x, g):
    xf = x.astype(jnp.float32)
    y = xf * lax.rsqrt(jnp.mean(xf * xf, axis=-1, keepdims=True) + EPS)
    return (y * g.astype(jnp.float32)).astype(x.dtype)


def to_blocks(a):
    b, s = a.shape[:2]
    return jnp.moveaxis(a.reshape((b, s // QBLOCK, QBLOCK) + a.shape[2:]), 1, 0)


def from_blocks(o):
    nb, b, qb = o.shape[:3]
    return jnp.moveaxis(o, 0, 1).reshape((b, nb * qb) + o.shape[3:])


def gather_rows(rows, idx):
    return jax.vmap(lambda r, i: r[i])(rows, idx)


def sq_relu_mlp(h, w1, w2):
    return jnp.square(jax.nn.relu(h @ w1)) @ w2


def mem_keys_values(mem, g_src, w_kv, g_k):
    b, m, _ = mem.shape
    kv = (rmsnorm(mem, g_src) @ w_kv).reshape(b, m, 2, N_HEADS_MEM, HEAD_DIM_MEM)
    return jnp.stack([rmsnorm(kv[:, :, 0], g_k), kv[:, :, 1]], axis=2)


def mem_attend(h, kv, w_q, g_q, w_o):
    b, t, _ = h.shape
    q = rmsnorm((h @ w_q).reshape(b, t, N_HEADS_MEM, HEAD_DIM_MEM), g_q)
    logits = jnp.einsum('bthd,bmhd->bhtm', q, kv[:, :, 0].astype(q.dtype)).astype(jnp.float32) * HEAD_DIM_MEM ** -0.5
    p = jax.nn.softmax(logits, axis=-1).astype(h.dtype)
    o = jnp.einsum('bhtm,bmhd->bthd', p, kv[:, :, 1].astype(h.dtype)).reshape(b, t, WIDTH_MEM)
    return o @ w_o


def proj_a(h, w_in, g_q, g_k):
    b, t, _ = h.shape
    p = h @ w_in
    o1 = Q_WIDTH_A
    o2 = o1 + KV_WIDTH_A
    o3 = o2 + KV_WIDTH_A
    o4 = o3 + N_IDX_HEADS * IDX_DIM
    o5 = o4 + IDX_DIM
    q = rmsnorm(p[..., :o1].reshape(b, t, N_HEADS_A, HEAD_DIM_A), g_q)
    k = rmsnorm(p[..., o1:o2].reshape(b, t, N_KV_A, HEAD_DIM_A), g_k)
    v = p[..., o2:o3].reshape(b, t, N_KV_A, HEAD_DIM_A)
    qi = p[..., o3:o4].reshape(b, t, N_IDX_HEADS, IDX_DIM)
    ki = p[..., o4:o5]
    wi = p[..., o5:]
    return q, k, v, qi, ki, wi


def indexer_scores(qi, wi, ki):
    s = jax.nn.relu(jnp.einsum('bthd,bsd->bths', qi, ki).astype(jnp.float32) * IDX_DIM ** -0.5)
    return jnp.einsum('bths,bth->bts', s, wi.astype(jnp.float32)) * N_IDX_HEADS ** -0.5


def attend_gathered(q, k_sel, v_sel, valid):
    b, t = q.shape[:2]
    qg = q.reshape(b, t, N_KV_A, GROUP_A, HEAD_DIM_A)
    logits = jnp.einsum('bthgd,btnhd->bthgn', qg, k_sel).astype(jnp.float32) * HEAD_DIM_A ** -0.5
    logits = jnp.where(valid[:, :, None, None, :], logits, -jnp.inf)
    p = jax.nn.softmax(logits, axis=-1).astype(v_sel.dtype)
    return jnp.einsum('bthgn,btnhd->bthgd', p, v_sel).reshape(b, t, Q_WIDTH_A)


def dsa_prompt(h, w_in, g_q, g_k, w_out):
    b, s, _ = h.shape
    q, k, v, qi, ki, wi = proj_a(h, w_in, g_q, g_k)
    n_top = min(TOP_K_MAX, s // 4)
    key_pos = jnp.arange(s)

    def block(args):
        blk, q_b, qi_b, wi_b = args
        tpos = blk * QBLOCK + jnp.arange(QBLOCK)
        score = indexer_scores(qi_b, wi_b, ki)
        score = jnp.where((key_pos[None, :] <= tpos[:, None])[None], score, -jnp.inf)
        _, idx = lax.top_k(score, n_top)
        valid = idx <= tpos[None, :, None]
        return attend_gathered(q_b, gather_rows(k, idx), gather_rows(v, idx), valid)

    o = lax.map(block, (jnp.arange(s // QBLOCK), to_blocks(q), to_blocks(qi), to_blocks(wi)))
    return from_blocks(o) @ w_out, jnp.stack([k, v], axis=2), ki


def dsa_sample(h, kv_pool, kidx_pool, page_table, w_in, g_q, g_k, w_out):
    b, t, _ = h.shape
    q, k, v, qi, ki, wi = proj_a(h, w_in, g_q, g_k)
    past = page_table.shape[1] * PAGE_SIZE
    n_top = min(TOP_K_MAX, (past + t) // 4)
    past_ki = kidx_pool[page_table].reshape(b, past, IDX_DIM).astype(ki.dtype)
    score = indexer_scores(qi, wi, jnp.concatenate([past_ki, ki], axis=1))
    tpos = past + jnp.arange(t)
    causal = jnp.arange(past + t)[None, :] <= tpos[:, None]
    _, idx = lax.top_k(jnp.where(causal[None], score, -jnp.inf), n_top)
    from_past = idx < past
    pidx = jnp.minimum(idx, past - 1)
    phys = page_table[jnp.arange(b)[:, None, None], pidx // PAGE_SIZE]
    off = pidx % PAGE_SIZE
    k_past = kv_pool[phys, off, 0].astype(k.dtype)
    v_past = kv_pool[phys, off, 1].astype(v.dtype)
    own = jnp.arange(t)
    sel_own = jnp.any(idx[..., None] == (past + own), axis=2) & (own[None, :] <= own[:, None])[None]
    qg = q.reshape(b, t, N_KV_A, GROUP_A, HEAD_DIM_A)
    scale = HEAD_DIM_A ** -0.5
    lp = jnp.einsum('bthgd,btnhd->bthgn', qg, k_past).astype(jnp.float32) * scale
    lp = jnp.where(from_past[:, :, None, None, :], lp, -jnp.inf)
    ln = jnp.einsum('bthgd,bshd->bthgs', qg, k).astype(jnp.float32) * scale
    ln = jnp.where(sel_own[:, :, None, None, :], ln, -jnp.inf)
    p = jax.nn.softmax(jnp.concatenate([lp, ln], axis=-1), axis=-1).astype(v.dtype)
    o = (jnp.einsum('bthgn,btnhd->bthgd', p[..., :n_top], v_past)
         + jnp.einsum('bthgs,bshd->bthgd', p[..., n_top:], v))
    return o.reshape(b, t, Q_WIDTH_A) @ w_out, jnp.stack([k, v], axis=2), ki


def proj_b(h, w_in, g_q, g_k):
    b, t, _ = h.shape
    p = (h @ w_in).reshape(b, t, 3, N_GROUPS_B, HEADS_PER_GROUP_B, HEAD_DIM_B)
    q = rmsnorm(p[:, :, 0], g_q[:, None, :])
    k = rmsnorm(p[:, :, 1], g_k[:, None, :])
    return q, k, p[:, :, 2]


def merge_groups(outs, lses):
    wts = jax.nn.softmax(jnp.stack(lses, axis=-1), axis=-1)
    o = jnp.einsum('bthg,bthgd->bthd', wts.astype(outs[0].dtype), jnp.stack(outs, axis=3))
    return o.reshape(o.shape[0], o.shape[1], OUT_WIDTH_B)


def band_dilated(q_b, kpad, vpad, t0, window, dil):
    b, qb, hg, dh = q_b.shape
    n_a = qb // dil
    n_m = (window + qb) // dil
    kb = lax.dynamic_slice_in_dim(kpad, t0, window + qb, axis=1).reshape(b, n_m, dil, hg, dh)
    vb = lax.dynamic_slice_in_dim(vpad, t0, window + qb, axis=1).reshape(b, n_m, dil, hg, dh)
    qr = q_b.reshape(b, n_a, dil, hg, dh)
    logits = jnp.einsum('barhe,bmrhe->barhm', qr, kb).astype(jnp.float32) * dh ** -0.5
    m_idx = jnp.arange(n_m)
    rel = m_idx - jnp.arange(n_a)[:, None, None, None]
    key_pos = t0 - window + m_idx * dil + jnp.arange(dil)[None, :, None, None]
    mask = (rel >= 0) & (rel <= window // dil) & (key_pos >= 0)
    logits = jnp.where(mask, logits, -jnp.inf)
    lse = jax.nn.logsumexp(logits, axis=-1)
    p = jnp.exp(logits - lse[..., None]).astype(vb.dtype)
    o = jnp.einsum('barhm,bmrhe->barhe', p, vb).reshape(b, qb, hg, dh)
    return o, lse.reshape(b, qb, hg)


def dilated_prompt(h, w_in, g_q, g_k, w_out):
    b, s, _ = h.shape
    q, k, v = proj_b(h, w_in, g_q, g_k)
    pads = [(jnp.pad(k[:, :, g], ((0, 0), (w, 0), (0, 0), (0, 0))),
             jnp.pad(v[:, :, g], ((0, 0), (w, 0), (0, 0), (0, 0)))) for g, (w, _) in enumerate(DILATED)]

    def block(args):
        blk, q_b = args
        t0 = blk * QBLOCK
        outs, lses = [], []
        for g, (w, d) in enumerate(DILATED):
            o, l = band_dilated(q_b[:, :, g], pads[g][0], pads[g][1], t0, w, d)
            outs.append(o)
            lses.append(l)
        return merge_groups(outs, lses)

    o = from_blocks(lax.map(block, (jnp.arange(s // QBLOCK), to_blocks(q))))
    kv = jnp.stack([k, v], axis=2)
    bufs = [kv[:, s - min(w, s):, :, g] for g, (w, _) in enumerate(DILATED)]
    return o @ w_out, bufs


def dilated_sample(h, bufs, w_in, g_q, g_k, w_out):
    b, t, _ = h.shape
    q, k, v = proj_b(h, w_in, g_q, g_k)
    kv_new = jnp.stack([k, v], axis=2)
    outs, lses, new_bufs = [], [], []
    for g, (w, d) in enumerate(DILATED):
        buf = bufs[g].astype(kv_new.dtype)
        wb = buf.shape[1]
        ext = jnp.concatenate([buf, kv_new[:, :, :, g]], axis=1)
        idx = wb + jnp.arange(t)[:, None] - jnp.arange(w // d + 1)[None, :] * d
        sel = jnp.take(ext, jnp.maximum(idx, 0), axis=1)
        logits = jnp.einsum('bthe,btnhe->bthn', q[:, :, g], sel[:, :, :, 0]).astype(jnp.float32) * HEAD_DIM_B ** -0.5
        logits = jnp.where((idx >= 0)[None, :, None, :], logits, -jnp.inf)
        lse = jax.nn.logsumexp(logits, axis=-1)
        p = jnp.exp(logits - lse[..., None]).astype(sel.dtype)
        outs.append(jnp.einsum('bthn,btnhe->bthe', p, sel[:, :, :, 1]))
        lses.append(lse)
        new_bufs.append(ext[:, t:])
    return merge_groups(outs, lses) @ w_out, new_bufs


def pool_mixer(h, prefix, w_in, w_grp, scale, w_out):
    b, t, _ = h.shape
    u = h @ w_in
    ext = jnp.concatenate([prefix.astype(u.dtype), u], axis=1)
    n_pre = prefix.shape[1]
    cs = jnp.concatenate([jnp.zeros((b, 1, D_MODEL), jnp.float32),
                          jnp.cumsum(ext.astype(jnp.float32), axis=1)], axis=1)
    hi = n_pre + 1 + jnp.arange(t)
    uf = u.astype(jnp.float32)
    res = []
    for g, w in enumerate(POOL_WINDOWS):
        lo = jnp.maximum(hi - w, 0)
        seg = slice(g * POOL_GROUP, (g + 1) * POOL_GROUP)
        csg = cs[:, :, seg]
        mean = (csg[:, hi] - csg[:, lo]) / (hi - lo).astype(jnp.float32)[:, None]
        res.append(mean - uf[:, :, seg])
    r = jnp.stack(res, axis=2).astype(u.dtype)
    y = jnp.einsum('btgc,gcd->btgd', r, w_grp).reshape(b, t, D_MODEL) * scale
    return y @ w_out, ext[:, -POOL_STATE:]


def mlstm_chunk(carry, xs):
    c, n, m = carry
    q, k, v, li, lf = xs
    L = q.shape[1]
    bt = jnp.swapaxes(jnp.cumsum(lf, axis=1), 1, 2)
    it = jnp.swapaxes(li, 1, 2)
    causal = jnp.tril(jnp.ones((L, L), bool))
    dmat = jnp.where(causal, bt[..., :, None] - bt[..., None, :] + it[..., None, :], -jnp.inf)
    g = bt + m[..., None]
    m_t = jnp.maximum(g, jnp.max(dmat, axis=-1))
    w_intra = jnp.exp(dmat - m_t[..., None])
    w_inter = jnp.exp(g - m_t)
    qk = jnp.einsum('blhd,bshd->bhls', q, k) * w_intra
    num = w_inter[..., None] * jnp.einsum('blhd,bhde->bhle', q, c) + jnp.einsum('bhls,bshe->bhle', qk, v)
    den = w_inter * jnp.einsum('blhd,bhd->bhl', q, n) + jnp.sum(qk, axis=-1)
    hout = num / jnp.maximum(jnp.abs(den), jnp.exp(-m_t))[..., None]
    b_last = bt[..., -1]
    decay = b_last[..., None] - bt + it
    m_new = jnp.maximum(b_last + m, jnp.max(decay, axis=-1))
    keep = jnp.exp(b_last + m - m_new)
    wk = jnp.exp(decay - m_new[..., None])
    c_new = keep[..., None, None] * c + jnp.einsum('bhs,bshd,bshe->bhde', wk, k, v)
    n_new = keep[..., None] * n + jnp.einsum('bhs,bshd->bhd', wk, k)
    return (c_new, n_new, m_new), jnp.swapaxes(hout, 1, 2)


def mlstm_mixer(h, c0, n0, m0, w_in, b_i, b_f, g_h, w_out):
    b, t, _ = h.shape
    p = h @ w_in
    shp = (b, t, N_HEADS_D, HEAD_DIM_D)
    q = p[..., :WIDTH_D].reshape(shp)
    k = p[..., WIDTH_D:2 * WIDTH_D].reshape(shp) * HEAD_DIM_D ** -0.5
    v = p[..., 2 * WIDTH_D:3 * WIDTH_D].reshape(shp)
    o_pre = p[..., 3 * WIDTH_D:4 * WIDTH_D]
    log_i = (p[..., 4 * WIDTH_D:4 * WIDTH_D + N_HEADS_D] + b_i).astype(jnp.float32)
    log_f = jax.nn.log_sigmoid((p[..., 4 * WIDTH_D + N_HEADS_D:] + b_f).astype(jnp.float32))
    L = math.gcd(t, CHUNK_D)
    nc = t // L

    def chunks(a):
        return jnp.moveaxis(a.astype(jnp.float32).reshape((b, nc, L) + a.shape[2:]), 1, 0)

    carry0 = (c0.astype(jnp.float32), n0.astype(jnp.float32), m0.astype(jnp.float32))
    (c, n, m), hs = lax.scan(mlstm_chunk, carry0, (chunks(q), chunks(k), chunks(v), chunks(log_i), chunks(log_f)))
    hs = jnp.moveaxis(hs, 0, 1).reshape(shp)
    hn = rmsnorm(hs, g_h.reshape(N_HEADS_D, HEAD_DIM_D)).reshape(b, t, WIDTH_D)
    y = (jax.nn.sigmoid(o_pre.astype(jnp.float32)) * hn).astype(h.dtype) @ w_out
    return y, c.astype(c0.dtype), n.astype(n0.dtype), m.astype(m0.dtype)


def setup_inputs(seed: int = 0) -> dict:
    key = jax.random.key(seed)
    ks = iter(jax.random.split(key, 64))

    def nrm(shape, scale=1.0):
        return jax.random.normal(next(ks), shape, jnp.float32) * scale

    def gain(shape):
        return 1.0 + nrm(shape, 0.05)

    n_pages = PAST_LEN // PAGE_SIZE
    n_used = DEC_BATCH * n_pages
    n_pool = n_used + n_used // 4
    inp = {}
    inp['x_prompt'] = nrm((BATCH, SEQ, D_MODEL))
    inp['x_sample'] = nrm((DEC_BATCH, DEC_SEQ, D_MODEL))
    inp['cache_kv_a'] = nrm((N_LAYERS_A, n_pool, PAGE_SIZE, 2, N_KV_A, HEAD_DIM_A))
    inp['cache_kidx_a'] = nrm((N_LAYERS_A, n_pool, PAGE_SIZE, IDX_DIM))
    inp['page_table'] = jax.random.permutation(next(ks), n_pool)[:n_used].reshape(DEC_BATCH, n_pages).astype(jnp.int32)
    for g, (w, _) in enumerate(DILATED):
        inp['cache_kv_b%d' % g] = nrm((N_LAYERS_B, DEC_BATCH, min(w, PAST_LEN), 2, HEADS_PER_GROUP_B, HEAD_DIM_B))
    inp['state_pool_c'] = nrm((N_LAYERS_C, DEC_BATCH, POOL_STATE, D_MODEL))
    inp['state_c_d'] = nrm((N_LAYERS_D, DEC_BATCH, N_HEADS_D, HEAD_DIM_D, HEAD_DIM_D), 0.1)
    inp['state_n_d'] = nrm((N_LAYERS_D, DEC_BATCH, N_HEADS_D, HEAD_DIM_D), 0.5)
    inp['state_m_d'] = nrm((N_LAYERS_D, DEC_BATCH, N_HEADS_D), 0.5)
    inp['cache_mem_kv'] = nrm((DEPTH, DEC_BATCH, N_MEM, 2, N_HEADS_MEM, HEAD_DIM_MEM))
    inp['mem_prompt'] = nrm((BATCH, N_MEM, D_MODEL))
    inp['g_norm_mix'] = gain((DEPTH, D_MODEL))
    inp['g_norm_mem'] = gain((DEPTH, D_MODEL))
    inp['g_norm_ffn'] = gain((DEPTH, D_MODEL))
    inp['g_mem_src'] = gain((DEPTH, D_MODEL))
    inp['w_mem_q'] = nrm((DEPTH, D_MODEL, WIDTH_MEM), D_MODEL ** -0.5)
    inp['w_mem_kv'] = nrm((DEPTH, D_MODEL, 2 * WIDTH_MEM), D_MODEL ** -0.5)
    inp['g_mem_q'] = gain((DEPTH, HEAD_DIM_MEM))
    inp['g_mem_k'] = gain((DEPTH, HEAD_DIM_MEM))
    inp['w_mem_o'] = nrm((DEPTH, WIDTH_MEM, D_MODEL), OUT_SCALE * WIDTH_MEM ** -0.5)
    inp['w_ff1'] = nrm((DEPTH, D_MODEL, D_FF), D_MODEL ** -0.5)
    inp['w_ff2'] = nrm((DEPTH, D_FF, D_MODEL), OUT_SCALE * D_FF ** -0.5)
    inp['w_in_a'] = nrm((N_LAYERS_A, D_MODEL, IN_WIDTH_A), D_MODEL ** -0.5)
    inp['g_q_a'] = gain((N_LAYERS_A, HEAD_DIM_A))
    inp['g_k_a'] = gain((N_LAYERS_A, HEAD_DIM_A))
    inp['w_out_a'] = nrm((N_LAYERS_A, Q_WIDTH_A, D_MODEL), OUT_SCALE * Q_WIDTH_A ** -0.5)
    inp['w_in_b'] = nrm((N_LAYERS_B, D_MODEL, IN_WIDTH_B), D_MODEL ** -0.5)
    inp['g_q_b'] = gain((N_LAYERS_B, N_GROUPS_B, HEAD_DIM_B))
    inp['g_k_b'] = gain((N_LAYERS_B, N_GROUPS_B, HEAD_DIM_B))
    inp['w_out_b'] = nrm((N_LAYERS_B, OUT_WIDTH_B, D_MODEL), OUT_SCALE * OUT_WIDTH_B ** -0.5)
    inp['w_in_c'] = nrm((N_LAYERS_C, D_MODEL, D_MODEL), D_MODEL ** -0.5)
    inp['w_grp_c'] = nrm((N_LAYERS_C, len(POOL_WINDOWS), POOL_GROUP, POOL_GROUP), POOL_GROUP ** -0.5)
    inp['scale_c'] = gain((N_LAYERS_C, D_MODEL))
    inp['w_out_c'] = nrm((N_LAYERS_C, D_MODEL, D_MODEL), OUT_SCALE * D_MODEL ** -0.5)
    inp['w_in_d'] = nrm((N_LAYERS_D, D_MODEL, IN_WIDTH_D), D_MODEL ** -0.5)
    inp['b_i_d'] = nrm((N_LAYERS_D, N_HEADS_D), 0.1)
    inp['b_f_d'] = 3.0 + nrm((N_LAYERS_D, N_HEADS_D), 0.5)
    inp['g_h_d'] = gain((N_LAYERS_D, WIDTH_D))
    inp['w_out_d'] = nrm((N_LAYERS_D, WIDTH_D, D_MODEL), OUT_SCALE * WIDTH_D ** -0.5)
    return inp


def reference(x_prompt, x_sample, cache_kv_a, cache_kidx_a, page_table, cache_kv_b0, cache_kv_b1, cache_kv_b2,
              state_pool_c, state_c_d, state_n_d, state_m_d, cache_mem_kv, mem_prompt,
              g_norm_mix, g_norm_mem, g_norm_ffn, g_mem_src, w_mem_q, w_mem_kv, g_mem_q, g_mem_k, w_mem_o,
              w_ff1, w_ff2, w_in_a, g_q_a, g_k_a, w_out_a, w_in_b, g_q_b, g_k_b, w_out_b,
              w_in_c, w_grp_c, scale_c, w_out_c, w_in_d, b_i_d, b_f_d, g_h_d, w_out_d):
    xp, xs = x_prompt, x_sample
    kv_a_p, ki_a_p, kv_a_s, ki_a_s = [], [], [], []
    kv_b_p = [[] for _ in DILATED]
    kv_b_s = [[] for _ in DILATED]
    pool_p, pool_s = [], []
    cd_p, nd_p, md_p, cd_s, nd_s, md_s = [], [], [], [], [], []
    mem_p = []
    cache_b = (cache_kv_b0, cache_kv_b1, cache_kv_b2)
    for i in range(DEPTH):
        kind, j = i % N_MIXERS, i // N_MIXERS
        hp = rmsnorm(xp, g_norm_mix[i])
        hs = rmsnorm(xs, g_norm_mix[i])
        if kind == 0:
            yp, kv, ki = dsa_prompt(hp, w_in_a[j], g_q_a[j], g_k_a[j], w_out_a[j])
            kv_a_p.append(kv)
            ki_a_p.append(ki)
            ys, kv, ki = dsa_sample(hs, cache_kv_a[j], cache_kidx_a[j], page_table, w_in_a[j], g_q_a[j], g_k_a[j], w_out_a[j])
            kv_a_s.append(kv)
            ki_a_s.append(ki)
        elif kind == 1:
            yp, bufs = dilated_prompt(hp, w_in_b[j], g_q_b[j], g_k_b[j], w_out_b[j])
            for g in range(N_GROUPS_B):
                kv_b_p[g].append(bufs[g])
            ys, bufs = dilated_sample(hs, [cb[j] for cb in cache_b], w_in_b[j], g_q_b[j], g_k_b[j], w_out_b[j])
            for g in range(N_GROUPS_B):
                kv_b_s[g].append(bufs[g])
        elif kind == 2:
            empty = jnp.zeros((hp.shape[0], 0, D_MODEL), hp.dtype)
            yp, st = pool_mixer(hp, empty, w_in_c[j], w_grp_c[j], scale_c[j], w_out_c[j])
            pool_p.append(st)
            ys, st = pool_mixer(hs, state_pool_c[j], w_in_c[j], w_grp_c[j], scale_c[j], w_out_c[j])
            pool_s.append(st)
        else:
            bp = hp.shape[0]
            c0 = jnp.zeros((bp, N_HEADS_D, HEAD_DIM_D, HEAD_DIM_D), hp.dtype)
            n0 = jnp.zeros((bp, N_HEADS_D, HEAD_DIM_D), hp.dtype)
            m0 = jnp.zeros((bp, N_HEADS_D), hp.dtype)
            yp, c, n, m = mlstm_mixer(hp, c0, n0, m0, w_in_d[j], b_i_d[j], b_f_d[j], g_h_d[j], w_out_d[j])
            cd_p.append(c)
            nd_p.append(n)
            md_p.append(m)
            ys, c, n, m = mlstm_mixer(hs, state_c_d[j], state_n_d[j], state_m_d[j], w_in_d[j], b_i_d[j], b_f_d[j], g_h_d[j], w_out_d[j])
            cd_s.append(c)
            nd_s.append(n)
            md_s.append(m)
        xp = xp + yp
        xs = xs + ys
        mkv = mem_keys_values(mem_prompt, g_mem_src[i], w_mem_kv[i], g_mem_k[i])
        mem_p.append(mkv)
        xp = xp + mem_attend(rmsnorm(xp, g_norm_mem[i]), mkv, w_mem_q[i], g_mem_q[i], w_mem_o[i])
        xs = xs + mem_attend(rmsnorm(xs, g_norm_mem[i]), cache_mem_kv[i], w_mem_q[i], g_mem_q[i], w_mem_o[i])
        xp = xp + sq_relu_mlp(rmsnorm(xp, g_norm_ffn[i]), w_ff1[i], w_ff2[i])
        xs = xs + sq_relu_mlp(rmsnorm(xs, g_norm_ffn[i]), w_ff1[i], w_ff2[i])
    return (xp, xs,
            jnp.stack(kv_a_p), jnp.stack(ki_a_p), jnp.stack(kv_a_s), jnp.stack(ki_a_s),
            jnp.stack(kv_b_p[0]), jnp.stack(kv_b_p[1]), jnp.stack(kv_b_p[2]),
            jnp.stack(kv_b_s[0]), jnp.stack(kv_b_s[1]), jnp.stack(kv_b_s[2]),
            jnp.stack(pool_p), jnp.stack(pool_s),
            jnp.stack(cd_p), jnp.stack(nd_p), jnp.stack(md_p),
            jnp.stack(cd_s), jnp.stack(nd_s), jnp.stack(md_s),
            jnp.stack(mem_p))
```

```python
import functools

import jax
import jax.numpy as jnp
from jax import lax
from jax.experimental import pallas as pl
from jax.experimental.pallas import tpu as pltpu

F32 = jnp.float32
BF16 = jnp.bfloat16
EPS = 1e-6
NEG = -1e30
LANES = 128
SUBLANES = 8
VMEM_LIMIT_BYTES = 56 * 2**20
ROW_TILE = 1024
POOL_HALO = 16
MAX_BISECTIONS = 512

NT_DIMS = (((1,), (1,)), ((), ()))
TN_DIMS = (((0,), (0,)), ((), ()))


def _params(*sem):
    return pltpu.CompilerParams(dimension_semantics=sem, vmem_limit_bytes=VMEM_LIMIT_BYTES)


def _rms(x, g):
    return x * lax.rsqrt(jnp.mean(x * x, axis=-1, keepdims=True) + EPS) * g


def _col_tile(n):
    for t in (512, 384, 256, 128):
        if n % t == 0:
            return t
    raise ValueError(f"matmul width {n} is not a multiple of {LANES}")


def _pad_rows(x, rows):
    if x.shape[0] == rows:
        return x
    return jnp.concatenate([x, jnp.zeros((rows - x.shape[0],) + x.shape[1:], x.dtype)], axis=0)


def _pad_cols(w, n):
    return jnp.pad(w, ((0, 0), (0, n - w.shape[1])))


def _rms_matmul_body(x_ref, g_ref, w_ref, o_ref, h_ref, *, precision):
    @pl.when(pl.program_id(1) == 0)
    def _():
        h_ref[...] = _rms(x_ref[...], g_ref[...]).astype(h_ref.dtype)

    o_ref[...] = jnp.dot(h_ref[...], w_ref[...], preferred_element_type=F32, precision=precision)


def rms_matmul(x, g, w, *, precise=False):
    m, d = x.shape
    n = w.shape[1]
    tm, tn = min(m, ROW_TILE), _col_tile(n)
    return pl.pallas_call(
        functools.partial(_rms_matmul_body, precision=lax.Precision.HIGHEST if precise else None),
        grid=(m // tm, n // tn),
        in_specs=[pl.BlockSpec((tm, d), lambda i, j: (i, 0)),
                  pl.BlockSpec((1, d), lambda i, j: (0, 0)),
                  pl.BlockSpec((d, tn), lambda i, j: (0, j))],
        out_specs=pl.BlockSpec((tm, tn), lambda i, j: (i, j)),
        out_shape=jax.ShapeDtypeStruct((m, n), F32),
        scratch_shapes=[pltpu.VMEM((tm, d), F32 if precise else BF16)],
        compiler_params=_params("parallel", "arbitrary"),
        name="rms_matmul",
    )(x, g.reshape(1, d), w)


def _matmul_res_body(a_ref, w_ref, r_ref, o_ref):
    o_ref[...] = r_ref[...] + jnp.dot(a_ref[...].astype(BF16), w_ref[...], preferred_element_type=F32)


def matmul_res(a, w, r):
    m, k = a.shape
    n = w.shape[1]
    tm, tn = min(m, ROW_TILE), _col_tile(n)
    return pl.pallas_call(
        _matmul_res_body,
        grid=(m // tm, n // tn),
        in_specs=[pl.BlockSpec((tm, k), lambda i, j: (i, 0)),
                  pl.BlockSpec((k, tn), lambda i, j: (0, j)),
                  pl.BlockSpec((tm, tn), lambda i, j: (i, j))],
        out_specs=pl.BlockSpec((tm, tn), lambda i, j: (i, j)),
        out_shape=jax.ShapeDtypeStruct((m, n), F32),
        compiler_params=_params("parallel", "parallel"),
        name="matmul_res",
    )(a, w, r)


def _ffn_body(x_ref, g_ref, w1_ref, w2_ref, o_ref, h_ref, acc_ref):
    f = pl.program_id(1)

    @pl.when(f == 0)
    def _():
        h_ref[...] = _rms(x_ref[...], g_ref[...]).astype(BF16)
        acc_ref[...] = jnp.zeros_like(acc_ref)

    a = jnp.dot(h_ref[...], w1_ref[...], preferred_element_type=F32)
    a = jnp.square(jnp.maximum(a, 0.0)).astype(BF16)
    acc_ref[...] += jnp.dot(a, w2_ref[...], preferred_element_type=F32)

    @pl.when(f == pl.num_programs(1) - 1)
    def _():
        o_ref[...] = x_ref[...] + acc_ref[...]


def ffn(x, g, w1, w2):
    m, d = x.shape
    ff = w1.shape[1]
    tm, tf = min(m, ROW_TILE), 512
    return pl.pallas_call(
        _ffn_body,
        grid=(m // tm, ff // tf),
        in_specs=[pl.BlockSpec((tm, d), lambda i, f: (i, 0)),
                  pl.BlockSpec((1, d), lambda i, f: (0, 0)),
                  pl.BlockSpec((d, tf), lambda i, f: (0, f)),
                  pl.BlockSpec((tf, d), lambda i, f: (f, 0))],
        out_specs=pl.BlockSpec((tm, d), lambda i, f: (i, 0)),
        out_shape=jax.ShapeDtypeStruct((m, d), F32),
        scratch_shapes=[pltpu.VMEM((tm, d), BF16), pltpu.VMEM((tm, d), F32)],
        compiler_params=_params("parallel", "arbitrary"),
        name="ffn",
    )(x, g.reshape(1, d), w1, w2)


def _head_norm_body(*refs, head_dim, n_heads, with_copy, with_bf16):
    k_ref, g_ref = refs[0], refs[1]
    v_ref = refs[2] if with_copy else None
    outs = refs[3 if with_copy else 2:]
    width = n_heads * head_dim
    for h in range(n_heads):
        sl = slice(h * head_dim, (h + 1) * head_dim)
        kn = _rms(k_ref[:, sl], g_ref[:, sl])
        for o in outs:
            o[:, sl] = kn.astype(o.dtype)
    if with_copy:
        v = v_ref[...]
        for o in outs:
            o[:, width:] = v.astype(o.dtype)


def head_norm(p, k_block, width, head_dim, gain, *, v_block=None, with_bf16=False):
    m = p.shape[0]
    n_heads = width // head_dim
    tm = min(m, ROW_TILE)
    with_copy = v_block is not None
    ow = 2 * width if with_copy else width
    in_specs = [pl.BlockSpec((tm, width), lambda i: (i, k_block)),
                pl.BlockSpec((1, width), lambda i: (0, 0))]
    args = [p, jnp.tile(gain.reshape(1, head_dim), (1, n_heads))]
    if with_copy:
        in_specs.append(pl.BlockSpec((tm, width), lambda i: (i, v_block)))
        args.append(p)
    out_shape = [jax.ShapeDtypeStruct((m, ow), F32)]
    if with_bf16:
        out_shape.append(jax.ShapeDtypeStruct((m, ow), BF16))
    out = pl.pallas_call(
        functools.partial(_head_norm_body, head_dim=head_dim, n_heads=n_heads, with_copy=with_copy,
                          with_bf16=with_bf16),
        grid=(m // tm,),
        in_specs=in_specs,
        out_specs=[pl.BlockSpec((tm, ow), lambda i: (i, 0)) for _ in out_shape],
        out_shape=out_shape,
        compiler_params=_params("parallel"),
        name="head_norm",
    )(*args)
    return out if with_bf16 else out[0]


def head_norm_groups(p, k_block, width, gain_row):
    m = p.shape[0]
    tm = min(m, ROW_TILE)
    n_heads = width // LANES
    return pl.pallas_call(
        functools.partial(_head_norm_body, head_dim=LANES, n_heads=n_heads, with_copy=False, with_bf16=False),
        grid=(m // tm,),
        in_specs=[pl.BlockSpec((tm, width), lambda i: (i, k_block)),
                  pl.BlockSpec((1, width), lambda i: (0, 0))],
        out_specs=[pl.BlockSpec((tm, width), lambda i: (i, 0))],
        out_shape=[jax.ShapeDtypeStruct((m, width), F32)],
        compiler_params=_params("parallel"),
        name="head_norm_groups",
    )(p, gain_row)[0]


def _mem_attn_body(q_ref, kv_ref, gq_ref, o_ref, *, n_heads, head_dim):
    width = n_heads * head_dim
    scale = head_dim ** -0.5
    for b in range(q_ref.shape[0]):
        for h in range(n_heads):
            sl = slice(h * head_dim, (h + 1) * head_dim)
            qn = (_rms(q_ref[b, :, sl], gq_ref[...]) * scale).astype(BF16)
            k = kv_ref[b, :, sl].astype(BF16)
            v = kv_ref[b, :, width + h * head_dim:width + (h + 1) * head_dim].astype(BF16)
            lg = lax.dot_general(qn, k, NT_DIMS, preferred_element_type=F32)
            p = jnp.exp(lg - jnp.max(lg, axis=-1, keepdims=True))
            l = jnp.sum(p, axis=-1, keepdims=True)
            o = jnp.dot(p.astype(BF16), v, preferred_element_type=F32) / l
            o_ref[b, :, sl] = o.astype(o_ref.dtype)


def mem_attn(q, kv, g_q, *, n_heads, batch_block, q_block, out_dtype):
    b, t, w = q.shape
    n_mem = kv.shape[1]
    head_dim = w // n_heads
    return pl.pallas_call(
        functools.partial(_mem_attn_body, n_heads=n_heads, head_dim=head_dim),
        grid=(b // batch_block, t // q_block),
        in_specs=[pl.BlockSpec((batch_block, q_block, w), lambda i, j: (i, j, 0)),
                  pl.BlockSpec((batch_block, n_mem, 2 * w), lambda i, j: (i, 0, 0)),
                  pl.BlockSpec((1, head_dim), lambda i, j: (0, 0))],
        out_specs=pl.BlockSpec((batch_block, q_block, w), lambda i, j: (i, j, 0)),
        out_shape=jax.ShapeDtypeStruct((b, t, w), out_dtype),
        compiler_params=_params("parallel", "parallel"),
        name="mem_attn",
    )(q, kv, g_q.reshape(1, head_dim))


def _fold_lanes(x):
    acc = x[:, :LANES]
    for c in range(1, x.shape[1] // LANES):
        acc = acc + x[:, c * LANES:(c + 1) * LANES]
    return acc


def _chunk_loop(n_ch, body, init):
    if isinstance(n_ch, int):
        carry = init
        for c in range(n_ch):
            carry = body(c, carry)
        return carry
    return lax.fori_loop(0, n_ch, body, init)


def _count(sc_ref, n_ch, pred):
    rows = sc_ref.shape[1]

    def body(c, acc):
        return acc + _fold_lanes(jnp.where(pred(sc_ref[c], c), 1.0, 0.0))

    acc = _chunk_loop(n_ch, body, jnp.zeros((rows, LANES), F32))
    return jnp.sum(acc, axis=-1, keepdims=True)


def _select_threshold(sc_ref, kk, n_ch):
    rows, ch = sc_ref.shape[1], sc_ref.shape[2]

    def minmax(c, carry):
        mn, mx = carry
        s = sc_ref[c]
        lo_c = jnp.where(s > -jnp.inf, s, jnp.inf)
        for j in range(ch // LANES):
            sl = slice(j * LANES, (j + 1) * LANES)
            mn = jnp.minimum(mn, lo_c[:, sl])
            mx = jnp.maximum(mx, s[:, sl])
        return mn, mx

    mn, mx = _chunk_loop(n_ch, minmax, (jnp.full((rows, LANES), jnp.inf, F32),
                                         jnp.full((rows, LANES), -jnp.inf, F32)))
    lo = jnp.min(mn, axis=-1, keepdims=True)
    hi = jnp.max(mx, axis=-1, keepdims=True)
    c_lo = _count(sc_ref, n_ch, lambda s, c: s >= lo)
    c_hi = _count(sc_ref, n_ch, lambda s, c: s >= hi)
    top_tie = c_hi >= kk
    lo = jnp.where(top_tie, hi, lo)
    c_lo = jnp.where(top_tie, c_hi, c_lo)
    done = jnp.where(top_tie | (c_lo == kk), 1.0, 0.0)

    def cond(st):
        return (jnp.min(st[3]) < 0.5) & (st[4] < MAX_BISECTIONS)

    def body(st):
        lo, hi, c_lo, done, it = st
        mid = lo * 0.5 + hi * 0.5
        stuck = (mid <= lo) | (mid >= hi)
        c_mid = _count(sc_ref, n_ch, lambda s, c: s >= mid)
        live = (done < 0.5) & jnp.logical_not(stuck)
        up = live & (c_mid >= kk)
        dn = live & (c_mid < kk)
        lo = jnp.where(up, mid, lo)
        c_lo = jnp.where(up, c_mid, c_lo)
        hi = jnp.where(dn, mid, hi)
        done = jnp.where(stuck | (c_lo == kk), 1.0, done)
        return lo, hi, c_lo, done, it + 1

    theta, _, c_theta, _, _ = lax.while_loop(cond, body, (lo, hi, c_lo, done, jnp.int32(0)))

    n_cols = n_ch * ch
    tie = c_theta > kk
    no_tie_jmax = jnp.full((rows, 1), 2.0**30, F32)

    def col_of(c):
        return (c * ch + lax.broadcasted_iota(jnp.int32, (1, ch), 1)).astype(F32)

    def resolve(_):
        need = kk - _count(sc_ref, n_ch, lambda s, c: s > theta)
        n_iter = max(1, (sc_ref.shape[0] * ch - 1).bit_length())

        def step(_, st):
            jl, jh = st
            jm = jnp.floor((jl + jh) * 0.5)
            c_m = _count(sc_ref, n_ch, lambda s, c: (s == theta) & (col_of(c) <= jm))
            ok = c_m >= need
            return jnp.where(ok, jl, jm + 1.0), jnp.where(ok, jm, jh)

        jl0 = jnp.zeros((rows, 1), F32)
        jh0 = jnp.zeros((rows, 1), F32) + (n_cols - 1)
        _, jh = lax.fori_loop(0, n_iter, step, (jl0, jh0))
        return jnp.where(tie, jh, no_tie_jmax)

    any_tie = jnp.max(jnp.where(tie, 1.0, 0.0)) > 0.5
    jmax = lax.cond(any_tie, resolve, lambda _: no_tie_jmax, 0)
    return theta, jmax


def _scores_to_bias(sc_ref, n_ch, theta, jmax):
    ch = sc_ref.shape[2]

    def body(c, carry):
        s = sc_ref[c]
        col = (c * ch + lax.broadcasted_iota(jnp.int32, (1, ch), 1)).astype(F32)
        sel = (s > theta) | ((s == theta) & (col <= jmax))
        sc_ref[c] = jnp.where(sel, 0.0, NEG)
        return carry

    _chunk_loop(n_ch, body, 0)


def _dsa_prompt_body(q_ref, qi_ref, wi_ref, ki_ref, kn_ref, v_ref, gq_ref, o_ref,
                     sc_ref, wb_ref, m_ref, l_ref, acc_ref, *, tq, n_top, n_idx_heads, idx_dim, n_kv, group, head_dim):
    ch = sc_ref.shape[2]
    t0 = pl.program_id(1) * tq
    n_ch = (t0 + tq + ch - 1) // ch
    rows = t0 + lax.broadcasted_iota(jnp.int32, (tq, 1), 0)

    qi = qi_ref[0]
    wi = wi_ref[0]
    qi_h = [qi[:, h * idx_dim:(h + 1) * idx_dim].astype(BF16) for h in range(n_idx_heads)]
    for h in range(n_idx_heads):
        wb_ref[h] = jnp.broadcast_to(wi[:, h:h + 1], (tq, LANES))

    def score_chunk(c, carry):
        ki = ki_ref[0, pl.ds(pl.multiple_of(c * ch, ch), ch), :]
        acc = jnp.zeros((tq, ch), F32)
        for h in range(n_idx_heads):
            s = lax.dot_general(qi_h[h], ki, NT_DIMS, preferred_element_type=F32)
            acc = acc + jnp.maximum(s, 0.0) * jnp.tile(wb_ref[h], (1, ch // LANES))
        cols = c * ch + lax.broadcasted_iota(jnp.int32, (1, ch), 1)
        sc_ref[c] = jnp.where(cols <= rows, acc, -jnp.inf)
        return carry

    lax.fori_loop(0, n_ch, score_chunk, 0)

    kk = jnp.minimum(rows + 1, n_top).astype(F32)
    theta, jmax = _select_threshold(sc_ref, kk, n_ch)
    _scores_to_bias(sc_ref, n_ch, theta, jmax)

    scale = head_dim ** -0.5
    q = q_ref[0]
    for h in range(n_kv):
        qs = [(_rms(q[:, (h * group + g) * head_dim:(h * group + g + 1) * head_dim], gq_ref[...]) * scale
               ).astype(BF16) for g in range(group)]
        qst = jnp.concatenate(qs, axis=0)
        m_ref[...] = jnp.full(m_ref.shape, NEG, F32)
        l_ref[...] = jnp.zeros(l_ref.shape, F32)
        acc_ref[...] = jnp.zeros(acc_ref.shape, F32)
        hs = slice(h * head_dim, (h + 1) * head_dim)

        def att_chunk(c, carry, hs=hs, qst=qst):
            off = pl.multiple_of(c * ch, ch)
            k = kn_ref[0, pl.ds(off, ch), hs]
            v = v_ref[0, pl.ds(off, ch), hs]
            bias = sc_ref[c]
            lg = lax.dot_general(qst, k, NT_DIMS, preferred_element_type=F32)
            lg = lg + jnp.concatenate([bias] * group, axis=0)
            m_prev = m_ref[...]
            m_new = jnp.maximum(m_prev, jnp.max(lg, axis=-1, keepdims=True))
            alpha = jnp.exp(m_prev - m_new)
            p = jnp.exp(lg - m_new)
            l_ref[...] = alpha * l_ref[...] + jnp.sum(p, axis=-1, keepdims=True)
            acc_ref[...] = alpha * acc_ref[...] + jnp.dot(p.astype(BF16), v, preferred_element_type=F32)
            m_ref[...] = m_new
            return carry

        lax.fori_loop(0, n_ch, att_chunk, 0)
        out = acc_ref[...] / l_ref[...]
        for g in range(group):
            o_ref[0, :, (h * group + g) * head_dim:(h * group + g + 1) * head_dim] = (
                out[g * tq:(g + 1) * tq].astype(o_ref.dtype))


def dsa_prompt(p, ki, kv, g_q, *, q_block, qi_block, wi_block, n_top, n_idx_heads, idx_dim, n_kv, group,
               head_dim, tq=128, ch=512):
    b, s, _ = p.shape
    qw = n_kv * group * head_dim
    kw = n_kv * head_dim
    iw = n_idx_heads * idx_dim
    body = functools.partial(_dsa_prompt_body, tq=tq, n_top=n_top, n_idx_heads=n_idx_heads, idx_dim=idx_dim,
                             n_kv=n_kv, group=group, head_dim=head_dim)
    return pl.pallas_call(
        body,
        grid=(b, s // tq),
        in_specs=[pl.BlockSpec((1, tq, qw), lambda i, j: (i, j, q_block)),
                  pl.BlockSpec((1, tq, iw), lambda i, j: (i, j, qi_block)),
                  pl.BlockSpec((1, tq, LANES), lambda i, j: (i, j, wi_block)),
                  pl.BlockSpec((1, s, idx_dim), lambda i, j: (i, 0, 0)),
                  pl.BlockSpec((1, s, kw), lambda i, j: (i, 0, 0)),
                  pl.BlockSpec((1, s, kw), lambda i, j: (i, 0, 1)),
                  pl.BlockSpec((1, head_dim), lambda i, j: (0, 0))],
        out_specs=pl.BlockSpec((1, tq, qw), lambda i, j: (i, j, 0)),
        out_shape=jax.ShapeDtypeStruct((b, s, qw), BF16),
        scratch_shapes=[pltpu.VMEM((s // ch, tq, ch), F32),
                        pltpu.VMEM((n_idx_heads, tq, LANES), F32),
                        pltpu.VMEM((group * tq, 1), F32),
                        pltpu.VMEM((group * tq, 1), F32),
                        pltpu.VMEM((group * tq, head_dim), F32)],
        compiler_params=_params("parallel", "arbitrary"),
        name="dsa_prompt",
    )(p, p, p, ki, kv, kv, g_q.reshape(1, head_dim))


def _dsa_sample_body(pt_ref, p_ref, kvn_ref, gq_ref, *refs, n_pages, page, n_top, n_idx_heads, idx_dim, n_kv,
                     group, head_dim, q_off, qi_off, ki_off, wi_off):
    del pt_ref
    kidx_refs = refs[:n_pages]
    kv_refs = refs[n_pages:2 * n_pages]
    o_ref, sc_ref = refs[2 * n_pages], refs[2 * n_pages + 1]
    t = p_ref.shape[0]
    kw = n_kv * head_dim

    qi = p_ref[:, qi_off:qi_off + n_idx_heads * idx_dim]
    ki_new = p_ref[:, ki_off:ki_off + idx_dim]
    wi = p_ref[:, wi_off:wi_off + LANES]
    qi_st = jnp.concatenate([qi[:, h * idx_dim:(h + 1) * idx_dim] for h in range(n_idx_heads)], axis=0).astype(BF16)
    w_col = jnp.concatenate([wi[:, h:h + 1] for h in range(n_idx_heads)], axis=0)

    def idx_scores(k_block):
        s = lax.dot_general(qi_st, k_block, NT_DIMS, preferred_element_type=F32)
        s = jnp.maximum(s, 0.0) * w_col
        out = s[0:t]
        for h in range(1, n_idx_heads):
            out = out + s[h * t:(h + 1) * t]
        return out

    for c in range(n_pages):
        sc_ref[c] = idx_scores(kidx_refs[c][...].astype(BF16))
    own = idx_scores(_pad_rows(ki_new, page).astype(BF16))
    tok = lax.broadcasted_iota(jnp.int32, (t, page), 0)
    col = lax.broadcasted_iota(jnp.int32, (t, page), 1)
    sc_ref[n_pages] = jnp.where(col <= tok, own, -jnp.inf)

    n_ch = n_pages + 1
    theta, jmax = _select_threshold(sc_ref, jnp.full((t, 1), float(n_top), F32), n_ch)
    _scores_to_bias(sc_ref, n_ch, theta, jmax)

    scale = head_dim ** -0.5
    for h in range(n_kv):
        hs = slice(h * head_dim, (h + 1) * head_dim)
        vs = slice(kw + h * head_dim, kw + (h + 1) * head_dim)
        qs = [(_rms(p_ref[:, q_off + (h * group + g) * head_dim:q_off + (h * group + g + 1) * head_dim],
                    gq_ref[...]) * scale).astype(BF16) for g in range(group)]
        qst = jnp.concatenate(qs, axis=0)
        lgs = []
        for c in range(n_ch):
            k = kv_refs[c][:, hs] if c < n_pages else _pad_rows(kvn_ref[:, hs], page)
            lg = lax.dot_general(qst, k.astype(BF16), NT_DIMS, preferred_element_type=F32)
            lgs.append(lg + jnp.concatenate([sc_ref[c]] * group, axis=0))
        m = lgs[0].max(axis=-1, keepdims=True)
        for lg in lgs[1:]:
            m = jnp.maximum(m, lg.max(axis=-1, keepdims=True))
        l = jnp.zeros((group * t, 1), F32)
        acc = jnp.zeros((group * t, head_dim), F32)
        for c in range(n_ch):
            pc = jnp.exp(lgs[c] - m)
            l = l + jnp.sum(pc, axis=-1, keepdims=True)
            v = kv_refs[c][:, vs] if c < n_pages else _pad_rows(kvn_ref[:, vs], page)
            acc = acc + jnp.dot(pc.astype(BF16), v.astype(BF16), preferred_element_type=F32)
        out = acc / l
        for g in range(group):
            o_ref[:, (h * group + g) * head_dim:(h * group + g + 1) * head_dim] = (
                out[g * t:(g + 1) * t].astype(o_ref.dtype))


def dsa_sample(p, kv_new, kidx_pool, kv_pool, page_table, g_q, *, t, n_top, n_idx_heads, idx_dim, n_kv, group,
               head_dim, q_off, qi_off, ki_off, wi_off):
    bd, n_pages = page_table.shape
    page = kidx_pool.shape[1]
    qw = n_kv * group * head_dim
    kw = n_kv * head_dim
    n = p.shape[1]
    body = functools.partial(_dsa_sample_body, n_pages=n_pages, page=page, n_top=n_top, n_idx_heads=n_idx_heads,
                             idx_dim=idx_dim, n_kv=n_kv, group=group, head_dim=head_dim, q_off=q_off,
                             qi_off=qi_off, ki_off=ki_off, wi_off=wi_off)
    in_specs = [pl.BlockSpec((t, n), lambda b, pt: (b, 0)),
                pl.BlockSpec((t, 2 * kw), lambda b, pt: (b, 0)),
                pl.BlockSpec((1, head_dim), lambda b, pt: (0, 0))]
    in_specs += [pl.BlockSpec((None, page, idx_dim), functools.partial(lambda b, pt, c: (pt[b, c], 0, 0), c=c))
                 for c in range(n_pages)]
    in_specs += [pl.BlockSpec((None, page, 2 * kw), functools.partial(lambda b, pt, c: (pt[b, c], 0, 0), c=c))
                 for c in range(n_pages)]
    return pl.pallas_call(
        body,
        grid_spec=pltpu.PrefetchScalarGridSpec(
            num_scalar_prefetch=1,
            grid=(bd,),
            in_specs=in_specs,
            out_specs=pl.BlockSpec((t, qw), lambda b, pt: (b, 0)),
            scratch_shapes=[pltpu.VMEM((n_pages + 1, t, page), F32)]),
        out_shape=jax.ShapeDtypeStruct((bd * t, qw), F32),
        compiler_params=_params("arbitrary"),
        name="dsa_sample",
    )(page_table, p, kv_new, g_q.reshape(1, head_dim), *([kidx_pool] * n_pages), *([kv_pool] * n_pages))


def _dilated_mask(rel, window, dil):
    return (rel >= 0) & (rel <= window) & ((rel & (dil - 1)) == 0)


def _dil_prompt_body(q_ref, k_ref, v_ref, gq_ref, o_ref, lse_ref, m_ref, l_ref, acc_ref, *, window, dil, n_heads,
                     head_dim):
    tq = q_ref.shape[1]
    i, kc, nkc = pl.program_id(1), pl.program_id(2), pl.num_programs(2)
    cidx = i - (nkc - 1) + kc

    @pl.when(kc == 0)
    def _():
        m_ref[...] = jnp.full(m_ref.shape, NEG, F32)
        l_ref[...] = jnp.zeros(l_ref.shape, F32)
        acc_ref[...] = jnp.zeros(acc_ref.shape, F32)

    @pl.when(cidx >= 0)
    def _():
        rel = ((i - cidx) * tq + lax.broadcasted_iota(jnp.int32, (tq, tq), 0)
               - lax.broadcasted_iota(jnp.int32, (tq, tq), 1))
        bias = jnp.where(_dilated_mask(rel, window, dil), 0.0, NEG)
        scale = head_dim ** -0.5
        for h in range(n_heads):
            hs = slice(h * head_dim, (h + 1) * head_dim)
            qn = (_rms(q_ref[0, :, hs], gq_ref[...]) * scale).astype(BF16)
            lg = lax.dot_general(qn, k_ref[0, :, hs].astype(BF16), NT_DIMS, preferred_element_type=F32) + bias
            m_prev = m_ref[h]
            m_new = jnp.maximum(m_prev, jnp.max(lg, axis=-1, keepdims=True))
            alpha = jnp.exp(m_prev - m_new)
            p = jnp.exp(lg - m_new)
            l_ref[h] = alpha * l_ref[h] + jnp.sum(p, axis=-1, keepdims=True)
            acc_ref[h] = alpha * acc_ref[h] + jnp.dot(p.astype(BF16), v_ref[0, :, hs].astype(BF16),
                                                      preferred_element_type=F32)
            m_ref[h] = m_new

    @pl.when(kc == nkc - 1)
    def _():
        for h in range(n_heads):
            hs = slice(h * head_dim, (h + 1) * head_dim)
            o_ref[0, :, hs] = acc_ref[h] / l_ref[h]
            lse_ref[0, :, hs] = jnp.broadcast_to(m_ref[h] + jnp.log(l_ref[h]), (tq, head_dim))


def dilated_prompt_group(p, kn, g_q, *, q_block, k_block, v_block, window, dil, n_heads, head_dim, tq=512):
    b, s, _ = p.shape
    w = n_heads * head_dim
    nkc = -(-window // tq) + 1

    def kv_map(blk):
        return lambda bi, i, kc: (bi, jnp.maximum(i - (nkc - 1) + kc, 0), blk)

    return pl.pallas_call(
        functools.partial(_dil_prompt_body, window=window, dil=dil, n_heads=n_heads, head_dim=head_dim),
        grid=(b, s // tq, nkc),
        in_specs=[pl.BlockSpec((1, tq, w), lambda bi, i, kc: (bi, i, q_block)),
                  pl.BlockSpec((1, tq, w), kv_map(k_block)),
                  pl.BlockSpec((1, tq, w), kv_map(v_block)),
                  pl.BlockSpec((1, head_dim), lambda bi, i, kc: (0, 0))],
        out_specs=[pl.BlockSpec((1, tq, w), lambda bi, i, kc: (bi, i, 0)),
                   pl.BlockSpec((1, tq, w), lambda bi, i, kc: (bi, i, 0))],
        out_shape=[jax.ShapeDtypeStruct((b, s, w), F32), jax.ShapeDtypeStruct((b, s, w), F32)],
        scratch_shapes=[pltpu.VMEM((n_heads, tq, 1), F32), pltpu.VMEM((n_heads, tq, 1), F32),
                        pltpu.VMEM((n_heads, tq, head_dim), F32)],
        compiler_params=_params("parallel", "parallel", "arbitrary"),
        name="dilated_prompt",
    )(p, kn, p, g_q.reshape(1, head_dim))


def _dil_sample_body(p_ref, kn_ref, buf_ref, gq_ref, o_ref, lse_ref, nbuf_ref, *, window, dil, n_heads, head_dim,
                     q_off, k_off, v_off):
    t = p_ref.shape[0]
    wb = buf_ref.shape[1]
    w = n_heads * head_dim
    k_new = kn_ref[:, k_off:k_off + w]
    v_new = p_ref[:, v_off:v_off + w]
    nbuf_ref[0, 0:wb - t, :] = buf_ref[0, t:wb, :]
    nbuf_ref[0, wb - t:wb, 0:w] = k_new
    nbuf_ref[0, wb - t:wb, w:2 * w] = v_new

    rel = wb + lax.broadcasted_iota(jnp.int32, (t, wb), 0) - lax.broadcasted_iota(jnp.int32, (t, wb), 1)
    bias = jnp.where(_dilated_mask(rel, window, dil), 0.0, NEG)
    rel_o = lax.broadcasted_iota(jnp.int32, (t, LANES), 0) - lax.broadcasted_iota(jnp.int32, (t, LANES), 1)
    bias_o = jnp.where(_dilated_mask(rel_o, window, dil), 0.0, NEG)
    scale = head_dim ** -0.5
    for h in range(n_heads):
        hs = slice(h * head_dim, (h + 1) * head_dim)
        vs = slice(w + h * head_dim, w + (h + 1) * head_dim)
        qn = (_rms(p_ref[:, q_off + h * head_dim:q_off + (h + 1) * head_dim], gq_ref[...]) * scale).astype(BF16)
        lg = lax.dot_general(qn, buf_ref[0, :, hs].astype(BF16), NT_DIMS, preferred_element_type=F32) + bias
        lg_o = lax.dot_general(qn, _pad_rows(k_new[:, hs], LANES).astype(BF16), NT_DIMS,
                               preferred_element_type=F32) + bias_o
        m = jnp.maximum(jnp.max(lg, axis=-1, keepdims=True), jnp.max(lg_o, axis=-1, keepdims=True))
        pp = jnp.exp(lg - m)
        pp_o = jnp.exp(lg_o - m)
        l = jnp.sum(pp, axis=-1, keepdims=True) + jnp.sum(pp_o, axis=-1, keepdims=True)
        acc = (jnp.dot(pp.astype(BF16), buf_ref[0, :, vs].astype(BF16), preferred_element_type=F32)
               + jnp.dot(pp_o.astype(BF16), _pad_rows(v_new[:, hs], LANES).astype(BF16),
                         preferred_element_type=F32))
        o_ref[:, hs] = acc / l
        lse_ref[:, hs] = jnp.broadcast_to(m + jnp.log(l), (t, head_dim))


def dilated_sample_group(p, kn, buf, g_q, *, t, window, dil, n_heads, head_dim, q_off, k_off, v_off):
    bd, wb, w2 = buf.shape
    w = n_heads * head_dim
    n, nk = p.shape[1], kn.shape[1]
    return pl.pallas_call(
        functools.partial(_dil_sample_body, window=window, dil=dil, n_heads=n_heads, head_dim=head_dim,
                          q_off=q_off, k_off=k_off, v_off=v_off),
        grid=(bd,),
        in_specs=[pl.BlockSpec((t, n), lambda b: (b, 0)),
                  pl.BlockSpec((t, nk), lambda b: (b, 0)),
                  pl.BlockSpec((1, wb, w2), lambda b: (b, 0, 0)),
                  pl.BlockSpec((1, head_dim), lambda b: (0, 0))],
        out_specs=[pl.BlockSpec((t, w), lambda b: (b, 0)),
                   pl.BlockSpec((t, w), lambda b: (b, 0)),
                   pl.BlockSpec((1, wb, w2), lambda b: (b, 0, 0))],
        out_shape=[jax.ShapeDtypeStruct((bd * t, w), F32), jax.ShapeDtypeStruct((bd * t, w), F32),
                   jax.ShapeDtypeStruct((bd, wb, w2), F32)],
        compiler_params=_params("parallel"),
        name="dilated_sample",
    )(p, kn, buf, g_q.reshape(1, head_dim))


def _merge_groups_body(*refs):
    n = (len(refs) - 1) // 2
    o_refs, lse_refs, out_ref = refs[:n], refs[n:2 * n], refs[2 * n]
    lses = [r[...] for r in lse_refs]
    m = lses[0]
    for x in lses[1:]:
        m = jnp.maximum(m, x)
    ws = [jnp.exp(x - m) for x in lses]
    den = ws[0]
    for x in ws[1:]:
        den = den + x
    acc = ws[0] * o_refs[0][...]
    for wgt, o in zip(ws[1:], o_refs[1:]):
        acc = acc + wgt * o[...]
    out_ref[...] = (acc / den).astype(out_ref.dtype)


def merge_groups(outs, lses):
    m, w = outs[0].shape
    tm = min(m, ROW_TILE)
    spec = pl.BlockSpec((tm, w), lambda i: (i, 0))
    return pl.pallas_call(
        _merge_groups_body,
        grid=(m // tm,),
        in_specs=[spec] * (2 * len(outs)),
        out_specs=spec,
        out_shape=jax.ShapeDtypeStruct((m, w), BF16),
        compiler_params=_params("parallel"),
        name="merge_groups",
    )(*outs, *lses)


def _pool_body(u_ref, halo_ref, wg_ref, sc_ref, o_ref, ext_ref, *, n_pre, windows, halo_is_history):
    nb, t, d = u_ref.shape
    gw = d // len(windows)
    i = pl.program_id(1)
    halo = halo_ref[...]
    if halo_is_history:
        halo = jnp.where(i == 0, 0.0, halo)
    ext_ref[:, 0:POOL_HALO, :] = halo
    ext_ref[:, POOL_HALO:POOL_HALO + t, :] = u_ref[...]
    pos = i * t + lax.broadcasted_iota(jnp.int32, (1, t, 1), 1)
    for g, w in enumerate(windows):
        cs = slice(g * gw, (g + 1) * gw)
        acc = ext_ref[:, POOL_HALO:POOL_HALO + t, cs]
        for j in range(1, w):
            acc = acc + ext_ref[:, POOL_HALO - j:POOL_HALO - j + t, cs]
        div = jnp.minimum(w, n_pre + 1 + pos).astype(F32)
        r = acc / div - u_ref[:, :, cs]
        y = jnp.dot(r.reshape(nb * t, gw).astype(BF16), wg_ref[g], preferred_element_type=F32) * sc_ref[:, cs]
        o_ref[:, :, cs] = y.reshape(nb, t, gw).astype(o_ref.dtype)


def pool_mix(u, halo, w_grp, scale, *, n_pre, windows, batch_block, t_block, halo_is_history, out_dtype):
    b, t, d = u.shape
    hb = t_block // POOL_HALO
    if halo_is_history:
        halo_map = lambda bi, i: (bi, jnp.maximum(i * hb - 1, 0), 0)
    else:
        halo_map = lambda bi, i: (bi, 0, 0)
    return pl.pallas_call(
        functools.partial(_pool_body, n_pre=n_pre, windows=windows, halo_is_history=halo_is_history),
        grid=(b // batch_block, t // t_block),
        in_specs=[pl.BlockSpec((batch_block, t_block, d), lambda bi, i: (bi, i, 0)),
                  pl.BlockSpec((batch_block, POOL_HALO, d), halo_map),
                  pl.BlockSpec(w_grp.shape, lambda bi, i: (0, 0, 0)),
                  pl.BlockSpec((1, d), lambda bi, i: (0, 0))],
        out_specs=pl.BlockSpec((batch_block, t_block, d), lambda bi, i: (bi, i, 0)),
        out_shape=jax.ShapeDtypeStruct((b, t, d), out_dtype),
        scratch_shapes=[pltpu.VMEM((batch_block, POOL_HALO + t_block, d), F32)],
        compiler_params=_params("parallel", "parallel"),
        name="pool_mix",
    )(u, halo, w_grp, scale.reshape(1, d))


def _mlstm_body(q_ref, k_ref, v_ref, op_ref, gi_ref, gf_ref, bi_ref, bf_ref, gh_ref, c0_ref, n0_ref, m0_ref,
                y_ref, c_ref, n_ref, m_ref, cs, ns, ms, *, n_heads, head_dim):
    L = q_ref.shape[1]
    ls = max(L, LANES)
    ci = pl.program_id(1)
    hp = lax.Precision.HIGHEST

    @pl.when(ci == 0)
    def _():
        cs[...] = c0_ref[0]
        ns[...] = n0_ref[0]
        ms[...] = m0_ref[0]

    li = gi_ref[0] + bi_ref[...]
    lf = jax.nn.log_sigmoid(gf_ref[0] + bf_ref[...])
    row = lax.broadcasted_iota(jnp.int32, (L, ls), 0)
    col = lax.broadcasted_iota(jnp.int32, (L, ls), 1)
    causal = col <= row
    bt = jnp.dot(jnp.where(causal, 1.0, 0.0), _pad_rows(lf, ls), preferred_element_type=F32, precision=hp)
    lane = lax.broadcasted_iota(jnp.int32, (L, LANES), 1)
    lane1 = lax.broadcasted_iota(jnp.int32, (1, LANES), 1)
    m_all = ms[...]
    for h in range(n_heads):
        hs = slice(h * head_dim, (h + 1) * head_dim)
        bt_h = bt[:, h:h + 1]
        li_h = li[:, h:h + 1]
        m_prev = m_all[:, h:h + 1]
        a_h = jnp.where(lane == 0, bt_h, jnp.where(lane == 1, 1.0, 0.0))
        b_h = jnp.where(lane == 0, 1.0, jnp.where(lane == 1, li_h - bt_h, 0.0))
        dmat = lax.dot_general(a_h, _pad_rows(b_h, ls), NT_DIMS, preferred_element_type=F32, precision=hp)
        dmat = jnp.where(causal, dmat, -jnp.inf)
        g = bt_h + m_prev
        m_t = jnp.maximum(g, jnp.max(dmat, axis=-1, keepdims=True))
        w_intra = jnp.exp(dmat - m_t)
        w_inter = jnp.exp(g - m_t)
        q = q_ref[0, :, hs]
        k = k_ref[0, :, hs] * head_dim ** -0.5
        v = v_ref[0, :, hs]
        qb, kb, vb = q.astype(BF16), _pad_rows(k, ls).astype(BF16), _pad_rows(v, ls).astype(BF16)
        qk = lax.dot_general(qb, kb, NT_DIMS, preferred_element_type=F32) * w_intra
        c_h = cs[h]
        num = (w_inter * jnp.dot(qb, c_h.astype(BF16), preferred_element_type=F32)
               + jnp.dot(qk.astype(BF16), vb, preferred_element_type=F32))
        n_h = ns[h:h + 1, :]
        den = w_inter * jnp.sum(q * n_h, axis=-1, keepdims=True) + jnp.sum(qk, axis=-1, keepdims=True)
        hout = num / jnp.maximum(jnp.abs(den), jnp.exp(-m_t))
        b_last = bt_h[L - 1:L, :]
        decay = b_last - bt_h + li_h
        m_new = jnp.maximum(b_last + m_prev, jnp.max(decay, axis=0, keepdims=True))
        keep = jnp.exp(b_last + m_prev - m_new)
        kw = k * jnp.exp(decay - m_new)
        cs[h] = keep * c_h + lax.dot_general(_pad_rows(kw, ls).astype(BF16), vb, TN_DIMS,
                                             preferred_element_type=F32)
        ns[h:h + 1, :] = keep * n_h + jnp.sum(kw, axis=0, keepdims=True)
        m_all = jnp.where(lane1 == h, m_new, m_all)
        hn = _rms(hout, gh_ref[:, hs])
        y_ref[0, :, hs] = (jax.nn.sigmoid(op_ref[0, :, hs]) * hn).astype(y_ref.dtype)
    ms[...] = m_all

    @pl.when(ci == pl.num_programs(1) - 1)
    def _():
        c_ref[0] = cs[...]
        n_ref[0] = ns[...]
        m_ref[0] = ms[...]


def mlstm(p, gates, b_i, b_f, g_h, c0, n0, m0, *, n_heads, chunk, out_dtype):
    b, t, w4 = p.shape
    w = w4 // 4
    head_dim = w // n_heads
    nc = t // chunk

    def lane_row(x):
        return jnp.pad(x.reshape(1, -1), ((0, 0), (0, LANES - x.size)))

    m0p = jnp.pad(m0.reshape(b, 1, n_heads), ((0, 0), (0, 0), (0, LANES - n_heads)))
    colspec = lambda blk, wd: pl.BlockSpec((1, chunk, wd), lambda bi, ci: (bi, ci, blk))
    const = lambda shape: pl.BlockSpec(shape, lambda bi, ci: (0,) * len(shape))
    state = lambda shape: pl.BlockSpec((1,) + shape, lambda bi, ci: (bi,) + (0,) * len(shape))
    y, c, n, m = pl.pallas_call(
        functools.partial(_mlstm_body, n_heads=n_heads, head_dim=head_dim),
        grid=(b, nc),
        in_specs=[colspec(0, w), colspec(1, w), colspec(2, w), colspec(3, w),
                  colspec(0, LANES), colspec(1, LANES),
                  const((1, LANES)), const((1, LANES)), const((1, w)),
                  state((n_heads, head_dim, head_dim)), state((n_heads, head_dim)), state((1, LANES))],
        out_specs=[colspec(0, w), state((n_heads, head_dim, head_dim)), state((n_heads, head_dim)),
                   state((1, LANES))],
        out_shape=[jax.ShapeDtypeStruct((b, t, w), out_dtype),
                   jax.ShapeDtypeStruct((b, n_heads, head_dim, head_dim), F32),
                   jax.ShapeDtypeStruct((b, n_heads, head_dim), F32),
                   jax.ShapeDtypeStruct((b, 1, LANES), F32)],
        scratch_shapes=[pltpu.VMEM((n_heads, head_dim, head_dim), F32),
                        pltpu.VMEM((n_heads, head_dim), F32),
                        pltpu.VMEM((1, LANES), F32)],
        compiler_params=_params("parallel", "arbitrary"),
        name="mlstm",
    )(p, p, p, p, gates, gates, lane_row(b_i), lane_row(b_f), g_h.reshape(1, w), c0, n0, m0p)
    return y, c, n, m[:, 0, :n_heads]


def kernel(x_prompt, x_sample, cache_kv_a, cache_kidx_a, page_table, cache_kv_b0, cache_kv_b1, cache_kv_b2,
           state_pool_c, state_c_d, state_n_d, state_m_d, cache_mem_kv, mem_prompt,
           g_norm_mix, g_norm_mem, g_norm_ffn, g_mem_src, w_mem_q, w_mem_kv, g_mem_q, g_mem_k, w_mem_o,
           w_ff1, w_ff2, w_in_a, g_q_a, g_k_a, w_out_a, w_in_b, g_q_b, g_k_b, w_out_b,
           w_in_c, w_grp_c, scale_c, w_out_c, w_in_d, b_i_d, b_f_d, g_h_d, w_out_d):
    bp, s, d = x_prompt.shape
    bd, t, _ = x_sample.shape
    depth = g_norm_mix.shape[0]
    n_mixers = 4
    xp = x_prompt.reshape(bp * s, d)
    xs = x_sample.reshape(bd * t, d)
    cache_b = (cache_kv_b0, cache_kv_b1, cache_kv_b2)
    dilated = ((128, 1), (512, 4), (2048, 16))
    pool_windows = (2, 4, 8, 16)
    n_mem, n_heads_mem = mem_prompt.shape[1], cache_mem_kv.shape[4]
    mem2d = mem_prompt.reshape(bp * n_mem, d)

    kv_a_p, ki_a_p, kv_a_s, ki_a_s = [], [], [], []
    kv_b_p = [[] for _ in dilated]
    kv_b_s = [[] for _ in dilated]
    pool_p, pool_s = [], []
    cd_p, nd_p, md_p, cd_s, nd_s, md_s = [], [], [], [], [], []
    mem_p = []

    for i in range(depth):
        kind, j = i % n_mixers, i // n_mixers
        g_mix = g_norm_mix[i]
        if kind == 0:
            n_kv, hd = cache_kv_a.shape[4], cache_kv_a.shape[5]
            idx_dim = cache_kidx_a.shape[3]
            qw = w_out_a.shape[1]
            group = qw // (n_kv * hd)
            kw = n_kv * hd
            n_idx = (w_in_a.shape[2] - qw - 2 * kw - idx_dim) // (idx_dim + 1)
            iw = n_idx * idx_dim
            w = w_in_a[j]
            o3 = qw + 2 * kw
            ki_off = o3 + iw
            wi_off = ki_off + LANES
            n_cols = -(-(wi_off + LANES) // 512) * 512
            wa = jnp.concatenate([w[:, :o3 + iw], _pad_cols(w[:, o3 + iw:o3 + iw + idx_dim], LANES),
                                  _pad_cols(w[:, o3 + iw + idx_dim:], n_cols - wi_off)], axis=1).astype(BF16)
            wo = w_out_a[j].astype(BF16)
            pp = rms_matmul(xp, g_mix, wa)
            ps = rms_matmul(xs, g_mix, wa)
            kvp, kvp_bf = head_norm(pp, qw // kw, kw, hd, g_k_a[j], v_block=qw // kw + 1, with_bf16=True)
            kvs = head_norm(ps, qw // kw, kw, hd, g_k_a[j], v_block=qw // kw + 1)
            kip = pp[:, ki_off:ki_off + idx_dim]
            kis = ps[:, ki_off:ki_off + idx_dim]
            kv_a_p.append(kvp.reshape(bp, s, 2, n_kv, hd))
            ki_a_p.append(kip.reshape(bp, s, idx_dim))
            kv_a_s.append(kvs.reshape(bd, t, 2, n_kv, hd))
            ki_a_s.append(kis.reshape(bd, t, idx_dim))
            dims = dict(n_idx_heads=n_idx, idx_dim=idx_dim, n_kv=n_kv, group=group, head_dim=hd)
            op = dsa_prompt(pp.reshape(bp, s, n_cols), kip.astype(BF16).reshape(bp, s, idx_dim),
                            kvp_bf.reshape(bp, s, 2 * kw), g_q_a[j], q_block=0, qi_block=o3 // iw,
                            wi_block=wi_off // LANES, n_top=min(256, s // 4), **dims)
            n_pool, page = cache_kidx_a.shape[1], cache_kidx_a.shape[2]
            past = page_table.shape[1] * page
            os_ = dsa_sample(ps, kvs, cache_kidx_a[j], cache_kv_a[j].reshape(n_pool, page, 2 * kw), page_table,
                             g_q_a[j], t=t, n_top=min(256, (past + t) // 4), q_off=0, qi_off=o3, ki_off=ki_off,
                             wi_off=wi_off, **dims)
            xp = matmul_res(op.reshape(bp * s, qw), wo, xp)
            xs = matmul_res(os_, wo, xs)
        elif kind == 1:
            n_groups = len(dilated)
            hg, hd = cache_kv_b0.shape[4], cache_kv_b0.shape[5]
            gw = hg * hd
            wb_ = w_in_b[j].astype(BF16)
            wo = w_out_b[j].astype(BF16)
            pp = rms_matmul(xp, g_mix, wb_)
            ps = rms_matmul(xs, g_mix, wb_)
            gk_row = jnp.repeat(g_k_b[j], hg, axis=0).reshape(1, n_groups * gw)
            knp = head_norm_groups(pp, 1, n_groups * gw, gk_row)
            kns = head_norm_groups(ps, 1, n_groups * gw, gk_row)
            pp3 = pp.reshape(bp, s, -1)
            knp3 = knp.reshape(bp, s, -1)
            outs_p, lses_p, outs_s, lses_s = [], [], [], []
            for g, (win, dil) in enumerate(dilated):
                o, lse = dilated_prompt_group(pp3, knp3, g_q_b[j, g], q_block=g, k_block=g,
                                              v_block=2 * n_groups + g, window=win, dil=dil, n_heads=hg,
                                              head_dim=hd)
                outs_p.append(o.reshape(bp * s, gw))
                lses_p.append(lse.reshape(bp * s, gw))
                wb = min(win, s)
                kv_b_p[g].append(jnp.stack(
                    [knp3[:, s - wb:, g * gw:(g + 1) * gw].reshape(bp, wb, hg, hd),
                     pp3[:, s - wb:, (2 * n_groups + g) * gw:(2 * n_groups + g + 1) * gw].reshape(bp, wb, hg, hd)],
                    axis=2))
                buf = cache_b[g][j]
                o, lse, nbuf = dilated_sample_group(ps, kns, buf.reshape(bd, buf.shape[1], 2 * gw), g_q_b[j, g],
                                                    t=t, window=win, dil=dil, n_heads=hg, head_dim=hd,
                                                    q_off=g * gw, k_off=g * gw, v_off=(2 * n_groups + g) * gw)
                outs_s.append(o)
                lses_s.append(lse)
                kv_b_s[g].append(nbuf.reshape(buf.shape))
            xp = matmul_res(merge_groups(outs_p, lses_p), wo, xp)
            xs = matmul_res(merge_groups(outs_s, lses_s), wo, xs)
        elif kind == 2:
            wi_, wo = w_in_c[j].astype(BF16), w_out_c[j].astype(BF16)
            wg = w_grp_c[j].astype(BF16)
            up = rms_matmul(xp, g_mix, wi_).reshape(bp, s, d)
            us = rms_matmul(xs, g_mix, wi_).reshape(bd, t, d)
            n_state = state_pool_c.shape[2]
            yp = pool_mix(up, up, wg, scale_c[j], n_pre=0, windows=pool_windows, batch_block=1, t_block=512,
                          halo_is_history=True, out_dtype=BF16)
            halo = jnp.pad(state_pool_c[j], ((0, 0), (POOL_HALO - n_state, 0), (0, 0)))
            ys = pool_mix(us, halo, wg, scale_c[j], n_pre=n_state, windows=pool_windows, batch_block=min(bd, 32),
                          t_block=t, halo_is_history=False, out_dtype=F32)
            pool_p.append(up[:, s - n_state:])
            pool_s.append(jnp.concatenate([state_pool_c[j], us], axis=1)[:, -n_state:])
            xp = matmul_res(yp.reshape(bp * s, d), wo, xp)
            xs = matmul_res(ys.reshape(bd * t, d), wo, xs)
        else:
            nh, hd = state_c_d.shape[2], state_c_d.shape[3]
            wd = nh * hd
            w = w_in_d[j]
            w_main = w[:, :4 * wd].astype(BF16)
            w_gate = jnp.concatenate([_pad_cols(w[:, 4 * wd:4 * wd + nh], LANES),
                                      _pad_cols(w[:, 4 * wd + nh:], LANES)], axis=1)
            wo = w_out_d[j].astype(BF16)
            pp = rms_matmul(xp, g_mix, w_main).reshape(bp, s, 4 * wd)
            ps = rms_matmul(xs, g_mix, w_main).reshape(bd, t, 4 * wd)
            gp = rms_matmul(xp, g_mix, w_gate, precise=True).reshape(bp, s, 2 * LANES)
            gs = rms_matmul(xs, g_mix, w_gate, precise=True).reshape(bd, t, 2 * LANES)
            zeros = lambda *shape: jnp.zeros(shape, F32)
            yp, c, n, m = mlstm(pp, gp, b_i_d[j], b_f_d[j], g_h_d[j], zeros(bp, nh, hd, hd), zeros(bp, nh, hd),
                                zeros(bp, nh), n_heads=nh, chunk=256, out_dtype=BF16)
            cd_p.append(c)
            nd_p.append(n)
            md_p.append(m)
            ys, c, n, m = mlstm(ps, gs, b_i_d[j], b_f_d[j], g_h_d[j], state_c_d[j], state_n_d[j], state_m_d[j],
                                n_heads=nh, chunk=t, out_dtype=F32)
            cd_s.append(c)
            nd_s.append(n)
            md_s.append(m)
            xp = matmul_res(yp.reshape(bp * s, wd), wo, xp)
            xs = matmul_res(ys.reshape(bd * t, wd), wo, xs)

        wm = n_heads_mem * cache_mem_kv.shape[5]
        mkv_raw = rms_matmul(mem2d, g_mem_src[i], w_mem_kv[i].astype(BF16))
        mkv = head_norm(mkv_raw, 0, wm, wm // n_heads_mem, g_mem_k[i], v_block=1)
        mem_p.append(mkv.reshape(bp, n_mem, 2, n_heads_mem, wm // n_heads_mem))
        wq, wo = w_mem_q[i].astype(BF16), w_mem_o[i].astype(BF16)
        qp = rms_matmul(xp, g_norm_mem[i], wq).reshape(bp, s, wm)
        qs = rms_matmul(xs, g_norm_mem[i], wq).reshape(bd, t, wm)
        ap = mem_attn(qp, mkv.reshape(bp, n_mem, 2 * wm), g_mem_q[i], n_heads=n_heads_mem, batch_block=1,
                      q_block=512, out_dtype=BF16)
        as_ = mem_attn(qs, cache_mem_kv[i].reshape(bd, n_mem, 2 * wm), g_mem_q[i], n_heads=n_heads_mem,
                       batch_block=4, q_block=t, out_dtype=F32)
        xp = matmul_res(ap.reshape(bp * s, wm), wo, xp)
        xs = matmul_res(as_.reshape(bd * t, wm), wo, xs)

        w1, w2 = w_ff1[i].astype(BF16), w_ff2[i].astype(BF16)
        xp = ffn(xp, g_norm_ffn[i], w1, w2)
        xs = ffn(xs, g_norm_ffn[i], w1, w2)

    return (xp.reshape(bp, s, d), xs.reshape(bd, t, d),
            jnp.stack(kv_a_p), jnp.stack(ki_a_p), jnp.stack(kv_a_s), jnp.stack(ki_a_s),
            jnp.stack(kv_b_p[0]), jnp.stack(kv_b_p[1]), jnp.stack(kv_b_p[2]),
            jnp.stack(kv_b_s[0]), jnp.stack(kv_b_s[1]), jnp.stack(kv_b_s[2]),
            jnp.stack(pool_p), jnp.stack(pool_s),
            jnp.stack(cd_p), jnp.stack(nd_p), jnp.stack(md_p),
            jnp.stack(cd_s), jnp.stack(nd_s), jnp.stack(md_s),
            jnp.stack(mem_p))
```

```python
import functools

import jax
import jax.numpy as jnp
from jax import lax
from jax.experimental import pallas as pl
from jax.experimental.pallas import tpu as pltpu

F32 = jnp.float32
BF16 = jnp.bfloat16
EPS = 1e-6
NEG = -1e30
LANES = 128
SUBLANES = 8
VMEM_LIMIT_BYTES = 56 * 2**20
ROW_TILE = 1024
POOL_HALO = 16
MAX_BISECTIONS = 512
SAFE_SHIFT = 40.0

NT_DIMS = (((1,), (1,)), ((), ()))
TN_DIMS = (((0,), (0,)), ((), ()))


def _params(*sem):
    return pltpu.CompilerParams(dimension_semantics=sem, vmem_limit_bytes=VMEM_LIMIT_BYTES)


def _rms(x, g):
    return x * lax.rsqrt(jnp.mean(x * x, axis=-1, keepdims=True) + EPS) * g


def _col_tile(n):
    for t in (512, 384, 256, 128):
        if n % t == 0:
            return t
    raise ValueError(f"matmul width {n} is not a multiple of {LANES}")


def _pad_rows(x, rows):
    if x.shape[0] == rows:
        return x
    return jnp.concatenate([x, jnp.zeros((rows - x.shape[0],) + x.shape[1:], x.dtype)], axis=0)


def _pad_cols(w, n):
    return jnp.pad(w, ((0, 0), (0, n - w.shape[1])))


def _rms_matmul_body(x_ref, g_ref, w_ref, o_ref, h_ref, *, precision):
    @pl.when(pl.program_id(1) == 0)
    def _():
        h_ref[...] = _rms(x_ref[...], g_ref[...]).astype(h_ref.dtype)

    o_ref[...] = jnp.dot(h_ref[...], w_ref[...], preferred_element_type=F32, precision=precision)


def rms_matmul(x, g, w, *, precise=False):
    m, d = x.shape
    n = w.shape[1]
    tm, tn = min(m, ROW_TILE), _col_tile(n)
    return pl.pallas_call(
        functools.partial(_rms_matmul_body, precision=lax.Precision.HIGHEST if precise else None),
        grid=(m // tm, n // tn),
        in_specs=[pl.BlockSpec((tm, d), lambda i, j: (i, 0)),
                  pl.BlockSpec((1, d), lambda i, j: (0, 0)),
                  pl.BlockSpec((d, tn), lambda i, j: (0, j))],
        out_specs=pl.BlockSpec((tm, tn), lambda i, j: (i, j)),
        out_shape=jax.ShapeDtypeStruct((m, n), F32),
        scratch_shapes=[pltpu.VMEM((tm, d), F32 if precise else BF16)],
        compiler_params=_params("parallel", "arbitrary"),
        name="rms_matmul",
    )(x, g.reshape(1, d), w)


def _matmul_res_body(a_ref, w_ref, r_ref, o_ref):
    o_ref[...] = r_ref[...] + jnp.dot(a_ref[...].astype(BF16), w_ref[...], preferred_element_type=F32)


def matmul_res(a, w, r):
    m, k = a.shape
    n = w.shape[1]
    tm, tn = min(m, ROW_TILE), _col_tile(n)
    return pl.pallas_call(
        _matmul_res_body,
        grid=(m // tm, n // tn),
        in_specs=[pl.BlockSpec((tm, k), lambda i, j: (i, 0)),
                  pl.BlockSpec((k, tn), lambda i, j: (0, j)),
                  pl.BlockSpec((tm, tn), lambda i, j: (i, j))],
        out_specs=pl.BlockSpec((tm, tn), lambda i, j: (i, j)),
        out_shape=jax.ShapeDtypeStruct((m, n), F32),
        compiler_params=_params("parallel", "parallel"),
        name="matmul_res",
    )(a, w, r)


def _ffn_body(x_ref, g_ref, w1_ref, w2_ref, o_ref, h_ref, acc_ref):
    f = pl.program_id(1)

    @pl.when(f == 0)
    def _():
        h_ref[...] = _rms(x_ref[...], g_ref[...]).astype(BF16)
        acc_ref[...] = jnp.zeros_like(acc_ref)

    a = jnp.dot(h_ref[...], w1_ref[...], preferred_element_type=F32)
    a = jnp.square(jnp.maximum(a, 0.0)).astype(BF16)
    acc_ref[...] += jnp.dot(a, w2_ref[...], preferred_element_type=F32)

    @pl.when(f == pl.num_programs(1) - 1)
    def _():
        o_ref[...] = x_ref[...] + acc_ref[...]


def ffn(x, g, w1, w2):
    m, d = x.shape
    ff = w1.shape[1]
    tm, tf = min(m, ROW_TILE), 512
    return pl.pallas_call(
        _ffn_body,
        grid=(m // tm, ff // tf),
        in_specs=[pl.BlockSpec((tm, d), lambda i, f: (i, 0)),
                  pl.BlockSpec((1, d), lambda i, f: (0, 0)),
                  pl.BlockSpec((d, tf), lambda i, f: (0, f)),
                  pl.BlockSpec((tf, d), lambda i, f: (f, 0))],
        out_specs=pl.BlockSpec((tm, d), lambda i, f: (i, 0)),
        out_shape=jax.ShapeDtypeStruct((m, d), F32),
        scratch_shapes=[pltpu.VMEM((tm, d), BF16), pltpu.VMEM((tm, d), F32)],
        compiler_params=_params("parallel", "arbitrary"),
        name="ffn",
    )(x, g.reshape(1, d), w1, w2)


def _head_norm_body(*refs, head_dim, n_heads, with_copy, with_bf16):
    k_ref, g_ref = refs[0], refs[1]
    v_ref = refs[2] if with_copy else None
    outs = refs[3 if with_copy else 2:]
    width = n_heads * head_dim
    for h in range(n_heads):
        sl = slice(h * head_dim, (h + 1) * head_dim)
        kn = _rms(k_ref[:, sl], g_ref[:, sl])
        for o in outs:
            o[:, sl] = kn.astype(o.dtype)
    if with_copy:
        v = v_ref[...]
        for o in outs:
            o[:, width:] = v.astype(o.dtype)


def head_norm(p, k_block, width, head_dim, gain, *, v_block=None, with_bf16=False):
    m = p.shape[0]
    n_heads = width // head_dim
    tm = min(m, ROW_TILE)
    with_copy = v_block is not None
    ow = 2 * width if with_copy else width
    in_specs = [pl.BlockSpec((tm, width), lambda i: (i, k_block)),
                pl.BlockSpec((1, width), lambda i: (0, 0))]
    args = [p, jnp.tile(gain.reshape(1, head_dim), (1, n_heads))]
    if with_copy:
        in_specs.append(pl.BlockSpec((tm, width), lambda i: (i, v_block)))
        args.append(p)
    out_shape = [jax.ShapeDtypeStruct((m, ow), F32)]
    if with_bf16:
        out_shape.append(jax.ShapeDtypeStruct((m, ow), BF16))
    out = pl.pallas_call(
        functools.partial(_head_norm_body, head_dim=head_dim, n_heads=n_heads, with_copy=with_copy,
                          with_bf16=with_bf16),
        grid=(m // tm,),
        in_specs=in_specs,
        out_specs=[pl.BlockSpec((tm, ow), lambda i: (i, 0)) for _ in out_shape],
        out_shape=out_shape,
        compiler_params=_params("parallel"),
        name="head_norm",
    )(*args)
    return out if with_bf16 else out[0]


def head_norm_groups(p, k_block, width, gain_row):
    m = p.shape[0]
    tm = min(m, ROW_TILE)
    n_heads = width // LANES
    return pl.pallas_call(
        functools.partial(_head_norm_body, head_dim=LANES, n_heads=n_heads, with_copy=False, with_bf16=False),
        grid=(m // tm,),
        in_specs=[pl.BlockSpec((tm, width), lambda i: (i, k_block)),
                  pl.BlockSpec((1, width), lambda i: (0, 0))],
        out_specs=[pl.BlockSpec((tm, width), lambda i: (i, 0))],
        out_shape=[jax.ShapeDtypeStruct((m, width), F32)],
        compiler_params=_params("parallel"),
        name="head_norm_groups",
    )(p, gain_row)[0]


def _mem_attn_body(q_ref, kv_ref, gq_ref, o_ref, *, n_heads, head_dim, n_mem, tiled_rows):
    width = n_heads * head_dim
    scale = head_dim ** -0.5
    n_lt = head_dim // LANES
    rstride = 2 * n_lt * n_heads

    def tiled(b, c, h):
        parts = [kv_ref[b, pl.ds((c * n_lt + lt) * n_heads + h, n_mem, stride=rstride), :] for lt in range(n_lt)]
        return jnp.concatenate(parts, axis=1)

    for b in range(q_ref.shape[0]):
        for h in range(n_heads):
            sl = slice(h * head_dim, (h + 1) * head_dim)
            qn = (_rms(q_ref[b, :, sl], gq_ref[...]) * scale).astype(BF16)
            if tiled_rows:
                k, v = tiled(b, 0, h).astype(BF16), tiled(b, 1, h).astype(BF16)
            else:
                k = kv_ref[b, :, sl].astype(BF16)
                v = kv_ref[b, :, width + h * head_dim:width + (h + 1) * head_dim].astype(BF16)
            lg = lax.dot_general(qn, k, NT_DIMS, preferred_element_type=F32)
            p = jnp.exp(lg - jnp.max(lg, axis=-1, keepdims=True))
            l = jnp.sum(p, axis=-1, keepdims=True)
            o = jnp.dot(p.astype(BF16), v, preferred_element_type=F32) / l
            o_ref[b, :, sl] = o.astype(o_ref.dtype)


def mem_attn(q, kv, g_q, *, n_heads, n_mem, batch_block, q_block, out_dtype, layer=None):
    b, t, w = q.shape
    head_dim = w // n_heads
    if layer is None:
        kv_spec = pl.BlockSpec((batch_block, n_mem, 2 * w), lambda i, j: (i, 0, 0))
    else:
        kv_spec = pl.BlockSpec((None, batch_block) + kv.shape[2:], lambda i, j: (layer, i, 0, 0))
    return pl.pallas_call(
        functools.partial(_mem_attn_body, n_heads=n_heads, head_dim=head_dim, n_mem=n_mem,
                          tiled_rows=layer is not None),
        grid=(b // batch_block, t // q_block),
        in_specs=[pl.BlockSpec((batch_block, q_block, w), lambda i, j: (i, j, 0)),
                  kv_spec,
                  pl.BlockSpec((1, head_dim), lambda i, j: (0, 0))],
        out_specs=pl.BlockSpec((batch_block, q_block, w), lambda i, j: (i, j, 0)),
        out_shape=jax.ShapeDtypeStruct((b, t, w), out_dtype),
        compiler_params=_params("parallel", "parallel"),
        name="mem_attn",
    )(q, kv, g_q.reshape(1, head_dim))


def _fold_lanes(x):
    acc = x[:, :LANES]
    for c in range(1, x.shape[1] // LANES):
        acc = acc + x[:, c * LANES:(c + 1) * LANES]
    return acc


def _chunk_loop(n_ch, body, init):
    if isinstance(n_ch, int):
        carry = init
        for c in range(n_ch):
            carry = body(c, carry)
        return carry
    return lax.fori_loop(0, n_ch, body, init)


def _count(sc_ref, n_ch, pred):
    rows = sc_ref.shape[1]

    def body(c, acc):
        return acc + _fold_lanes(jnp.where(pred(sc_ref[c], c), 1.0, 0.0))

    acc = _chunk_loop(n_ch, body, jnp.zeros((rows, LANES), F32))
    return jnp.sum(acc, axis=-1, keepdims=True)


def _minmax_update(carry, s):
    mn, mx = carry
    lo_c = jnp.where(s > -jnp.inf, s, jnp.inf)
    for j in range(s.shape[1] // LANES):
        sl = slice(j * LANES, (j + 1) * LANES)
        mn = jnp.minimum(mn, lo_c[:, sl])
        mx = jnp.maximum(mx, s[:, sl])
    return mn, mx


def _minmax_init(rows):
    return jnp.full((rows, LANES), jnp.inf, F32), jnp.full((rows, LANES), -jnp.inf, F32)


def _select_threshold(sc_ref, kk, n_ch, minmax, n_finite):
    rows, ch = sc_ref.shape[1], sc_ref.shape[2]
    lo = jnp.min(minmax[0], axis=-1, keepdims=True)
    mx = jnp.max(minmax[1], axis=-1, keepdims=True)
    hi = mx + jnp.maximum(jnp.abs(mx) * 2.0**-22, 1e-36)
    c_lo = n_finite
    done = jnp.where(c_lo == kk, 1.0, 0.0)

    def cond(st):
        return (jnp.min(st[3]) < 0.5) & (st[4] < MAX_BISECTIONS)

    def body(st):
        lo, hi, c_lo, done, it = st
        mid = lo * 0.5 + hi * 0.5
        stuck = (mid <= lo) | (mid >= hi)
        c_mid = _count(sc_ref, n_ch, lambda s, c: s >= mid)
        live = (done < 0.5) & jnp.logical_not(stuck)
        up = live & (c_mid >= kk)
        dn = live & (c_mid < kk)
        lo = jnp.where(up, mid, lo)
        c_lo = jnp.where(up, c_mid, c_lo)
        hi = jnp.where(dn, mid, hi)
        done = jnp.where(stuck | (c_lo == kk), 1.0, done)
        return lo, hi, c_lo, done, it + 1

    theta, _, c_theta, _, _ = lax.while_loop(cond, body, (lo, hi, c_lo, done, jnp.int32(0)))

    n_cols = n_ch * ch
    tie = c_theta > kk
    no_tie_jmax = jnp.full((rows, 1), 2.0**30, F32)

    def col_of(c):
        return (c * ch + lax.broadcasted_iota(jnp.int32, (1, ch), 1)).astype(F32)

    def resolve(_):
        need = kk - _count(sc_ref, n_ch, lambda s, c: s > theta)
        n_iter = max(1, (sc_ref.shape[0] * ch - 1).bit_length())

        def step(_, st):
            jl, jh = st
            jm = jnp.floor((jl + jh) * 0.5)
            c_m = _count(sc_ref, n_ch, lambda s, c: (s == theta) & (col_of(c) <= jm))
            ok = c_m >= need
            return jnp.where(ok, jl, jm + 1.0), jnp.where(ok, jm, jh)

        jl0 = jnp.zeros((rows, 1), F32)
        jh0 = jnp.zeros((rows, 1), F32) + (n_cols - 1)
        _, jh = lax.fori_loop(0, n_iter, step, (jl0, jh0))
        return jnp.where(tie, jh, no_tie_jmax)

    any_tie = jnp.max(jnp.where(tie, 1.0, 0.0)) > 0.5
    jmax = lax.cond(any_tie, resolve, lambda _: no_tie_jmax, 0)
    return theta, jmax


def _scores_to_bias(sc_ref, n_ch, theta, jmax):
    ch = sc_ref.shape[2]

    def body(c, carry):
        s = sc_ref[c]
        col = (c * ch + lax.broadcasted_iota(jnp.int32, (1, ch), 1)).astype(F32)
        sel = (s > theta) | ((s == theta) & (col <= jmax))
        sc_ref[c] = jnp.where(sel, 0.0, NEG)
        return carry

    _chunk_loop(n_ch, body, 0)


def _dsa_prompt_body(q_ref, qi_ref, wi_ref, ki_ref, kn_ref, v_ref, gq_ref, gk_ref, o_ref,
                     sc_ref, wb_ref, *, tq, n_top, n_idx_heads, idx_dim, n_kv, group, head_dim):
    ch = sc_ref.shape[2]
    t0 = pl.program_id(1) * tq
    n_ch = (t0 + tq + ch - 1) // ch
    rows = t0 + lax.broadcasted_iota(jnp.int32, (tq, 1), 0)

    qi = qi_ref[0]
    wi = wi_ref[0]
    qi_h = [qi[:, h * idx_dim:(h + 1) * idx_dim].astype(BF16) for h in range(n_idx_heads)]
    for h in range(n_idx_heads):
        wb_ref[h] = jnp.broadcast_to(wi[:, h:h + 1], (tq, LANES))

    def score_chunk(c, carry):
        ki = ki_ref[0, pl.ds(pl.multiple_of(c * ch, ch), ch), :]
        acc = jnp.zeros((tq, ch), F32)
        for h in range(n_idx_heads):
            s = lax.dot_general(qi_h[h], ki, NT_DIMS, preferred_element_type=F32)
            acc = acc + jnp.maximum(s, 0.0) * jnp.tile(wb_ref[h], (1, ch // LANES))
        cols = c * ch + lax.broadcasted_iota(jnp.int32, (1, ch), 1)
        acc = jnp.where(cols <= rows, acc, -jnp.inf)
        sc_ref[c] = acc
        return _minmax_update(carry, acc)

    minmax = lax.fori_loop(0, n_ch, score_chunk, _minmax_init(tq))

    n_causal = (rows + 1).astype(F32)
    kk = jnp.minimum(n_causal, float(n_top))
    theta, jmax = _select_threshold(sc_ref, kk, n_ch, minmax, n_causal)
    _scores_to_bias(sc_ref, n_ch, theta, jmax)

    scale = head_dim ** -0.5
    q = q_ref[0]
    qf = []
    for h in range(n_kv):
        qs = [_rms(q[:, (h * group + g) * head_dim:(h * group + g + 1) * head_dim], gq_ref[...]) * scale
              for g in range(group)]
        qf.append(jnp.concatenate(qs, axis=0))
    qst = [x.astype(BF16) for x in qf]
    gt = group * tq

    def logits(c, h):
        off = pl.multiple_of(c * ch, ch)
        k = kn_ref[0, pl.ds(off, ch), h * head_dim:(h + 1) * head_dim]
        return (lax.dot_general(qst[h], k, NT_DIMS, preferred_element_type=F32)
                + jnp.concatenate([sc_ref[c]] * group, axis=0))

    def softmax_pv(m_rows):
        m_rep = [jnp.tile(jnp.broadcast_to(m, (gt, LANES)), (1, ch // LANES)) for m in m_rows]

        def sum_chunk(c, carry):
            off = pl.multiple_of(c * ch, ch)
            lf, acc = carry
            lf_out, acc_out = [], []
            for h in range(n_kv):
                v = v_ref[0, pl.ds(off, ch), h * head_dim:(h + 1) * head_dim]
                p = jnp.exp(logits(c, h) - m_rep[h])
                lf_out.append(lf[h] + _fold_lanes(p))
                acc_out.append(acc[h] + jnp.dot(p.astype(BF16), v, preferred_element_type=F32))
            return tuple(lf_out), tuple(acc_out)

        lf, acc = lax.fori_loop(0, n_ch, sum_chunk, (tuple(jnp.zeros((gt, LANES), F32) for _ in range(n_kv)),
                                                      tuple(jnp.zeros((gt, head_dim), F32) for _ in range(n_kv))))
        for h in range(n_kv):
            out = acc[h] / jnp.sum(lf[h], axis=-1, keepdims=True)
            for g in range(group):
                o_ref[0, :, (h * group + g) * head_dim:(h * group + g + 1) * head_dim] = (
                    out[g * tq:(g + 1) * tq].astype(o_ref.dtype))

    k_norm = head_dim ** 0.5 * jnp.max(jnp.abs(gk_ref[...]), axis=-1, keepdims=True)
    bound = [jnp.sqrt(jnp.sum(x * x, axis=-1, keepdims=True)) * k_norm for x in qf]
    worst = bound[0].max()
    for x in bound[1:]:
        worst = jnp.maximum(worst, x.max())
    bound_is_safe = worst <= SAFE_SHIFT

    @pl.when(bound_is_safe)
    def _():
        softmax_pv(bound)

    @pl.when(jnp.logical_not(bound_is_safe))
    def _():
        def max_chunk(c, mf):
            out = []
            for h in range(n_kv):
                lg = logits(c, h)
                m = mf[h]
                for j in range(ch // LANES):
                    m = jnp.maximum(m, lg[:, j * LANES:(j + 1) * LANES])
                out.append(m)
            return tuple(out)

        mf = lax.fori_loop(0, n_ch, max_chunk, tuple(jnp.full((gt, LANES), NEG, F32) for _ in range(n_kv)))
        softmax_pv([jnp.max(x, axis=-1, keepdims=True) for x in mf])


def dsa_prompt(p, ki, kv, g_q, g_k, *, q_block, qi_block, wi_block, n_top, n_idx_heads, idx_dim, n_kv, group,
               head_dim, tq=128, ch=512):
    b, s, _ = p.shape
    qw = n_kv * group * head_dim
    kw = n_kv * head_dim
    iw = n_idx_heads * idx_dim
    body = functools.partial(_dsa_prompt_body, tq=tq, n_top=n_top, n_idx_heads=n_idx_heads, idx_dim=idx_dim,
                             n_kv=n_kv, group=group, head_dim=head_dim)
    return pl.pallas_call(
        body,
        grid=(b, s // tq),
        in_specs=[pl.BlockSpec((1, tq, qw), lambda i, j: (i, j, q_block)),
                  pl.BlockSpec((1, tq, iw), lambda i, j: (i, j, qi_block)),
                  pl.BlockSpec((1, tq, LANES), lambda i, j: (i, j, wi_block)),
                  pl.BlockSpec((1, s, idx_dim), lambda i, j: (i, 0, 0)),
                  pl.BlockSpec((1, s, kw), lambda i, j: (i, 0, 0)),
                  pl.BlockSpec((1, s, kw), lambda i, j: (i, 0, 1)),
                  pl.BlockSpec((1, head_dim), lambda i, j: (0, 0)),
                  pl.BlockSpec((1, head_dim), lambda i, j: (0, 0))],
        out_specs=pl.BlockSpec((1, tq, qw), lambda i, j: (i, j, 0)),
        out_shape=jax.ShapeDtypeStruct((b, s, qw), BF16),
        scratch_shapes=[pltpu.VMEM((s // ch, tq, ch), F32),
                        pltpu.VMEM((n_idx_heads, tq, LANES), F32)],
        compiler_params=_params("parallel", "arbitrary"),
        name="dsa_prompt",
    )(p, p, p, ki, kv, kv, g_q.reshape(1, head_dim), g_k.reshape(1, head_dim))


def _dsa_sample_body(pt_ref, p_ref, kvn_ref, gq_ref, *refs, n_pages, page, n_top, n_idx_heads, idx_dim, n_kv,
                     group, head_dim, q_off, qi_off, ki_off, wi_off):
    del pt_ref
    kidx_refs = refs[:n_pages]
    kv_refs = refs[n_pages:2 * n_pages]
    o_ref, sc_ref = refs[2 * n_pages], refs[2 * n_pages + 1]
    t = p_ref.shape[0]
    kw = n_kv * head_dim

    qi = p_ref[:, qi_off:qi_off + n_idx_heads * idx_dim]
    ki_new = p_ref[:, ki_off:ki_off + idx_dim]
    wi = p_ref[:, wi_off:wi_off + LANES]
    qi_st = jnp.concatenate([qi[:, h * idx_dim:(h + 1) * idx_dim] for h in range(n_idx_heads)], axis=0).astype(BF16)
    w_col = jnp.concatenate([wi[:, h:h + 1] for h in range(n_idx_heads)], axis=0)

    def idx_scores(s):
        s = jnp.maximum(s, 0.0) * w_col
        out = s[0:t]
        for h in range(1, n_idx_heads):
            out = out + s[h * t:(h + 1) * t]
        return out

    minmax = _minmax_init(t)
    for c in range(n_pages):
        s = idx_scores(jnp.dot(qi_st, kidx_refs[c][...].astype(BF16), preferred_element_type=F32))
        sc_ref[c] = s
        minmax = _minmax_update(minmax, s)
    own = idx_scores(lax.dot_general(qi_st, _pad_rows(ki_new, page).astype(BF16), NT_DIMS,
                                     preferred_element_type=F32))
    tok = lax.broadcasted_iota(jnp.int32, (t, page), 0)
    col = lax.broadcasted_iota(jnp.int32, (t, page), 1)
    own = jnp.where(col <= tok, own, -jnp.inf)
    sc_ref[n_pages] = own
    minmax = _minmax_update(minmax, own)

    n_ch = n_pages + 1
    n_finite = (n_pages * page + 1 + lax.broadcasted_iota(jnp.int32, (t, 1), 0)).astype(F32)
    theta, jmax = _select_threshold(sc_ref, jnp.full((t, 1), float(n_top), F32), n_ch, minmax, n_finite)
    _scores_to_bias(sc_ref, n_ch, theta, jmax)

    rstride = 2 * n_kv
    scale = head_dim ** -0.5
    for h in range(n_kv):
        hs = slice(h * head_dim, (h + 1) * head_dim)
        vs = slice(kw + h * head_dim, kw + (h + 1) * head_dim)
        qs = [(_rms(p_ref[:, q_off + (h * group + g) * head_dim:q_off + (h * group + g + 1) * head_dim],
                    gq_ref[...]) * scale).astype(BF16) for g in range(group)]
        qst = jnp.concatenate(qs, axis=0)
        lgs = []
        for c in range(n_ch):
            k = (kv_refs[c][pl.ds(h, page, stride=rstride), :] if c < n_pages
                 else _pad_rows(kvn_ref[:, hs], page))
            lg = lax.dot_general(qst, k.astype(BF16), NT_DIMS, preferred_element_type=F32)
            lgs.append(lg + jnp.concatenate([sc_ref[c]] * group, axis=0))
        m = lgs[0].max(axis=-1, keepdims=True)
        for lg in lgs[1:]:
            m = jnp.maximum(m, lg.max(axis=-1, keepdims=True))
        l = jnp.zeros((group * t, 1), F32)
        acc = jnp.zeros((group * t, head_dim), F32)
        for c in range(n_ch):
            pc = jnp.exp(lgs[c] - m)
            l = l + jnp.sum(pc, axis=-1, keepdims=True)
            v = (kv_refs[c][pl.ds(n_kv + h, page, stride=rstride), :] if c < n_pages
                 else _pad_rows(kvn_ref[:, vs], page))
            acc = acc + jnp.dot(pc.astype(BF16), v.astype(BF16), preferred_element_type=F32)
        out = acc / l
        for g in range(group):
            o_ref[:, (h * group + g) * head_dim:(h * group + g + 1) * head_dim] = (
                out[g * t:(g + 1) * t].astype(o_ref.dtype))


def dsa_sample(p, kv_new, kidx_pool, kv_pool, page_table, g_q, *, t, n_top, n_idx_heads, idx_dim, n_kv, group,
               head_dim, q_off, qi_off, ki_off, wi_off):
    bd, n_pages = page_table.shape
    page = kidx_pool.shape[2]
    qw = n_kv * group * head_dim
    kw = n_kv * head_dim
    n = p.shape[1]
    body = functools.partial(_dsa_sample_body, n_pages=n_pages, page=page, n_top=n_top, n_idx_heads=n_idx_heads,
                             idx_dim=idx_dim, n_kv=n_kv, group=group, head_dim=head_dim, q_off=q_off,
                             qi_off=qi_off, ki_off=ki_off, wi_off=wi_off)
    in_specs = [pl.BlockSpec((t, n), lambda b, pt: (b, 0)),
                pl.BlockSpec((t, 2 * kw), lambda b, pt: (b, 0)),
                pl.BlockSpec((1, head_dim), lambda b, pt: (0, 0))]
    in_specs += [pl.BlockSpec((None, idx_dim, page), functools.partial(lambda b, pt, c: (pt[b, c], 0, 0), c=c))
                 for c in range(n_pages)]
    in_specs += [pl.BlockSpec((None, page * 2 * n_kv, head_dim),
                              functools.partial(lambda b, pt, c: (pt[b, c], 0, 0), c=c))
                 for c in range(n_pages)]
    return pl.pallas_call(
        body,
        grid_spec=pltpu.PrefetchScalarGridSpec(
            num_scalar_prefetch=1,
            grid=(bd,),
            in_specs=in_specs,
            out_specs=pl.BlockSpec((t, qw), lambda b, pt: (b, 0)),
            scratch_shapes=[pltpu.VMEM((n_pages + 1, t, page), F32)]),
        out_shape=jax.ShapeDtypeStruct((bd * t, qw), F32),
        compiler_params=_params("arbitrary"),
        name="dsa_sample",
    )(page_table, p, kv_new, g_q.reshape(1, head_dim), *([kidx_pool] * n_pages), *([kv_pool] * n_pages))


def _dilated_mask(rel, window, dil):
    return (rel >= 0) & (rel <= window) & ((rel & (dil - 1)) == 0)


def _dil_prompt_body(q_ref, k_ref, v_ref, gq_ref, o_ref, lse_ref, m_ref, l_ref, acc_ref, *, window, dil, n_heads,
                     head_dim):
    tq = q_ref.shape[1]
    i, kc, nkc = pl.program_id(1), pl.program_id(2), pl.num_programs(2)
    cidx = i - (nkc - 1) + kc

    @pl.when(kc == 0)
    def _():
        m_ref[...] = jnp.full(m_ref.shape, NEG, F32)
        l_ref[...] = jnp.zeros(l_ref.shape, F32)
        acc_ref[...] = jnp.zeros(acc_ref.shape, F32)

    @pl.when(cidx >= 0)
    def _():
        rel = ((i - cidx) * tq + lax.broadcasted_iota(jnp.int32, (tq, tq), 0)
               - lax.broadcasted_iota(jnp.int32, (tq, tq), 1))
        bias = jnp.where(_dilated_mask(rel, window, dil), 0.0, NEG)
        scale = head_dim ** -0.5
        for h in range(n_heads):
            hs = slice(h * head_dim, (h + 1) * head_dim)
            qn = (_rms(q_ref[0, :, hs], gq_ref[...]) * scale).astype(BF16)
            lg = lax.dot_general(qn, k_ref[0, :, hs].astype(BF16), NT_DIMS, preferred_element_type=F32) + bias
            m_prev = m_ref[h]
            m_new = jnp.maximum(m_prev, jnp.max(lg, axis=-1, keepdims=True))
            alpha = jnp.exp(m_prev - m_new)
            p = jnp.exp(lg - m_new)
            l_ref[h] = alpha * l_ref[h] + jnp.sum(p, axis=-1, keepdims=True)
            acc_ref[h] = alpha * acc_ref[h] + jnp.dot(p.astype(BF16), v_ref[0, :, hs].astype(BF16),
                                                      preferred_element_type=F32)
            m_ref[h] = m_new

    @pl.when(kc == nkc - 1)
    def _():
        for h in range(n_heads):
            hs = slice(h * head_dim, (h + 1) * head_dim)
            o_ref[0, :, hs] = acc_ref[h] / l_ref[h]
            lse_ref[0, :, hs] = jnp.broadcast_to(m_ref[h] + jnp.log(l_ref[h]), (tq, head_dim))


def dilated_prompt_group(p, kn, g_q, *, q_block, k_block, v_block, window, dil, n_heads, head_dim, tq=512):
    b, s, _ = p.shape
    w = n_heads * head_dim
    nkc = -(-window // tq) + 1

    def kv_map(blk):
        return lambda bi, i, kc: (bi, jnp.maximum(i - (nkc - 1) + kc, 0), blk)

    return pl.pallas_call(
        functools.partial(_dil_prompt_body, window=window, dil=dil, n_heads=n_heads, head_dim=head_dim),
        grid=(b, s // tq, nkc),
        in_specs=[pl.BlockSpec((1, tq, w), lambda bi, i, kc: (bi, i, q_block)),
                  pl.BlockSpec((1, tq, w), kv_map(k_block)),
                  pl.BlockSpec((1, tq, w), kv_map(v_block)),
                  pl.BlockSpec((1, head_dim), lambda bi, i, kc: (0, 0))],
        out_specs=[pl.BlockSpec((1, tq, w), lambda bi, i, kc: (bi, i, 0)),
                   pl.BlockSpec((1, tq, w), lambda bi, i, kc: (bi, i, 0))],
        out_shape=[jax.ShapeDtypeStruct((b, s, w), F32), jax.ShapeDtypeStruct((b, s, w), F32)],
        scratch_shapes=[pltpu.VMEM((n_heads, tq, 1), F32), pltpu.VMEM((n_heads, tq, 1), F32),
                        pltpu.VMEM((n_heads, tq, head_dim), F32)],
        compiler_params=_params("parallel", "parallel", "arbitrary"),
        name="dilated_prompt",
    )(p, kn, p, g_q.reshape(1, head_dim))


def _dil_sample_body(p_ref, kn_ref, buf_ref, gq_ref, o_ref, lse_ref, nbuf_ref, *, window, dil, n_heads, head_dim,
                     q_off, k_off, v_off):
    t = p_ref.shape[0]
    rstride = 2 * n_heads
    wb = buf_ref.shape[1] // rstride
    w = n_heads * head_dim
    k_new = kn_ref[:, k_off:k_off + w]
    v_new = p_ref[:, v_off:v_off + w]
    nbuf_ref[0, 0:(wb - t) * rstride, :] = buf_ref[0, t * rstride:wb * rstride, :]
    for c, new in enumerate((k_new, v_new)):
        for h in range(n_heads):
            nbuf_ref[0, pl.ds((wb - t) * rstride + c * n_heads + h, t, stride=rstride), :] = (
                new[:, h * head_dim:(h + 1) * head_dim])

    rel = wb + lax.broadcasted_iota(jnp.int32, (t, wb), 0) - lax.broadcasted_iota(jnp.int32, (t, wb), 1)
    bias = jnp.where(_dilated_mask(rel, window, dil), 0.0, NEG)
    rel_o = lax.broadcasted_iota(jnp.int32, (t, LANES), 0) - lax.broadcasted_iota(jnp.int32, (t, LANES), 1)
    bias_o = jnp.where(_dilated_mask(rel_o, window, dil), 0.0, NEG)
    scale = head_dim ** -0.5
    for h in range(n_heads):
        hs = slice(h * head_dim, (h + 1) * head_dim)
        k_past = buf_ref[0, pl.ds(h, wb, stride=rstride), :]
        v_past = buf_ref[0, pl.ds(n_heads + h, wb, stride=rstride), :]
        qn = (_rms(p_ref[:, q_off + h * head_dim:q_off + (h + 1) * head_dim], gq_ref[...]) * scale).astype(BF16)
        lg = lax.dot_general(qn, k_past.astype(BF16), NT_DIMS, preferred_element_type=F32) + bias
        lg_o = lax.dot_general(qn, _pad_rows(k_new[:, hs], LANES).astype(BF16), NT_DIMS,
                               preferred_element_type=F32) + bias_o
        m = jnp.maximum(jnp.max(lg, axis=-1, keepdims=True), jnp.max(lg_o, axis=-1, keepdims=True))
        pp = jnp.exp(lg - m)
        pp_o = jnp.exp(lg_o - m)
        l = jnp.sum(pp, axis=-1, keepdims=True) + jnp.sum(pp_o, axis=-1, keepdims=True)
        acc = (jnp.dot(pp.astype(BF16), v_past.astype(BF16), preferred_element_type=F32)
               + jnp.dot(pp_o.astype(BF16), _pad_rows(v_new[:, hs], LANES).astype(BF16),
                         preferred_element_type=F32))
        o_ref[:, hs] = acc / l
        lse_ref[:, hs] = jnp.broadcast_to(m + jnp.log(l), (t, head_dim))


def dilated_sample_group(p, kn, buf, g_q, *, t, window, dil, n_heads, head_dim, q_off, k_off, v_off):
    bd, wb, w2 = buf.shape
    w = n_heads * head_dim
    n, nk = p.shape[1], kn.shape[1]
    return pl.pallas_call(
        functools.partial(_dil_sample_body, window=window, dil=dil, n_heads=n_heads, head_dim=head_dim,
                          q_off=q_off, k_off=k_off, v_off=v_off),
        grid=(bd,),
        in_specs=[pl.BlockSpec((t, n), lambda b: (b, 0)),
                  pl.BlockSpec((t, nk), lambda b: (b, 0)),
                  pl.BlockSpec((1, wb, w2), lambda b: (b, 0, 0)),
                  pl.BlockSpec((1, head_dim), lambda b: (0, 0))],
        out_specs=[pl.BlockSpec((t, w), lambda b: (b, 0)),
                   pl.BlockSpec((t, w), lambda b: (b, 0)),
                   pl.BlockSpec((1, wb, w2), lambda b: (b, 0, 0))],
        out_shape=[jax.ShapeDtypeStruct((bd * t, w), F32), jax.ShapeDtypeStruct((bd * t, w), F32),
                   jax.ShapeDtypeStruct((bd, wb, w2), F32)],
        compiler_params=_params("parallel"),
        name="dilated_sample",
    )(p, kn, buf, g_q.reshape(1, head_dim))


def _merge_groups_body(*refs):
    n = (len(refs) - 1) // 2
    o_refs, lse_refs, out_ref = refs[:n], refs[n:2 * n], refs[2 * n]
    lses = [r[...] for r in lse_refs]
    m = lses[0]
    for x in lses[1:]:
        m = jnp.maximum(m, x)
    ws = [jnp.exp(x - m) for x in lses]
    den = ws[0]
    for x in ws[1:]:
        den = den + x
    acc = ws[0] * o_refs[0][...]
    for wgt, o in zip(ws[1:], o_refs[1:]):
        acc = acc + wgt * o[...]
    out_ref[...] = (acc / den).astype(out_ref.dtype)


def merge_groups(outs, lses):
    m, w = outs[0].shape
    tm = min(m, ROW_TILE)
    spec = pl.BlockSpec((tm, w), lambda i: (i, 0))
    return pl.pallas_call(
        _merge_groups_body,
        grid=(m // tm,),
        in_specs=[spec] * (2 * len(outs)),
        out_specs=spec,
        out_shape=jax.ShapeDtypeStruct((m, w), BF16),
        compiler_params=_params("parallel"),
        name="merge_groups",
    )(*outs, *lses)


def _pool_body(u_ref, halo_ref, wg_ref, sc_ref, o_ref, ext_ref, *, n_pre, windows, halo_is_history):
    nb, t, d = u_ref.shape
    gw = d // len(windows)
    i = pl.program_id(1)
    halo = halo_ref[...]
    if halo_is_history:
        halo = jnp.where(i == 0, 0.0, halo)
    ext_ref[:, 0:POOL_HALO, :] = halo
    ext_ref[:, POOL_HALO:POOL_HALO + t, :] = u_ref[...]
    pos = i * t + lax.broadcasted_iota(jnp.int32, (1, t, 1), 1)
    for g, w in enumerate(windows):
        cs = slice(g * gw, (g + 1) * gw)
        acc = ext_ref[:, POOL_HALO:POOL_HALO + t, cs]
        for j in range(1, w):
            acc = acc + ext_ref[:, POOL_HALO - j:POOL_HALO - j + t, cs]
        div = jnp.minimum(w, n_pre + 1 + pos).astype(F32)
        r = acc / div - u_ref[:, :, cs]
        y = jnp.dot(r.reshape(nb * t, gw).astype(BF16), wg_ref[g], preferred_element_type=F32) * sc_ref[:, cs]
        o_ref[:, :, cs] = y.reshape(nb, t, gw).astype(o_ref.dtype)


def pool_mix(u, halo, w_grp, scale, *, n_pre, windows, batch_block, t_block, halo_is_history, out_dtype):
    b, t, d = u.shape
    hb = t_block // POOL_HALO
    if halo_is_history:
        halo_map = lambda bi, i: (bi, jnp.maximum(i * hb - 1, 0), 0)
    else:
        halo_map = lambda bi, i: (bi, 0, 0)
    return pl.pallas_call(
        functools.partial(_pool_body, n_pre=n_pre, windows=windows, halo_is_history=halo_is_history),
        grid=(b // batch_block, t // t_block),
        in_specs=[pl.BlockSpec((batch_block, t_block, d), lambda bi, i: (bi, i, 0)),
                  pl.BlockSpec((batch_block, POOL_HALO, d), halo_map),
                  pl.BlockSpec(w_grp.shape, lambda bi, i: (0, 0, 0)),
                  pl.BlockSpec((1, d), lambda bi, i: (0, 0))],
        out_specs=pl.BlockSpec((batch_block, t_block, d), lambda bi, i: (bi, i, 0)),
        out_shape=jax.ShapeDtypeStruct((b, t, d), out_dtype),
        scratch_shapes=[pltpu.VMEM((batch_block, POOL_HALO + t_block, d), F32)],
        compiler_params=_params("parallel", "parallel"),
        name="pool_mix",
    )(u, halo, w_grp, scale.reshape(1, d))


def _mlstm_body(q_ref, k_ref, v_ref, op_ref, gi_ref, gf_ref, bi_ref, bf_ref, gh_ref, c0_ref, n0_ref, m0_ref,
                y_ref, c_ref, n_ref, m_ref, cs, ns, ms, *, n_heads, head_dim):
    L = q_ref.shape[1]
    ls = max(L, LANES)
    ci = pl.program_id(1)
    hp = lax.Precision.HIGHEST

    @pl.when(ci == 0)
    def _():
        cs[...] = c0_ref[0]
        ns[...] = n0_ref[0]
        ms[...] = m0_ref[0]

    li = gi_ref[0] + bi_ref[...]
    lf = jax.nn.log_sigmoid(gf_ref[0] + bf_ref[...])
    row = lax.broadcasted_iota(jnp.int32, (L, ls), 0)
    col = lax.broadcasted_iota(jnp.int32, (L, ls), 1)
    causal = col <= row
    bt = jnp.dot(jnp.where(causal, 1.0, 0.0), _pad_rows(lf, ls), preferred_element_type=F32, precision=hp)
    lane = lax.broadcasted_iota(jnp.int32, (L, LANES), 1)
    lane1 = lax.broadcasted_iota(jnp.int32, (1, LANES), 1)
    m_all = ms[...]
    for h in range(n_heads):
        hs = slice(h * head_dim, (h + 1) * head_dim)
        bt_h = bt[:, h:h + 1]
        li_h = li[:, h:h + 1]
        m_prev = m_all[:, h:h + 1]
        a_h = jnp.where(lane == 0, bt_h, jnp.where(lane == 1, 1.0, 0.0))
        b_h = jnp.where(lane == 0, 1.0, jnp.where(lane == 1, li_h - bt_h, 0.0))
        dmat = lax.dot_general(a_h, _pad_rows(b_h, ls), NT_DIMS, preferred_element_type=F32, precision=hp)
        dmat = jnp.where(causal, dmat, -jnp.inf)
        g = bt_h + m_prev
        m_t = jnp.maximum(g, jnp.max(dmat, axis=-1, keepdims=True))
        w_intra = jnp.exp(dmat - m_t)
        w_inter = jnp.exp(g - m_t)
        q = q_ref[0, :, hs]
        k = k_ref[0, :, hs] * head_dim ** -0.5
        v = v_ref[0, :, hs]
        qb, kb, vb = q.astype(BF16), _pad_rows(k, ls).astype(BF16), _pad_rows(v, ls).astype(BF16)
        qk = lax.dot_general(qb, kb, NT_DIMS, preferred_element_type=F32) * w_intra
        c_h = cs[h]
        num = (w_inter * jnp.dot(qb, c_h.astype(BF16), preferred_element_type=F32)
               + jnp.dot(qk.astype(BF16), vb, preferred_element_type=F32))
        n_h = ns[h:h + 1, :]
        den = w_inter * jnp.sum(q * n_h, axis=-1, keepdims=True) + jnp.sum(qk, axis=-1, keepdims=True)
        hout = num / jnp.maximum(jnp.abs(den), jnp.exp(-m_t))
        b_last = bt_h[L - 1:L, :]
        decay = b_last - bt_h + li_h
        m_new = jnp.maximum(b_last + m_prev, jnp.max(decay, axis=0, keepdims=True))
        keep = jnp.exp(b_last + m_prev - m_new)
        kw = k * jnp.exp(decay - m_new)
        cs[h] = keep * c_h + lax.dot_general(_pad_rows(kw, ls).astype(BF16), vb, TN_DIMS,
                                             preferred_element_type=F32)
        ns[h:h + 1, :] = keep * n_h + jnp.sum(kw, axis=0, keepdims=True)
        m_all = jnp.where(lane1 == h, m_new, m_all)
        hn = _rms(hout, gh_ref[:, hs])
        y_ref[0, :, hs] = (jax.nn.sigmoid(op_ref[0, :, hs]) * hn).astype(y_ref.dtype)
    ms[...] = m_all

    @pl.when(ci == pl.num_programs(1) - 1)
    def _():
        c_ref[0] = cs[...]
        n_ref[0] = ns[...]
        m_ref[0] = ms[...]


def mlstm(p, gates, b_i, b_f, g_h, c0, n0, m0, *, n_heads, chunk, out_dtype):
    b, t, w4 = p.shape
    w = w4 // 4
    head_dim = w // n_heads
    nc = t // chunk

    def lane_row(x):
        return jnp.pad(x.reshape(1, -1), ((0, 0), (0, LANES - x.size)))

    m0p = jnp.pad(m0.reshape(b, 1, n_heads), ((0, 0), (0, 0), (0, LANES - n_heads)))
    colspec = lambda blk, wd: pl.BlockSpec((1, chunk, wd), lambda bi, ci: (bi, ci, blk))
    const = lambda shape: pl.BlockSpec(shape, lambda bi, ci: (0,) * len(shape))
    state = lambda shape: pl.BlockSpec((1,) + shape, lambda bi, ci: (bi,) + (0,) * len(shape))
    y, c, n, m = pl.pallas_call(
        functools.partial(_mlstm_body, n_heads=n_heads, head_dim=head_dim),
        grid=(b, nc),
        in_specs=[colspec(0, w), colspec(1, w), colspec(2, w), colspec(3, w),
                  colspec(0, LANES), colspec(1, LANES),
                  const((1, LANES)), const((1, LANES)), const((1, w)),
                  state((n_heads, head_dim, head_dim)), state((n_heads, head_dim)), state((1, LANES))],
        out_specs=[colspec(0, w), state((n_heads, head_dim, head_dim)), state((n_heads, head_dim)),
                   state((1, LANES))],
        out_shape=[jax.ShapeDtypeStruct((b, t, w), out_dtype),
                   jax.ShapeDtypeStruct((b, n_heads, head_dim, head_dim), F32),
                   jax.ShapeDtypeStruct((b, n_heads, head_dim), F32),
                   jax.ShapeDtypeStruct((b, 1, LANES), F32)],
        scratch_shapes=[pltpu.VMEM((n_heads, head_dim, head_dim), F32),
                        pltpu.VMEM((n_heads, head_dim), F32),
                        pltpu.VMEM((1, LANES), F32)],
        compiler_params=_params("parallel", "arbitrary"),
        name="mlstm",
    )(p, p, p, p, gates, gates, lane_row(b_i), lane_row(b_f), g_h.reshape(1, w), c0, n0, m0p)
    return y, c, n, m[:, 0, :n_heads]


def kernel(x_prompt, x_sample, cache_kv_a, cache_kidx_a, page_table, cache_kv_b0, cache_kv_b1, cache_kv_b2,
           state_pool_c, state_c_d, state_n_d, state_m_d, cache_mem_kv, mem_prompt,
           g_norm_mix, g_norm_mem, g_norm_ffn, g_mem_src, w_mem_q, w_mem_kv, g_mem_q, g_mem_k, w_mem_o,
           w_ff1, w_ff2, w_in_a, g_q_a, g_k_a, w_out_a, w_in_b, g_q_b, g_k_b, w_out_b,
           w_in_c, w_grp_c, scale_c, w_out_c, w_in_d, b_i_d, b_f_d, g_h_d, w_out_d):
    bp, s, d = x_prompt.shape
    bd, t, _ = x_sample.shape
    depth = g_norm_mix.shape[0]
    n_mixers = 4
    xp = x_prompt.reshape(bp * s, d)
    xs = x_sample.reshape(bd * t, d)
    cache_b = (cache_kv_b0, cache_kv_b1, cache_kv_b2)
    dilated = ((128, 1), (512, 4), (2048, 16))
    pool_windows = (2, 4, 8, 16)
    n_mem, n_heads_mem = mem_prompt.shape[1], cache_mem_kv.shape[4]
    mem2d = mem_prompt.reshape(bp * n_mem, d)
    hd_mem = cache_mem_kv.shape[5]
    mem_cache_rows = jnp.swapaxes(
        cache_mem_kv.reshape(depth, bd, n_mem, 2, n_heads_mem, hd_mem // LANES, LANES), 4, 5
    ).reshape(depth, bd, n_mem * 2 * hd_mem // LANES * n_heads_mem, LANES)

    kv_a_p, ki_a_p, kv_a_s, ki_a_s = [], [], [], []
    kv_b_p = [[] for _ in dilated]
    kv_b_s = [[] for _ in dilated]
    pool_p, pool_s = [], []
    cd_p, nd_p, md_p, cd_s, nd_s, md_s = [], [], [], [], [], []
    mem_p = []

    for i in range(depth):
        kind, j = i % n_mixers, i // n_mixers
        g_mix = g_norm_mix[i]
        if kind == 0:
            n_kv, hd = cache_kv_a.shape[4], cache_kv_a.shape[5]
            idx_dim = cache_kidx_a.shape[3]
            qw = w_out_a.shape[1]
            group = qw // (n_kv * hd)
            kw = n_kv * hd
            n_idx = (w_in_a.shape[2] - qw - 2 * kw - idx_dim) // (idx_dim + 1)
            iw = n_idx * idx_dim
            w = w_in_a[j]
            o3 = qw + 2 * kw
            ki_off = o3 + iw
            wi_off = ki_off + LANES
            n_cols = -(-(wi_off + LANES) // 512) * 512
            wa = jnp.concatenate([w[:, :o3 + iw], _pad_cols(w[:, o3 + iw:o3 + iw + idx_dim], LANES),
                                  _pad_cols(w[:, o3 + iw + idx_dim:], n_cols - wi_off)], axis=1).astype(BF16)
            wo = w_out_a[j].astype(BF16)
            pp = rms_matmul(xp, g_mix, wa)
            ps = rms_matmul(xs, g_mix, wa)
            kvp, kvp_bf = head_norm(pp, qw // kw, kw, hd, g_k_a[j], v_block=qw // kw + 1, with_bf16=True)
            kvs = head_norm(ps, qw // kw, kw, hd, g_k_a[j], v_block=qw // kw + 1)
            kip = pp[:, ki_off:ki_off + idx_dim]
            kis = ps[:, ki_off:ki_off + idx_dim]
            kv_a_p.append(kvp.reshape(bp, s, 2, n_kv, hd))
            ki_a_p.append(kip.reshape(bp, s, idx_dim))
            kv_a_s.append(kvs.reshape(bd, t, 2, n_kv, hd))
            ki_a_s.append(kis.reshape(bd, t, idx_dim))
            dims = dict(n_idx_heads=n_idx, idx_dim=idx_dim, n_kv=n_kv, group=group, head_dim=hd)
            op = dsa_prompt(pp.reshape(bp, s, n_cols), kip.astype(BF16).reshape(bp, s, idx_dim),
                            kvp_bf.reshape(bp, s, 2 * kw), g_q_a[j], g_k_a[j], q_block=0, qi_block=o3 // iw,
                            wi_block=wi_off // LANES, n_top=min(256, s // 4), tq=256, **dims)
            n_pool, page = cache_kidx_a.shape[1], cache_kidx_a.shape[2]
            past = page_table.shape[1] * page
            kidx_pool = jnp.swapaxes(cache_kidx_a[j], 1, 2)
            kv_pool = cache_kv_a[j].reshape(n_pool, page * 2 * n_kv, hd)
            os_ = dsa_sample(ps, kvs, kidx_pool, kv_pool, page_table,
                             g_q_a[j], t=t, n_top=min(256, (past + t) // 4), q_off=0, qi_off=o3, ki_off=ki_off,
                             wi_off=wi_off, **dims)
            xp = matmul_res(op.reshape(bp * s, qw), wo, xp)
            xs = matmul_res(os_, wo, xs)
        elif kind == 1:
            n_groups = len(dilated)
            hg, hd = cache_kv_b0.shape[4], cache_kv_b0.shape[5]
            gw = hg * hd
            wb_ = w_in_b[j].astype(BF16)
            wo = w_out_b[j].astype(BF16)
            pp = rms_matmul(xp, g_mix, wb_)
            ps = rms_matmul(xs, g_mix, wb_)
            gk_row = jnp.repeat(g_k_b[j], hg, axis=0).reshape(1, n_groups * gw)
            knp = head_norm_groups(pp, 1, n_groups * gw, gk_row)
            kns = head_norm_groups(ps, 1, n_groups * gw, gk_row)
            pp3 = pp.reshape(bp, s, -1)
            knp3 = knp.reshape(bp, s, -1)
            outs_p, lses_p, outs_s, lses_s = [], [], [], []
            for g, (win, dil) in enumerate(dilated):
                o, lse = dilated_prompt_group(pp3, knp3, g_q_b[j, g], q_block=g, k_block=g,
                                              v_block=2 * n_groups + g, window=win, dil=dil, n_heads=hg,
                                              head_dim=hd)
                outs_p.append(o.reshape(bp * s, gw))
                lses_p.append(lse.reshape(bp * s, gw))
                wb = min(win, s)
                kv_b_p[g].append(jnp.stack(
                    [knp3[:, s - wb:, g * gw:(g + 1) * gw].reshape(bp, wb, hg, hd),
                     pp3[:, s - wb:, (2 * n_groups + g) * gw:(2 * n_groups + g + 1) * gw].reshape(bp, wb, hg, hd)],
                    axis=2))
                buf = cache_b[g][j]
                o, lse, nbuf = dilated_sample_group(ps, kns, buf.reshape(bd, buf.shape[1] * 2 * hg, hd), g_q_b[j, g],
                                                    t=t, window=win, dil=dil, n_heads=hg, head_dim=hd,
                                                    q_off=g * gw, k_off=g * gw, v_off=(2 * n_groups + g) * gw)
                outs_s.append(o)
                lses_s.append(lse)
                kv_b_s[g].append(nbuf.reshape(buf.shape))
            xp = matmul_res(merge_groups(outs_p, lses_p), wo, xp)
            xs = matmul_res(merge_groups(outs_s, lses_s), wo, xs)
        elif kind == 2:
            wi_, wo = w_in_c[j].astype(BF16), w_out_c[j].astype(BF16)
            wg = w_grp_c[j].astype(BF16)
            up = rms_matmul(xp, g_mix, wi_).reshape(bp, s, d)
            us = rms_matmul(xs, g_mix, wi_).reshape(bd, t, d)
            n_state = state_pool_c.shape[2]
            yp = pool_mix(up, up, wg, scale_c[j], n_pre=0, windows=pool_windows, batch_block=1, t_block=512,
                          halo_is_history=True, out_dtype=BF16)
            halo = jnp.pad(state_pool_c[j], ((0, 0), (POOL_HALO - n_state, 0), (0, 0)))
            ys = pool_mix(us, halo, wg, scale_c[j], n_pre=n_state, windows=pool_windows, batch_block=min(bd, 32),
                          t_block=t, halo_is_history=False, out_dtype=F32)
            pool_p.append(up[:, s - n_state:])
            pool_s.append(jnp.concatenate([state_pool_c[j], us], axis=1)[:, -n_state:])
            xp = matmul_res(yp.reshape(bp * s, d), wo, xp)
            xs = matmul_res(ys.reshape(bd * t, d), wo, xs)
        else:
            nh, hd = state_c_d.shape[2], state_c_d.shape[3]
            wd = nh * hd
            w = w_in_d[j]
            w_main = w[:, :4 * wd].astype(BF16)
            w_gate = jnp.concatenate([_pad_cols(w[:, 4 * wd:4 * wd + nh], LANES),
                                      _pad_cols(w[:, 4 * wd + nh:], LANES)], axis=1)
            wo = w_out_d[j].astype(BF16)
            pp = rms_matmul(xp, g_mix, w_main).reshape(bp, s, 4 * wd)
            ps = rms_matmul(xs, g_mix, w_main).reshape(bd, t, 4 * wd)
            gp = rms_matmul(xp, g_mix, w_gate, precise=True).reshape(bp, s, 2 * LANES)
            gs = rms_matmul(xs, g_mix, w_gate, precise=True).reshape(bd, t, 2 * LANES)
            zeros = lambda *shape: jnp.zeros(shape, F32)
            yp, c, n, m = mlstm(pp, gp, b_i_d[j], b_f_d[j], g_h_d[j], zeros(bp, nh, hd, hd), zeros(bp, nh, hd),
                                zeros(bp, nh), n_heads=nh, chunk=256, out_dtype=BF16)
            cd_p.append(c)
            nd_p.append(n)
            md_p.append(m)
            ys, c, n, m = mlstm(ps, gs, b_i_d[j], b_f_d[j], g_h_d[j], state_c_d[j], state_n_d[j], state_m_d[j],
                                n_heads=nh, chunk=t, out_dtype=F32)
            cd_s.append(c)
            nd_s.append(n)
            md_s.append(m)
            xp = matmul_res(yp.reshape(bp * s, wd), wo, xp)
            xs = matmul_res(ys.reshape(bd * t, wd), wo, xs)

        wm = n_heads_mem * cache_mem_kv.shape[5]
        mkv_raw = rms_matmul(mem2d, g_mem_src[i], w_mem_kv[i].astype(BF16))
        mkv = head_norm(mkv_raw, 0, wm, wm // n_heads_mem, g_mem_k[i], v_block=1)
        mem_p.append(mkv.reshape(bp, n_mem, 2, n_heads_mem, wm // n_heads_mem))
        wq, wo = w_mem_q[i].astype(BF16), w_mem_o[i].astype(BF16)
        qp = rms_matmul(xp, g_norm_mem[i], wq).reshape(bp, s, wm)
        qs = rms_matmul(xs, g_norm_mem[i], wq).reshape(bd, t, wm)
        ap = mem_attn(qp, mkv.reshape(bp, n_mem, 2 * wm), g_mem_q[i], n_heads=n_heads_mem, n_mem=n_mem,
                      batch_block=1, q_block=512, out_dtype=BF16)
        as_ = mem_attn(qs, mem_cache_rows, g_mem_q[i], n_heads=n_heads_mem, n_mem=n_mem, batch_block=4,
                       q_block=t, out_dtype=F32, layer=i)
        xp = matmul_res(ap.reshape(bp * s, wm), wo, xp)
        xs = matmul_res(as_.reshape(bd * t, wm), wo, xs)

        w1, w2 = w_ff1[i].astype(BF16), w_ff2[i].astype(BF16)
        xp = ffn(xp, g_norm_ffn[i], w1, w2)
        xs = ffn(xs, g_norm_ffn[i], w1, w2)

    return (xp.reshape(bp, s, d), xs.reshape(bd, t, d),
            jnp.stack(kv_a_p), jnp.stack(ki_a_p), jnp.stack(kv_a_s), jnp.stack(ki_a_s),
            jnp.stack(kv_b_p[0]), jnp.stack(kv_b_p[1]), jnp.stack(kv_b_p[2]),
            jnp.stack(kv_b_s[0]), jnp.stack(kv_b_s[1]), jnp.stack(kv_b_s[2]),
            jnp.stack(pool_p), jnp.stack(pool_s),
            jnp.stack(cd_p), jnp.stack(nd_p), jnp.stack(md_p),
            jnp.stack(cd_s), jnp.stack(nd_s), jnp.stack(md_s),
            jnp.stack(mem_p))
```

```python
import functools

import jax
import jax.numpy as jnp
from jax import lax
from jax.experimental import pallas as pl
from jax.experimental.pallas import tpu as pltpu

F32 = jnp.float32
BF16 = jnp.bfloat16
EPS = 1e-6
NEG = -1e30
LANES = 128
SUBLANES = 8
VMEM_LIMIT_BYTES = 56 * 2**20
ROW_TILE = 1024
POOL_HALO = 16
MAX_BISECTIONS = 512
SAFE_SHIFT = 40.0
FOLD_CHAINS = 4

NT_DIMS = (((1,), (1,)), ((), ()))
TN_DIMS = (((0,), (0,)), ((), ()))


def _params(*sem):
    return pltpu.CompilerParams(dimension_semantics=sem, vmem_limit_bytes=VMEM_LIMIT_BYTES)


def _rms(x, g):
    return x * lax.rsqrt(jnp.mean(x * x, axis=-1, keepdims=True) + EPS) * g


def _col_tile(n):
    for t in (1536, 1024, 768, 512, 384, 256, 128):
        if n % t == 0:
            return t
    raise ValueError(f"matmul width {n} is not a multiple of {LANES}")


def _pad_rows(x, rows):
    if x.shape[0] == rows:
        return x
    return jnp.concatenate([x, jnp.zeros((rows - x.shape[0],) + x.shape[1:], x.dtype)], axis=0)


def _pad_cols(w, n):
    return jnp.pad(w, ((0, 0), (0, n - w.shape[1])))


def _rms_matmul_body(x_ref, g_ref, w_ref, o_ref, h_ref, *, precision):
    @pl.when(pl.program_id(1) == 0)
    def _():
        h_ref[...] = _rms(x_ref[...], g_ref[...]).astype(h_ref.dtype)

    o_ref[...] = jnp.dot(h_ref[...], w_ref[...], preferred_element_type=F32, precision=precision)


def rms_matmul(x, g, w, *, precise=False):
    m, d = x.shape
    n = w.shape[1]
    tm, tn = min(m, ROW_TILE), _col_tile(n)
    return pl.pallas_call(
        functools.partial(_rms_matmul_body, precision=lax.Precision.HIGHEST if precise else None),
        grid=(m // tm, n // tn),
        in_specs=[pl.BlockSpec((tm, d), lambda i, j: (i, 0)),
                  pl.BlockSpec((1, d), lambda i, j: (0, 0)),
                  pl.BlockSpec((d, tn), lambda i, j: (0, j))],
        out_specs=pl.BlockSpec((tm, tn), lambda i, j: (i, j)),
        out_shape=jax.ShapeDtypeStruct((m, n), F32),
        scratch_shapes=[pltpu.VMEM((tm, d), F32 if precise else BF16)],
        compiler_params=_params("parallel", "arbitrary"),
        name="rms_matmul",
    )(x, g.reshape(1, d), w)


def _matmul_res_body(a_ref, w_ref, r_ref, o_ref):
    o_ref[...] = r_ref[...] + jnp.dot(a_ref[...].astype(BF16), w_ref[...], preferred_element_type=F32)


def matmul_res(a, w, r):
    m, k = a.shape
    n = w.shape[1]
    tm, tn = min(m, ROW_TILE), _col_tile(n)
    return pl.pallas_call(
        _matmul_res_body,
        grid=(m // tm, n // tn),
        in_specs=[pl.BlockSpec((tm, k), lambda i, j: (i, 0)),
                  pl.BlockSpec((k, tn), lambda i, j: (0, j)),
                  pl.BlockSpec((tm, tn), lambda i, j: (i, j))],
        out_specs=pl.BlockSpec((tm, tn), lambda i, j: (i, j)),
        out_shape=jax.ShapeDtypeStruct((m, n), F32),
        compiler_params=_params("parallel", "parallel"),
        name="matmul_res",
    )(a, w, r)


def _ffn_body(x_ref, g_ref, w1_ref, w2_ref, o_ref, h_ref, acc_ref):
    f = pl.program_id(1)

    @pl.when(f == 0)
    def _():
        h_ref[...] = _rms(x_ref[...], g_ref[...]).astype(BF16)
        acc_ref[...] = jnp.zeros_like(acc_ref)

    a = jnp.dot(h_ref[...], w1_ref[...], preferred_element_type=F32)
    a = jnp.square(jnp.maximum(a, 0.0)).astype(BF16)
    acc_ref[...] += jnp.dot(a, w2_ref[...], preferred_element_type=F32)

    @pl.when(f == pl.num_programs(1) - 1)
    def _():
        o_ref[...] = x_ref[...] + acc_ref[...]


def ffn(x, g, w1, w2):
    m, d = x.shape
    ff = w1.shape[1]
    tm, tf = min(m, ROW_TILE), 1024
    return pl.pallas_call(
        _ffn_body,
        grid=(m // tm, ff // tf),
        in_specs=[pl.BlockSpec((tm, d), lambda i, f: (i, 0)),
                  pl.BlockSpec((1, d), lambda i, f: (0, 0)),
                  pl.BlockSpec((d, tf), lambda i, f: (0, f)),
                  pl.BlockSpec((tf, d), lambda i, f: (f, 0))],
        out_specs=pl.BlockSpec((tm, d), lambda i, f: (i, 0)),
        out_shape=jax.ShapeDtypeStruct((m, d), F32),
        scratch_shapes=[pltpu.VMEM((tm, d), BF16), pltpu.VMEM((tm, d), F32)],
        compiler_params=_params("parallel", "arbitrary"),
        name="ffn",
    )(x, g.reshape(1, d), w1, w2)


def _head_norm_body(*refs, head_dim, n_heads, with_copy, with_bf16):
    k_ref, g_ref = refs[0], refs[1]
    v_ref = refs[2] if with_copy else None
    outs = refs[3 if with_copy else 2:]
    width = n_heads * head_dim
    for h in range(n_heads):
        sl = slice(h * head_dim, (h + 1) * head_dim)
        kn = _rms(k_ref[:, sl], g_ref[:, sl])
        for o in outs:
            o[:, sl] = kn.astype(o.dtype)
    if with_copy:
        v = v_ref[...]
        for o in outs:
            o[:, width:] = v.astype(o.dtype)


def head_norm(p, k_block, width, head_dim, gain, *, v_block=None, with_bf16=False):
    m = p.shape[0]
    n_heads = width // head_dim
    tm = min(m, ROW_TILE)
    with_copy = v_block is not None
    ow = 2 * width if with_copy else width
    in_specs = [pl.BlockSpec((tm, width), lambda i: (i, k_block)),
                pl.BlockSpec((1, width), lambda i: (0, 0))]
    args = [p, jnp.tile(gain.reshape(1, head_dim), (1, n_heads))]
    if with_copy:
        in_specs.append(pl.BlockSpec((tm, width), lambda i: (i, v_block)))
        args.append(p)
    out_shape = [jax.ShapeDtypeStruct((m, ow), F32)]
    if with_bf16:
        out_shape.append(jax.ShapeDtypeStruct((m, ow), BF16))
    out = pl.pallas_call(
        functools.partial(_head_norm_body, head_dim=head_dim, n_heads=n_heads, with_copy=with_copy,
                          with_bf16=with_bf16),
        grid=(m // tm,),
        in_specs=in_specs,
        out_specs=[pl.BlockSpec((tm, ow), lambda i: (i, 0)) for _ in out_shape],
        out_shape=out_shape,
        compiler_params=_params("parallel"),
        name="head_norm",
    )(*args)
    return out if with_bf16 else out[0]


def head_norm_groups(p, k_block, width, gain_row):
    m = p.shape[0]
    tm = min(m, ROW_TILE)
    n_heads = width // LANES
    return pl.pallas_call(
        functools.partial(_head_norm_body, head_dim=LANES, n_heads=n_heads, with_copy=False, with_bf16=False),
        grid=(m // tm,),
        in_specs=[pl.BlockSpec((tm, width), lambda i: (i, k_block)),
                  pl.BlockSpec((1, width), lambda i: (0, 0))],
        out_specs=[pl.BlockSpec((tm, width), lambda i: (i, 0))],
        out_shape=[jax.ShapeDtypeStruct((m, width), F32)],
        compiler_params=_params("parallel"),
        name="head_norm_groups",
    )(p, gain_row)[0]


def _mem_attn_body(q_ref, kv_ref, gq_ref, o_ref, *, n_heads, head_dim, n_mem, tiled_rows):
    width = n_heads * head_dim
    scale = head_dim ** -0.5
    n_lt = head_dim // LANES
    rstride = 2 * n_lt * n_heads

    def tiled(b, c, h):
        parts = [kv_ref[b, pl.ds((c * n_lt + lt) * n_heads + h, n_mem, stride=rstride), :] for lt in range(n_lt)]
        return jnp.concatenate(parts, axis=1)

    for b in range(q_ref.shape[0]):
        for h in range(n_heads):
            sl = slice(h * head_dim, (h + 1) * head_dim)
            qn = (_rms(q_ref[b, :, sl], gq_ref[...]) * scale).astype(BF16)
            if tiled_rows:
                k, v = tiled(b, 0, h).astype(BF16), tiled(b, 1, h).astype(BF16)
            else:
                k = kv_ref[b, :, sl].astype(BF16)
                v = kv_ref[b, :, width + h * head_dim:width + (h + 1) * head_dim].astype(BF16)
            lg = lax.dot_general(qn, k, NT_DIMS, preferred_element_type=F32)
            p = jnp.exp(lg - jnp.max(lg, axis=-1, keepdims=True))
            l = jnp.sum(p, axis=-1, keepdims=True)
            o = jnp.dot(p.astype(BF16), v, preferred_element_type=F32) / l
            o_ref[b, :, sl] = o.astype(o_ref.dtype)


def mem_attn(q, kv, g_q, *, n_heads, n_mem, batch_block, q_block, out_dtype, layer=None):
    b, t, w = q.shape
    head_dim = w // n_heads
    if layer is None:
        kv_spec = pl.BlockSpec((batch_block, n_mem, 2 * w), lambda i, j: (i, 0, 0))
    else:
        kv_spec = pl.BlockSpec((None, batch_block) + kv.shape[2:], lambda i, j: (layer, i, 0, 0))
    return pl.pallas_call(
        functools.partial(_mem_attn_body, n_heads=n_heads, head_dim=head_dim, n_mem=n_mem,
                          tiled_rows=layer is not None),
        grid=(b // batch_block, t // q_block),
        in_specs=[pl.BlockSpec((batch_block, q_block, w), lambda i, j: (i, j, 0)),
                  kv_spec,
                  pl.BlockSpec((1, head_dim), lambda i, j: (0, 0))],
        out_specs=pl.BlockSpec((batch_block, q_block, w), lambda i, j: (i, j, 0)),
        out_shape=jax.ShapeDtypeStruct((b, t, w), out_dtype),
        compiler_params=_params("parallel", "parallel"),
        name="mem_attn",
    )(q, kv, g_q.reshape(1, head_dim))


def _fold(x, op, ka):
    if ka == 1:
        parts = [x[:, j * LANES:(j + 1) * LANES] for j in range(x.shape[1] // LANES)]
    else:
        parts = [x[j * SUBLANES:(j + 1) * SUBLANES, :] for j in range(x.shape[0] // SUBLANES)]
    accs = parts[:FOLD_CHAINS]
    for i, part in enumerate(parts[FOLD_CHAINS:]):
        accs[i % FOLD_CHAINS] = op(accs[i % FOLD_CHAINS], part)
    while len(accs) > 1:
        accs = [op(accs[i], accs[i + 1]) if i + 1 < len(accs) else accs[i] for i in range(0, len(accs), 2)]
    return accs[0]


def _fold_lanes(x):
    return _fold(x, jnp.add, 1)


def _folded_shape(chunk_shape, ka):
    return (chunk_shape[0], LANES) if ka == 1 else (SUBLANES, chunk_shape[1])


def _chunk_loop(n_ch, body, init):
    if isinstance(n_ch, int):
        carry = init
        for c in range(n_ch):
            carry = body(c, carry)
        return carry
    return lax.fori_loop(0, n_ch, body, init)


def _count(sc_ref, n_ch, pred, ka):
    def body(c, acc):
        return acc + _fold(jnp.where(pred(sc_ref[c], c), 1.0, 0.0), jnp.add, ka)

    acc = _chunk_loop(n_ch, body, jnp.zeros(_folded_shape(sc_ref.shape[1:], ka), F32))
    return jnp.sum(acc, axis=ka, keepdims=True)


def _minmax_update(carry, s, ka):
    mn, mx = carry
    mn = jnp.minimum(mn, _fold(jnp.where(s > -jnp.inf, s, jnp.inf), jnp.minimum, ka))
    return mn, jnp.maximum(mx, _fold(s, jnp.maximum, ka))


def _minmax_init(chunk_shape, ka):
    shape = _folded_shape(chunk_shape, ka)
    return jnp.full(shape, jnp.inf, F32), jnp.full(shape, -jnp.inf, F32)


def _key_index(c, chunk_shape, ka):
    shape = (1, chunk_shape[1]) if ka == 1 else (chunk_shape[0], 1)
    return (c * chunk_shape[ka] + lax.broadcasted_iota(jnp.int32, shape, ka)).astype(F32)


def _select_threshold(sc_ref, kk, n_ch, minmax, n_finite, ka):
    chunk = sc_ref.shape[1:]
    ch = chunk[ka]
    lo = jnp.min(minmax[0], axis=ka, keepdims=True)
    mx = jnp.max(minmax[1], axis=ka, keepdims=True)
    hi = mx + jnp.maximum(jnp.abs(mx) * 2.0**-22, 1e-36)
    c_lo = n_finite
    done = jnp.where(c_lo == kk, 1.0, 0.0)

    def cond(st):
        return (jnp.min(st[3]) < 0.5) & (st[4] < MAX_BISECTIONS)

    def body(st):
        lo, hi, c_lo, done, it = st
        mid = lo * 0.5 + hi * 0.5
        stuck = (mid <= lo) | (mid >= hi)
        c_mid = _count(sc_ref, n_ch, lambda s, c: s >= mid, ka)
        live = (done < 0.5) & jnp.logical_not(stuck)
        up = live & (c_mid >= kk)
        dn = live & (c_mid < kk)
        lo = jnp.where(up, mid, lo)
        c_lo = jnp.where(up, c_mid, c_lo)
        hi = jnp.where(dn, mid, hi)
        done = jnp.where(stuck | (c_lo == kk), 1.0, done)
        return lo, hi, c_lo, done, it + 1

    theta, _, c_theta, _, _ = lax.while_loop(cond, body, (lo, hi, c_lo, done, jnp.int32(0)))

    n_keys = n_ch * ch
    tie = c_theta > kk
    no_tie_jmax = jnp.full(kk.shape, 2.0**30, F32)

    def resolve(_):
        need = kk - _count(sc_ref, n_ch, lambda s, c: s > theta, ka)
        n_iter = max(1, (sc_ref.shape[0] * ch - 1).bit_length())

        def step(_, st):
            jl, jh = st
            jm = jnp.floor((jl + jh) * 0.5)
            c_m = _count(sc_ref, n_ch, lambda s, c: (s == theta) & (_key_index(c, chunk, ka) <= jm), ka)
            ok = c_m >= need
            return jnp.where(ok, jl, jm + 1.0), jnp.where(ok, jm, jh)

        jl0 = jnp.zeros(kk.shape, F32)
        jh0 = jnp.zeros(kk.shape, F32) + (n_keys - 1)
        _, jh = lax.fori_loop(0, n_iter, step, (jl0, jh0))
        return jnp.where(tie, jh, no_tie_jmax)

    any_tie = jnp.max(jnp.where(tie, 1.0, 0.0)) > 0.5
    jmax = lax.cond(any_tie, resolve, lambda _: no_tie_jmax, 0)
    return theta, jmax


def _scores_to_bias(sc_ref, n_ch, theta, jmax, ka):
    chunk = sc_ref.shape[1:]

    def body(c, carry):
        s = sc_ref[c]
        sel = (s > theta) | ((s == theta) & (_key_index(c, chunk, ka) <= jmax))
        sc_ref[c] = jnp.where(sel, 0.0, NEG)
        return carry

    _chunk_loop(n_ch, body, 0)


def _dsa_prompt_body(q_ref, qi_ref, wi_ref, ki_ref, kn_ref, vt_ref, gq_ref, gk_ref, o_ref,
                     sc_ref, *, tq, n_top, n_idx_heads, idx_dim, n_kv, group, head_dim):
    ka = 0
    ch = sc_ref.shape[1]
    t0 = pl.program_id(1) * tq
    n_ch = (t0 + tq + ch - 1) // ch
    qpos = t0 + lax.broadcasted_iota(jnp.int32, (1, tq), 1)

    qi = qi_ref[0]
    wi_t = wi_ref[0].T
    qi_h = [qi[:, h * idx_dim:(h + 1) * idx_dim].astype(BF16) for h in range(n_idx_heads)]
    w_row = [wi_t[h:h + 1, :] for h in range(n_idx_heads)]

    def score_chunk(c, carry):
        ki = ki_ref[0, pl.ds(pl.multiple_of(c * ch, ch), ch), :]
        acc = jnp.zeros((ch, tq), F32)
        for h in range(n_idx_heads):
            s = lax.dot_general(ki, qi_h[h], NT_DIMS, preferred_element_type=F32)
            acc = acc + jnp.maximum(s, 0.0) * w_row[h]
        kpos = c * ch + lax.broadcasted_iota(jnp.int32, (ch, 1), 0)
        acc = jnp.where(kpos <= qpos, acc, -jnp.inf)
        sc_ref[c] = acc
        return _minmax_update(carry, acc, ka)

    minmax = lax.fori_loop(0, n_ch, score_chunk, _minmax_init((ch, tq), ka))

    n_causal = (qpos + 1).astype(F32)
    kk = jnp.minimum(n_causal, float(n_top))
    theta, jmax = _select_threshold(sc_ref, kk, n_ch, minmax, n_causal, ka)
    _scores_to_bias(sc_ref, n_ch, theta, jmax, ka)

    scale = head_dim ** -0.5
    q = q_ref[0]
    qf = []
    for h in range(n_kv):
        qs = [_rms(q[:, (h * group + g) * head_dim:(h * group + g + 1) * head_dim], gq_ref[...]) * scale
              for g in range(group)]
        qf.append(jnp.concatenate(qs, axis=0))
    qst = [x.astype(BF16) for x in qf]
    gt = group * tq

    def logits(c, h):
        off = pl.multiple_of(c * ch, ch)
        k = kn_ref[0, pl.ds(off, ch), h * head_dim:(h + 1) * head_dim]
        return (lax.dot_general(k, qst[h], NT_DIMS, preferred_element_type=F32)
                + jnp.concatenate([sc_ref[c]] * group, axis=1))

    def softmax_pv(m_rows):
        def sum_chunk(c, carry):
            lf, acc = carry
            lf_out, acc_out = [], []
            for h in range(n_kv):
                vt = vt_ref[0, c, h * head_dim:(h + 1) * head_dim, :]
                p = jnp.exp(logits(c, h) - m_rows[h])
                lf_out.append(lf[h] + _fold(p, jnp.add, ka))
                acc_out.append(acc[h] + jnp.dot(vt, p.astype(BF16), preferred_element_type=F32))
            return tuple(lf_out), tuple(acc_out)

        lf, acc = lax.fori_loop(0, n_ch, sum_chunk, (tuple(jnp.zeros((SUBLANES, gt), F32) for _ in range(n_kv)),
                                                      tuple(jnp.zeros((head_dim, gt), F32) for _ in range(n_kv))))
        for h in range(n_kv):
            out = (acc[h] / jnp.sum(lf[h], axis=0, keepdims=True)).T
            for g in range(group):
                o_ref[0, :, (h * group + g) * head_dim:(h * group + g + 1) * head_dim] = (
                    out[g * tq:(g + 1) * tq].astype(o_ref.dtype))

    k_norm = head_dim ** 0.5 * jnp.max(jnp.abs(gk_ref[...]), axis=-1, keepdims=True)
    ones = jnp.ones((SUBLANES, head_dim), F32)
    bound = [jnp.sqrt(lax.dot_general(ones, x * x, NT_DIMS, preferred_element_type=F32)[0:1, :]) * k_norm * 1.01
             for x in qf]
    worst = bound[0].max()
    for x in bound[1:]:
        worst = jnp.maximum(worst, x.max())
    bound_is_safe = worst <= SAFE_SHIFT

    @pl.when(bound_is_safe)
    def _():
        softmax_pv(bound)

    @pl.when(jnp.logical_not(bound_is_safe))
    def _():
        def max_chunk(c, mf):
            return tuple(jnp.maximum(mf[h], _fold(logits(c, h), jnp.maximum, ka)) for h in range(n_kv))

        mf = lax.fori_loop(0, n_ch, max_chunk, tuple(jnp.full((SUBLANES, gt), NEG, F32) for _ in range(n_kv)))
        softmax_pv([jnp.max(x, axis=0, keepdims=True) for x in mf])


def dsa_prompt(p, ki, kv, g_q, g_k, *, q_block, qi_block, wi_block, n_top, n_idx_heads, idx_dim, n_kv, group,
               head_dim, tq=256, ch=512):
    b, s, _ = p.shape
    qw = n_kv * group * head_dim
    kw = n_kv * head_dim
    iw = n_idx_heads * idx_dim
    vt = jnp.swapaxes(kv[:, :, kw:].reshape(b, s // ch, ch, kw), 2, 3)
    body = functools.partial(_dsa_prompt_body, tq=tq, n_top=n_top, n_idx_heads=n_idx_heads, idx_dim=idx_dim,
                             n_kv=n_kv, group=group, head_dim=head_dim)
    return pl.pallas_call(
        body,
        grid=(b, s // tq),
        in_specs=[pl.BlockSpec((1, tq, qw), lambda i, j: (i, j, q_block)),
                  pl.BlockSpec((1, tq, iw), lambda i, j: (i, j, qi_block)),
                  pl.BlockSpec((1, tq, LANES), lambda i, j: (i, j, wi_block)),
                  pl.BlockSpec((1, s, idx_dim), lambda i, j: (i, 0, 0)),
                  pl.BlockSpec((1, s, kw), lambda i, j: (i, 0, 0)),
                  pl.BlockSpec((1, s // ch, kw, ch), lambda i, j: (i, 0, 0, 0)),
                  pl.BlockSpec((1, head_dim), lambda i, j: (0, 0)),
                  pl.BlockSpec((1, head_dim), lambda i, j: (0, 0))],
        out_specs=pl.BlockSpec((1, tq, qw), lambda i, j: (i, j, 0)),
        out_shape=jax.ShapeDtypeStruct((b, s, qw), BF16),
        scratch_shapes=[pltpu.VMEM((s // ch, ch, tq), F32)],
        compiler_params=_params("parallel", "arbitrary"),
        name="dsa_prompt",
    )(p, p, p, ki, kv, vt, g_q.reshape(1, head_dim), g_k.reshape(1, head_dim))


def _dsa_sample_body(pt_ref, p_ref, kvn_ref, gq_ref, *refs, n_pages, page, n_top, n_idx_heads, idx_dim, n_kv,
                     group, head_dim, q_off, qi_off, ki_off, wi_off):
    del pt_ref
    kidx_refs = refs[:n_pages]
    kv_refs = refs[n_pages:2 * n_pages]
    o_ref, sc_ref = refs[2 * n_pages], refs[2 * n_pages + 1]
    t = p_ref.shape[0]
    kw = n_kv * head_dim

    qi = p_ref[:, qi_off:qi_off + n_idx_heads * idx_dim]
    ki_new = p_ref[:, ki_off:ki_off + idx_dim]
    wi = p_ref[:, wi_off:wi_off + LANES]
    qi_st = jnp.concatenate([qi[:, h * idx_dim:(h + 1) * idx_dim] for h in range(n_idx_heads)], axis=0).astype(BF16)
    w_col = jnp.concatenate([wi[:, h:h + 1] for h in range(n_idx_heads)], axis=0)

    def idx_scores(s):
        s = jnp.maximum(s, 0.0) * w_col
        out = s[0:t]
        for h in range(1, n_idx_heads):
            out = out + s[h * t:(h + 1) * t]
        return out

    ka = 1
    minmax = _minmax_init((t, page), ka)
    for c in range(n_pages):
        s = idx_scores(jnp.dot(qi_st, kidx_refs[c][...].astype(BF16), preferred_element_type=F32))
        sc_ref[c] = s
        minmax = _minmax_update(minmax, s, ka)
    own = idx_scores(lax.dot_general(qi_st, _pad_rows(ki_new, page).astype(BF16), NT_DIMS,
                                     preferred_element_type=F32))
    tok = lax.broadcasted_iota(jnp.int32, (t, page), 0)
    col = lax.broadcasted_iota(jnp.int32, (t, page), 1)
    own = jnp.where(col <= tok, own, -jnp.inf)
    sc_ref[n_pages] = own
    minmax = _minmax_update(minmax, own, ka)

    n_ch = n_pages + 1
    n_finite = (n_pages * page + 1 + lax.broadcasted_iota(jnp.int32, (t, 1), 0)).astype(F32)
    theta, jmax = _select_threshold(sc_ref, jnp.full((t, 1), float(n_top), F32), n_ch, minmax, n_finite, ka)
    _scores_to_bias(sc_ref, n_ch, theta, jmax, ka)

    rstride = 2 * n_kv
    scale = head_dim ** -0.5
    for h in range(n_kv):
        hs = slice(h * head_dim, (h + 1) * head_dim)
        vs = slice(kw + h * head_dim, kw + (h + 1) * head_dim)
        qs = [(_rms(p_ref[:, q_off + (h * group + g) * head_dim:q_off + (h * group + g + 1) * head_dim],
                    gq_ref[...]) * scale).astype(BF16) for g in range(group)]
        qst = jnp.concatenate(qs, axis=0)
        lgs = []
        for c in range(n_ch):
            k = (kv_refs[c][pl.ds(h, page, stride=rstride), :] if c < n_pages
                 else _pad_rows(kvn_ref[:, hs], page))
            lg = lax.dot_general(qst, k.astype(BF16), NT_DIMS, preferred_element_type=F32)
            lgs.append(lg + jnp.concatenate([sc_ref[c]] * group, axis=0))
        m = lgs[0].max(axis=-1, keepdims=True)
        for lg in lgs[1:]:
            m = jnp.maximum(m, lg.max(axis=-1, keepdims=True))
        l = jnp.zeros((group * t, 1), F32)
        acc = jnp.zeros((group * t, head_dim), F32)
        for c in range(n_ch):
            pc = jnp.exp(lgs[c] - m)
            l = l + jnp.sum(pc, axis=-1, keepdims=True)
            v = (kv_refs[c][pl.ds(n_kv + h, page, stride=rstride), :] if c < n_pages
                 else _pad_rows(kvn_ref[:, vs], page))
            acc = acc + jnp.dot(pc.astype(BF16), v.astype(BF16), preferred_element_type=F32)
        out = acc / l
        for g in range(group):
            o_ref[:, (h * group + g) * head_dim:(h * group + g + 1) * head_dim] = (
                out[g * t:(g + 1) * t].astype(o_ref.dtype))


def dsa_sample(p, kv_new, kidx_pool, kv_pool, page_table, g_q, *, t, n_top, n_idx_heads, idx_dim, n_kv, group,
               head_dim, q_off, qi_off, ki_off, wi_off):
    bd, n_pages = page_table.shape
    page = kidx_pool.shape[2]
    qw = n_kv * group * head_dim
    kw = n_kv * head_dim
    n = p.shape[1]
    body = functools.partial(_dsa_sample_body, n_pages=n_pages, page=page, n_top=n_top, n_idx_heads=n_idx_heads,
                             idx_dim=idx_dim, n_kv=n_kv, group=group, head_dim=head_dim, q_off=q_off,
                             qi_off=qi_off, ki_off=ki_off, wi_off=wi_off)
    in_specs = [pl.BlockSpec((t, n), lambda b, pt: (b, 0)),
                pl.BlockSpec((t, 2 * kw), lambda b, pt: (b, 0)),
                pl.BlockSpec((1, head_dim), lambda b, pt: (0, 0))]
    in_specs += [pl.BlockSpec((None, idx_dim, page), functools.partial(lambda b, pt, c: (pt[b, c], 0, 0), c=c))
                 for c in range(n_pages)]
    in_specs += [pl.BlockSpec((None, page * 2 * n_kv, head_dim),
                              functools.partial(lambda b, pt, c: (pt[b, c], 0, 0), c=c))
                 for c in range(n_pages)]
    return pl.pallas_call(
        body,
        grid_spec=pltpu.PrefetchScalarGridSpec(
            num_scalar_prefetch=1,
            grid=(bd,),
            in_specs=in_specs,
            out_specs=pl.BlockSpec((t, qw), lambda b, pt: (b, 0)),
            scratch_shapes=[pltpu.VMEM((n_pages + 1, t, page), F32)]),
        out_shape=jax.ShapeDtypeStruct((bd * t, qw), F32),
        compiler_params=_params("arbitrary"),
        name="dsa_sample",
    )(page_table, p, kv_new, g_q.reshape(1, head_dim), *([kidx_pool] * n_pages), *([kv_pool] * n_pages))


def _dilated_mask(rel, window, dil):
    return (rel >= 0) & (rel <= window) & ((rel & (dil - 1)) == 0)


def _dil_prompt_body(q_ref, k_ref, v_ref, gq_ref, o_ref, lse_ref, m_ref, l_ref, acc_ref, *, window, dil, n_heads,
                     head_dim):
    tq = q_ref.shape[1]
    i, kc, nkc = pl.program_id(1), pl.program_id(2), pl.num_programs(2)
    cidx = i - (nkc - 1) + kc

    @pl.when(kc == 0)
    def _():
        m_ref[...] = jnp.full(m_ref.shape, NEG, F32)
        l_ref[...] = jnp.zeros(l_ref.shape, F32)
        acc_ref[...] = jnp.zeros(acc_ref.shape, F32)

    @pl.when(cidx >= 0)
    def _():
        rel = ((i - cidx) * tq + lax.broadcasted_iota(jnp.int32, (tq, tq), 0)
               - lax.broadcasted_iota(jnp.int32, (tq, tq), 1))
        bias = jnp.where(_dilated_mask(rel, window, dil), 0.0, NEG)
        scale = head_dim ** -0.5
        for h in range(n_heads):
            hs = slice(h * head_dim, (h + 1) * head_dim)
            qn = (_rms(q_ref[0, :, hs], gq_ref[...]) * scale).astype(BF16)
            lg = lax.dot_general(qn, k_ref[0, :, hs].astype(BF16), NT_DIMS, preferred_element_type=F32) + bias
            m_prev = m_ref[h]
            m_new = jnp.maximum(m_prev, jnp.max(lg, axis=-1, keepdims=True))
            alpha = jnp.exp(m_prev - m_new)
            p = jnp.exp(lg - m_new)
            l_ref[h] = alpha * l_ref[h] + jnp.sum(p, axis=-1, keepdims=True)
            acc_ref[h] = alpha * acc_ref[h] + jnp.dot(p.astype(BF16), v_ref[0, :, hs].astype(BF16),
                                                      preferred_element_type=F32)
            m_ref[h] = m_new

    @pl.when(kc == nkc - 1)
    def _():
        for h in range(n_heads):
            hs = slice(h * head_dim, (h + 1) * head_dim)
            o_ref[0, :, hs] = acc_ref[h] / l_ref[h]
            lse_ref[0, :, hs] = jnp.broadcast_to(m_ref[h] + jnp.log(l_ref[h]), (tq, head_dim))


def dilated_prompt_group(p, kn, g_q, *, q_block, k_block, v_block, window, dil, n_heads, head_dim, tq=512):
    b, s, _ = p.shape
    w = n_heads * head_dim
    nkc = -(-window // tq) + 1

    def kv_map(blk):
        return lambda bi, i, kc: (bi, jnp.maximum(i - (nkc - 1) + kc, 0), blk)

    return pl.pallas_call(
        functools.partial(_dil_prompt_body, window=window, dil=dil, n_heads=n_heads, head_dim=head_dim),
        grid=(b, s // tq, nkc),
        in_specs=[pl.BlockSpec((1, tq, w), lambda bi, i, kc: (bi, i, q_block)),
                  pl.BlockSpec((1, tq, w), kv_map(k_block)),
                  pl.BlockSpec((1, tq, w), kv_map(v_block)),
                  pl.BlockSpec((1, head_dim), lambda bi, i, kc: (0, 0))],
        out_specs=[pl.BlockSpec((1, tq, w), lambda bi, i, kc: (bi, i, 0)),
                   pl.BlockSpec((1, tq, w), lambda bi, i, kc: (bi, i, 0))],
        out_shape=[jax.ShapeDtypeStruct((b, s, w), F32), jax.ShapeDtypeStruct((b, s, w), F32)],
        scratch_shapes=[pltpu.VMEM((n_heads, tq, 1), F32), pltpu.VMEM((n_heads, tq, 1), F32),
                        pltpu.VMEM((n_heads, tq, head_dim), F32)],
        compiler_params=_params("parallel", "parallel", "arbitrary"),
        name="dilated_prompt",
    )(p, kn, p, g_q.reshape(1, head_dim))


def _dil_sample_body(p_ref, kn_ref, buf_ref, gq_ref, o_ref, lse_ref, nbuf_ref, *, window, dil, n_heads, head_dim,
                     q_off, k_off, v_off):
    t = p_ref.shape[0]
    rstride = 2 * n_heads
    wb = buf_ref.shape[1] // rstride
    w = n_heads * head_dim
    k_new = kn_ref[:, k_off:k_off + w]
    v_new = p_ref[:, v_off:v_off + w]
    nbuf_ref[0, 0:(wb - t) * rstride, :] = buf_ref[0, t * rstride:wb * rstride, :]
    for c, new in enumerate((k_new, v_new)):
        for h in range(n_heads):
            nbuf_ref[0, pl.ds((wb - t) * rstride + c * n_heads + h, t, stride=rstride), :] = (
                new[:, h * head_dim:(h + 1) * head_dim])

    rel = wb + lax.broadcasted_iota(jnp.int32, (t, wb), 0) - lax.broadcasted_iota(jnp.int32, (t, wb), 1)
    bias = jnp.where(_dilated_mask(rel, window, dil), 0.0, NEG)
    rel_o = lax.broadcasted_iota(jnp.int32, (t, LANES), 0) - lax.broadcasted_iota(jnp.int32, (t, LANES), 1)
    bias_o = jnp.where(_dilated_mask(rel_o, window, dil), 0.0, NEG)
    scale = head_dim ** -0.5
    for h in range(n_heads):
        hs = slice(h * head_dim, (h + 1) * head_dim)
        k_past = buf_ref[0, pl.ds(h, wb, stride=rstride), :]
        v_past = buf_ref[0, pl.ds(n_heads + h, wb, stride=rstride), :]
        qn = (_rms(p_ref[:, q_off + h * head_dim:q_off + (h + 1) * head_dim], gq_ref[...]) * scale).astype(BF16)
        lg = lax.dot_general(qn, k_past.astype(BF16), NT_DIMS, preferred_element_type=F32) + bias
        lg_o = lax.dot_general(qn, _pad_rows(k_new[:, hs], LANES).astype(BF16), NT_DIMS,
                               preferred_element_type=F32) + bias_o
        m = jnp.maximum(jnp.max(lg, axis=-1, keepdims=True), jnp.max(lg_o, axis=-1, keepdims=True))
        pp = jnp.exp(lg - m)
        pp_o = jnp.exp(lg_o - m)
        l = jnp.sum(pp, axis=-1, keepdims=True) + jnp.sum(pp_o, axis=-1, keepdims=True)
        acc = (jnp.dot(pp.astype(BF16), v_past.astype(BF16), preferred_element_type=F32)
               + jnp.dot(pp_o.astype(BF16), _pad_rows(v_new[:, hs], LANES).astype(BF16),
                         preferred_element_type=F32))
        o_ref[:, hs] = acc / l
        lse_ref[:, hs] = jnp.broadcast_to(m + jnp.log(l), (t, head_dim))


def dilated_sample_group(p, kn, buf, g_q, *, t, window, dil, n_heads, head_dim, q_off, k_off, v_off):
    bd, wb, w2 = buf.shape
    w = n_heads * head_dim
    n, nk = p.shape[1], kn.shape[1]
    return pl.pallas_call(
        functools.partial(_dil_sample_body, window=window, dil=dil, n_heads=n_heads, head_dim=head_dim,
                          q_off=q_off, k_off=k_off, v_off=v_off),
        grid=(bd,),
        in_specs=[pl.BlockSpec((t, n), lambda b: (b, 0)),
                  pl.BlockSpec((t, nk), lambda b: (b, 0)),
                  pl.BlockSpec((1, wb, w2), lambda b: (b, 0, 0)),
                  pl.BlockSpec((1, head_dim), lambda b: (0, 0))],
        out_specs=[pl.BlockSpec((t, w), lambda b: (b, 0)),
                   pl.BlockSpec((t, w), lambda b: (b, 0)),
                   pl.BlockSpec((1, wb, w2), lambda b: (b, 0, 0))],
        out_shape=[jax.ShapeDtypeStruct((bd * t, w), F32), jax.ShapeDtypeStruct((bd * t, w), F32),
                   jax.ShapeDtypeStruct((bd, wb, w2), F32)],
        compiler_params=_params("parallel"),
        name="dilated_sample",
    )(p, kn, buf, g_q.reshape(1, head_dim))


def _merge_groups_body(*refs):
    n = (len(refs) - 1) // 2
    o_refs, lse_refs, out_ref = refs[:n], refs[n:2 * n], refs[2 * n]
    lses = [r[...] for r in lse_refs]
    m = lses[0]
    for x in lses[1:]:
        m = jnp.maximum(m, x)
    ws = [jnp.exp(x - m) for x in lses]
    den = ws[0]
    for x in ws[1:]:
        den = den + x
    acc = ws[0] * o_refs[0][...]
    for wgt, o in zip(ws[1:], o_refs[1:]):
        acc = acc + wgt * o[...]
    out_ref[...] = (acc / den).astype(out_ref.dtype)


def merge_groups(outs, lses):
    m, w = outs[0].shape
    tm = min(m, ROW_TILE)
    spec = pl.BlockSpec((tm, w), lambda i: (i, 0))
    return pl.pallas_call(
        _merge_groups_body,
        grid=(m // tm,),
        in_specs=[spec] * (2 * len(outs)),
        out_specs=spec,
        out_shape=jax.ShapeDtypeStruct((m, w), BF16),
        compiler_params=_params("parallel"),
        name="merge_groups",
    )(*outs, *lses)


def _pool_body(u_ref, halo_ref, wg_ref, sc_ref, o_ref, ext_ref, *, n_pre, windows, halo_is_history):
    nb, t, d = u_ref.shape
    gw = d // len(windows)
    i = pl.program_id(1)
    halo = halo_ref[...]
    if halo_is_history:
        halo = jnp.where(i == 0, 0.0, halo)
    ext_ref[:, 0:POOL_HALO, :] = halo
    ext_ref[:, POOL_HALO:POOL_HALO + t, :] = u_ref[...]
    pos = i * t + lax.broadcasted_iota(jnp.int32, (1, t, 1), 1)
    for g, w in enumerate(windows):
        cs = slice(g * gw, (g + 1) * gw)
        acc = ext_ref[:, POOL_HALO:POOL_HALO + t, cs]
        for j in range(1, w):
            acc = acc + ext_ref[:, POOL_HALO - j:POOL_HALO - j + t, cs]
        div = jnp.minimum(w, n_pre + 1 + pos).astype(F32)
        r = acc / div - u_ref[:, :, cs]
        y = jnp.dot(r.reshape(nb * t, gw).astype(BF16), wg_ref[g], preferred_element_type=F32) * sc_ref[:, cs]
        o_ref[:, :, cs] = y.reshape(nb, t, gw).astype(o_ref.dtype)


def pool_mix(u, halo, w_grp, scale, *, n_pre, windows, batch_block, t_block, halo_is_history, out_dtype):
    b, t, d = u.shape
    hb = t_block // POOL_HALO
    if halo_is_history:
        halo_map = lambda bi, i: (bi, jnp.maximum(i * hb - 1, 0), 0)
    else:
        halo_map = lambda bi, i: (bi, 0, 0)
    return pl.pallas_call(
        functools.partial(_pool_body, n_pre=n_pre, windows=windows, halo_is_history=halo_is_history),
        grid=(b // batch_block, t // t_block),
        in_specs=[pl.BlockSpec((batch_block, t_block, d), lambda bi, i: (bi, i, 0)),
                  pl.BlockSpec((batch_block, POOL_HALO, d), halo_map),
                  pl.BlockSpec(w_grp.shape, lambda bi, i: (0, 0, 0)),
                  pl.BlockSpec((1, d), lambda bi, i: (0, 0))],
        out_specs=pl.BlockSpec((batch_block, t_block, d), lambda bi, i: (bi, i, 0)),
        out_shape=jax.ShapeDtypeStruct((b, t, d), out_dtype),
        scratch_shapes=[pltpu.VMEM((batch_block, POOL_HALO + t_block, d), F32)],
        compiler_params=_params("parallel", "parallel"),
        name="pool_mix",
    )(u, halo, w_grp, scale.reshape(1, d))


def _mlstm_body(q_ref, k_ref, v_ref, op_ref, gi_ref, gf_ref, bi_ref, bf_ref, gh_ref, c0_ref, n0_ref, m0_ref,
                y_ref, c_ref, n_ref, m_ref, cs, ns, ms, *, n_heads, head_dim):
    L = q_ref.shape[1]
    ls = max(L, LANES)
    ci = pl.program_id(1)
    hp = lax.Precision.HIGHEST

    @pl.when(ci == 0)
    def _():
        cs[...] = c0_ref[0]
        ns[...] = n0_ref[0]
        ms[...] = m0_ref[0]

    li = gi_ref[0] + bi_ref[...]
    lf = jax.nn.log_sigmoid(gf_ref[0] + bf_ref[...])
    row = lax.broadcasted_iota(jnp.int32, (L, ls), 0)
    col = lax.broadcasted_iota(jnp.int32, (L, ls), 1)
    causal = col <= row
    bt = jnp.dot(jnp.where(causal, 1.0, 0.0), _pad_rows(lf, ls), preferred_element_type=F32, precision=hp)
    lane = lax.broadcasted_iota(jnp.int32, (L, LANES), 1)
    lane1 = lax.broadcasted_iota(jnp.int32, (1, LANES), 1)
    m_all = ms[...]
    for h in range(n_heads):
        hs = slice(h * head_dim, (h + 1) * head_dim)
        bt_h = bt[:, h:h + 1]
        li_h = li[:, h:h + 1]
        m_prev = m_all[:, h:h + 1]
        a_h = jnp.where(lane == 0, bt_h, jnp.where(lane == 1, 1.0, 0.0))
        b_h = jnp.where(lane == 0, 1.0, jnp.where(lane == 1, li_h - bt_h, 0.0))
        dmat = lax.dot_general(a_h, _pad_rows(b_h, ls), NT_DIMS, preferred_element_type=F32, precision=hp)
        dmat = jnp.where(causal, dmat, -jnp.inf)
        g = bt_h + m_prev
        m_t = jnp.maximum(g, jnp.max(dmat, axis=-1, keepdims=True))
        w_intra = jnp.exp(dmat - m_t)
        w_inter = jnp.exp(g - m_t)
        q = q_ref[0, :, hs]
        k = k_ref[0, :, hs] * head_dim ** -0.5
        v = v_ref[0, :, hs]
        qb, kb, vb = q.astype(BF16), _pad_rows(k, ls).astype(BF16), _pad_rows(v, ls).astype(BF16)
        qk = lax.dot_general(qb, kb, NT_DIMS, preferred_element_type=F32) * w_intra
        c_h = cs[h]
        num = (w_inter * jnp.dot(qb, c_h.astype(BF16), preferred_element_type=F32)
               + jnp.dot(qk.astype(BF16), vb, preferred_element_type=F32))
        n_h = ns[h:h + 1, :]
        den = w_inter * jnp.sum(q * n_h, axis=-1, keepdims=True) + jnp.sum(qk, axis=-1, keepdims=True)
        hout = num / jnp.maximum(jnp.abs(den), jnp.exp(-m_t))
        b_last = bt_h[L - 1:L, :]
        decay = b_last - bt_h + li_h
        m_new = jnp.maximum(b_last + m_prev, jnp.max(decay, axis=0, keepdims=True))
        keep = jnp.exp(b_last + m_prev - m_new)
        kw = k * jnp.exp(decay - m_new)
        cs[h] = keep * c_h + lax.dot_general(_pad_rows(kw, ls).astype(BF16), vb, TN_DIMS,
                                             preferred_element_type=F32)
        ns[h:h + 1, :] = keep * n_h + jnp.sum(kw, axis=0, keepdims=True)
        m_all = jnp.where(lane1 == h, m_new, m_all)
        hn = _rms(hout, gh_ref[:, hs])
        y_ref[0, :, hs] = (jax.nn.sigmoid(op_ref[0, :, hs]) * hn).astype(y_ref.dtype)
    ms[...] = m_all

    @pl.when(ci == pl.num_programs(1) - 1)
    def _():
        c_ref[0] = cs[...]
        n_ref[0] = ns[...]
        m_ref[0] = ms[...]


def mlstm(p, gates, b_i, b_f, g_h, c0, n0, m0, *, n_heads, chunk, out_dtype):
    b, t, w4 = p.shape
    w = w4 // 4
    head_dim = w // n_heads
    nc = t // chunk

    def lane_row(x):
        return jnp.pad(x.reshape(1, -1), ((0, 0), (0, LANES - x.size)))

    m0p = jnp.pad(m0.reshape(b, 1, n_heads), ((0, 0), (0, 0), (0, LANES - n_heads)))
    colspec = lambda blk, wd: pl.BlockSpec((1, chunk, wd), lambda bi, ci: (bi, ci, blk))
    const = lambda shape: pl.BlockSpec(shape, lambda bi, ci: (0,) * len(shape))
    state = lambda shape: pl.BlockSpec((1,) + shape, lambda bi, ci: (bi,) + (0,) * len(shape))
    y, c, n, m = pl.pallas_call(
        functools.partial(_mlstm_body, n_heads=n_heads, head_dim=head_dim),
        grid=(b, nc),
        in_specs=[colspec(0, w), colspec(1, w), colspec(2, w), colspec(3, w),
                  colspec(0, LANES), colspec(1, LANES),
                  const((1, LANES)), const((1, LANES)), const((1, w)),
                  state((n_heads, head_dim, head_dim)), state((n_heads, head_dim)), state((1, LANES))],
        out_specs=[colspec(0, w), state((n_heads, head_dim, head_dim)), state((n_heads, head_dim)),
                   state((1, LANES))],
        out_shape=[jax.ShapeDtypeStruct((b, t, w), out_dtype),
                   jax.ShapeDtypeStruct((b, n_heads, head_dim, head_dim), F32),
                   jax.ShapeDtypeStruct((b, n_heads, head_dim), F32),
                   jax.ShapeDtypeStruct((b, 1, LANES), F32)],
        scratch_shapes=[pltpu.VMEM((n_heads, head_dim, head_dim), F32),
                        pltpu.VMEM((n_heads, head_dim), F32),
                        pltpu.VMEM((1, LANES), F32)],
        compiler_params=_params("parallel", "arbitrary"),
        name="mlstm",
    )(p, p, p, p, gates, gates, lane_row(b_i), lane_row(b_f), g_h.reshape(1, w), c0, n0, m0p)
    return y, c, n, m[:, 0, :n_heads]


def kernel(x_prompt, x_sample, cache_kv_a, cache_kidx_a, page_table, cache_kv_b0, cache_kv_b1, cache_kv_b2,
           state_pool_c, state_c_d, state_n_d, state_m_d, cache_mem_kv, mem_prompt,
           g_norm_mix, g_norm_mem, g_norm_ffn, g_mem_src, w_mem_q, w_mem_kv, g_mem_q, g_mem_k, w_mem_o,
           w_ff1, w_ff2, w_in_a, g_q_a, g_k_a, w_out_a, w_in_b, g_q_b, g_k_b, w_out_b,
           w_in_c, w_grp_c, scale_c, w_out_c, w_in_d, b_i_d, b_f_d, g_h_d, w_out_d):
    bp, s, d = x_prompt.shape
    bd, t, _ = x_sample.shape
    depth = g_norm_mix.shape[0]
    n_mixers = 4
    xp = x_prompt.reshape(bp * s, d)
    xs = x_sample.reshape(bd * t, d)
    cache_b = (cache_kv_b0, cache_kv_b1, cache_kv_b2)
    dilated = ((128, 1), (512, 4), (2048, 16))
    pool_windows = (2, 4, 8, 16)
    n_mem, n_heads_mem = mem_prompt.shape[1], cache_mem_kv.shape[4]
    mem2d = mem_prompt.reshape(bp * n_mem, d)
    hd_mem = cache_mem_kv.shape[5]
    mem_cache_rows = jnp.swapaxes(
        cache_mem_kv.reshape(depth, bd, n_mem, 2, n_heads_mem, hd_mem // LANES, LANES), 4, 5
    ).reshape(depth, bd, n_mem * 2 * hd_mem // LANES * n_heads_mem, LANES)

    kv_a_p, ki_a_p, kv_a_s, ki_a_s = [], [], [], []
    kv_b_p = [[] for _ in dilated]
    kv_b_s = [[] for _ in dilated]
    pool_p, pool_s = [], []
    cd_p, nd_p, md_p, cd_s, nd_s, md_s = [], [], [], [], [], []
    mem_p = []

    for i in range(depth):
        kind, j = i % n_mixers, i // n_mixers
        g_mix = g_norm_mix[i]
        if kind == 0:
            n_kv, hd = cache_kv_a.shape[4], cache_kv_a.shape[5]
            idx_dim = cache_kidx_a.shape[3]
            qw = w_out_a.shape[1]
            group = qw // (n_kv * hd)
            kw = n_kv * hd
            n_idx = (w_in_a.shape[2] - qw - 2 * kw - idx_dim) // (idx_dim + 1)
            iw = n_idx * idx_dim
            w = w_in_a[j]
            o3 = qw + 2 * kw
            ki_off = o3 + iw
            wi_off = ki_off + LANES
            n_cols = -(-(wi_off + LANES) // 512) * 512
            wa = jnp.concatenate([w[:, :o3 + iw], _pad_cols(w[:, o3 + iw:o3 + iw + idx_dim], LANES),
                                  _pad_cols(w[:, o3 + iw + idx_dim:], n_cols - wi_off)], axis=1).astype(BF16)
            wo = w_out_a[j].astype(BF16)
            pp = rms_matmul(xp, g_mix, wa)
            ps = rms_matmul(xs, g_mix, wa)
            kvp, kvp_bf = head_norm(pp, qw // kw, kw, hd, g_k_a[j], v_block=qw // kw + 1, with_bf16=True)
            kvs = head_norm(ps, qw // kw, kw, hd, g_k_a[j], v_block=qw // kw + 1)
            kip = pp[:, ki_off:ki_off + idx_dim]
            kis = ps[:, ki_off:ki_off + idx_dim]
            kv_a_p.append(kvp.reshape(bp, s, 2, n_kv, hd))
            ki_a_p.append(kip.reshape(bp, s, idx_dim))
            kv_a_s.append(kvs.reshape(bd, t, 2, n_kv, hd))
            ki_a_s.append(kis.reshape(bd, t, idx_dim))
            dims = dict(n_idx_heads=n_idx, idx_dim=idx_dim, n_kv=n_kv, group=group, head_dim=hd)
            op = dsa_prompt(pp.reshape(bp, s, n_cols), kip.astype(BF16).reshape(bp, s, idx_dim),
                            kvp_bf.reshape(bp, s, 2 * kw), g_q_a[j], g_k_a[j], q_block=0, qi_block=o3 // iw,
                            wi_block=wi_off // LANES, n_top=min(256, s // 4), **dims)
            n_pool, page = cache_kidx_a.shape[1], cache_kidx_a.shape[2]
            past = page_table.shape[1] * page
            kidx_pool = jnp.swapaxes(cache_kidx_a[j], 1, 2)
            kv_pool = cache_kv_a[j].reshape(n_pool, page * 2 * n_kv, hd)
            os_ = dsa_sample(ps, kvs, kidx_pool, kv_pool, page_table,
                             g_q_a[j], t=t, n_top=min(256, (past + t) // 4), q_off=0, qi_off=o3, ki_off=ki_off,
                             wi_off=wi_off, **dims)
            xp = matmul_res(op.reshape(bp * s, qw), wo, xp)
            xs = matmul_res(os_, wo, xs)
        elif kind == 1:
            n_groups = len(dilated)
            hg, hd = cache_kv_b0.shape[4], cache_kv_b0.shape[5]
            gw = hg * hd
            wb_ = w_in_b[j].astype(BF16)
            wo = w_out_b[j].astype(BF16)
            pp = rms_matmul(xp, g_mix, wb_)
            ps = rms_matmul(xs, g_mix, wb_)
            gk_row = jnp.repeat(g_k_b[j], hg, axis=0).reshape(1, n_groups * gw)
            knp = head_norm_groups(pp, 1, n_groups * gw, gk_row)
            kns = head_norm_groups(ps, 1, n_groups * gw, gk_row)
            pp3 = pp.reshape(bp, s, -1)
            knp3 = knp.reshape(bp, s, -1)
            outs_p, lses_p, outs_s, lses_s = [], [], [], []
            for g, (win, dil) in enumerate(dilated):
                o, lse = dilated_prompt_group(pp3, knp3, g_q_b[j, g], q_block=g, k_block=g,
                                              v_block=2 * n_groups + g, window=win, dil=dil, n_heads=hg,
                                              head_dim=hd)
                outs_p.append(o.reshape(bp * s, gw))
                lses_p.append(lse.reshape(bp * s, gw))
                wb = min(win, s)
                kv_b_p[g].append(jnp.stack(
                    [knp3[:, s - wb:, g * gw:(g + 1) * gw].reshape(bp, wb, hg, hd),
                     pp3[:, s - wb:, (2 * n_groups + g) * gw:(2 * n_groups + g + 1) * gw].reshape(bp, wb, hg, hd)],
                    axis=2))
                buf = cache_b[g][j]
                o, lse, nbuf = dilated_sample_group(ps, kns, buf.reshape(bd, buf.shape[1] * 2 * hg, hd), g_q_b[j, g],
                                                    t=t, window=win, dil=dil, n_heads=hg, head_dim=hd,
                                                    q_off=g * gw, k_off=g * gw, v_off=(2 * n_groups + g) * gw)
                outs_s.append(o)
                lses_s.append(lse)
                kv_b_s[g].append(nbuf.reshape(buf.shape))
            xp = matmul_res(merge_groups(outs_p, lses_p), wo, xp)
            xs = matmul_res(merge_groups(outs_s, lses_s), wo, xs)
        elif kind == 2:
            wi_, wo = w_in_c[j].astype(BF16), w_out_c[j].astype(BF16)
            wg = w_grp_c[j].astype(BF16)
            up = rms_matmul(xp, g_mix, wi_).reshape(bp, s, d)
            us = rms_matmul(xs, g_mix, wi_).reshape(bd, t, d)
            n_state = state_pool_c.shape[2]
            yp = pool_mix(up, up, wg, scale_c[j], n_pre=0, windows=pool_windows, batch_block=1, t_block=512,
                          halo_is_history=True, out_dtype=BF16)
            halo = jnp.pad(state_pool_c[j], ((0, 0), (POOL_HALO - n_state, 0), (0, 0)))
            ys = pool_mix(us, halo, wg, scale_c[j], n_pre=n_state, windows=pool_windows, batch_block=min(bd, 32),
                          t_block=t, halo_is_history=False, out_dtype=F32)
            pool_p.append(up[:, s - n_state:])
            pool_s.append(jnp.concatenate([state_pool_c[j], us], axis=1)[:, -n_state:])
            xp = matmul_res(yp.reshape(bp * s, d), wo, xp)
            xs = matmul_res(ys.reshape(bd * t, d), wo, xs)
        else:
            nh, hd = state_c_d.shape[2], state_c_d.shape[3]
            wd = nh * hd
            w = w_in_d[j]
            w_main = w[:, :4 * wd].astype(BF16)
            w_gate = jnp.concatenate([_pad_cols(w[:, 4 * wd:4 * wd + nh], LANES),
                                      _pad_cols(w[:, 4 * wd + nh:], LANES)], axis=1)
            wo = w_out_d[j].astype(BF16)
            pp = rms_matmul(xp, g_mix, w_main).reshape(bp, s, 4 * wd)
            ps = rms_matmul(xs, g_mix, w_main).reshape(bd, t, 4 * wd)
            gp = rms_matmul(xp, g_mix, w_gate, precise=True).reshape(bp, s, 2 * LANES)
            gs = rms_matmul(xs, g_mix, w_gate, precise=True).reshape(bd, t, 2 * LANES)
            zeros = lambda *shape: jnp.zeros(shape, F32)
            yp, c, n, m = mlstm(pp, gp, b_i_d[j], b_f_d[j], g_h_d[j], zeros(bp, nh, hd, hd), zeros(bp, nh, hd),
                                zeros(bp, nh), n_heads=nh, chunk=256, out_dtype=BF16)
            cd_p.append(c)
            nd_p.append(n)
            md_p.append(m)
            ys, c, n, m = mlstm(ps, gs, b_i_d[j], b_f_d[j], g_h_d[j], state_c_d[j], state_n_d[j], state_m_d[j],
                                n_heads=nh, chunk=t, out_dtype=F32)
            cd_s.append(c)
            nd_s.append(n)
            md_s.append(m)
            xp = matmul_res(yp.reshape(bp * s, wd), wo, xp)
            xs = matmul_res(ys.reshape(bd * t, wd), wo, xs)

        wm = n_heads_mem * cache_mem_kv.shape[5]
        mkv_raw = rms_matmul(mem2d, g_mem_src[i], w_mem_kv[i].astype(BF16))
        mkv = head_norm(mkv_raw, 0, wm, wm // n_heads_mem, g_mem_k[i], v_block=1)
        mem_p.append(mkv.reshape(bp, n_mem, 2, n_heads_mem, wm // n_heads_mem))
        wq, wo = w_mem_q[i].astype(BF16), w_mem_o[i].astype(BF16)
        qp = rms_matmul(xp, g_norm_mem[i], wq).reshape(bp, s, wm)
        qs = rms_matmul(xs, g_norm_mem[i], wq).reshape(bd, t, wm)
        ap = mem_attn(qp, mkv.reshape(bp, n_mem, 2 * wm), g_mem_q[i], n_heads=n_heads_mem, n_mem=n_mem,
                      batch_block=1, q_block=512, out_dtype=BF16)
        as_ = mem_attn(qs, mem_cache_rows, g_mem_q[i], n_heads=n_heads_mem, n_mem=n_mem, batch_block=4,
                       q_block=t, out_dtype=F32, layer=i)
        xp = matmul_res(ap.reshape(bp * s, wm), wo, xp)
        xs = matmul_res(as_.reshape(bd * t, wm), wo, xs)

        w1, w2 = w_ff1[i].astype(BF16), w_ff2[i].astype(BF16)
        xp = ffn(xp, g_norm_ffn[i], w1, w2)
        xs = ffn(xs, g_norm_ffn[i], w1, w2)

    return (xp.reshape(bp, s, d), xs.reshape(bd, t, d),
            jnp.stack(kv_a_p), jnp.stack(ki_a_p), jnp.stack(kv_a_s), jnp.stack(ki_a_s),
            jnp.stack(kv_b_p[0]), jnp.stack(kv_b_p[1]), jnp.stack(kv_b_p[2]),
            jnp.stack(kv_b_s[0]), jnp.stack(kv_b_s[1]), jnp.stack(kv_b_s[2]),
            jnp.stack(pool_p), jnp.stack(pool_s),
            jnp.stack(cd_p), jnp.stack(nd_p), jnp.stack(md_p),
            jnp.stack(cd_s), jnp.stack(nd_s), jnp.stack(md_s),
            jnp.stack(mem_p))
```

```python
import functools

import jax
import jax.numpy as jnp
from jax import lax
from jax.experimental import pallas as pl
from jax.experimental.pallas import tpu as pltpu

F32 = jnp.float32
BF16 = jnp.bfloat16
EPS = 1e-6
NEG = -1e30
LANES = 128
SUBLANES = 8
VMEM_LIMIT_BYTES = 56 * 2**20
ROW_TILE = 1024
POOL_HALO = 16
MAX_BISECTIONS = 512
SAFE_SHIFT = 40.0
LOG2E = 1.4426950408889634
FOLD_CHAINS = 4

NT_DIMS = (((1,), (1,)), ((), ()))
TN_DIMS = (((0,), (0,)), ((), ()))


def _params(*sem):
    return pltpu.CompilerParams(dimension_semantics=sem, vmem_limit_bytes=VMEM_LIMIT_BYTES)


def _rms(x, g):
    return x * lax.rsqrt(jnp.mean(x * x, axis=-1, keepdims=True) + EPS) * g


def _col_tile(n):
    for t in (1536, 1024, 768, 512, 384, 256, 128):
        if n % t == 0:
            return t
    raise ValueError(f"matmul width {n} is not a multiple of {LANES}")


def _pad_rows(x, rows):
    if x.shape[0] == rows:
        return x
    return jnp.concatenate([x, jnp.zeros((rows - x.shape[0],) + x.shape[1:], x.dtype)], axis=0)


def _pad_cols(w, n):
    return jnp.pad(w, ((0, 0), (0, n - w.shape[1])))


def _rms_matmul_body(x_ref, g_ref, w_ref, o_ref, h_ref, *, precision):
    @pl.when(pl.program_id(1) == 0)
    def _():
        h_ref[...] = _rms(x_ref[...], g_ref[...]).astype(h_ref.dtype)

    o_ref[...] = jnp.dot(h_ref[...], w_ref[...], preferred_element_type=F32, precision=precision)


def rms_matmul(x, g, w, *, precise=False):
    m, d = x.shape
    n = w.shape[1]
    tm, tn = min(m, ROW_TILE), _col_tile(n)
    return pl.pallas_call(
        functools.partial(_rms_matmul_body, precision=lax.Precision.HIGHEST if precise else None),
        grid=(m // tm, n // tn),
        in_specs=[pl.BlockSpec((tm, d), lambda i, j: (i, 0)),
                  pl.BlockSpec((1, d), lambda i, j: (0, 0)),
                  pl.BlockSpec((d, tn), lambda i, j: (0, j))],
        out_specs=pl.BlockSpec((tm, tn), lambda i, j: (i, j)),
        out_shape=jax.ShapeDtypeStruct((m, n), F32),
        scratch_shapes=[pltpu.VMEM((tm, d), F32 if precise else BF16)],
        compiler_params=_params("parallel", "arbitrary"),
        name="rms_matmul",
    )(x, g.reshape(1, d), w)


def _matmul_res_body(a_ref, w_ref, r_ref, o_ref):
    o_ref[...] = r_ref[...] + jnp.dot(a_ref[...].astype(BF16), w_ref[...], preferred_element_type=F32)


def matmul_res(a, w, r):
    m, k = a.shape
    n = w.shape[1]
    tm, tn = min(m, ROW_TILE), _col_tile(n)
    return pl.pallas_call(
        _matmul_res_body,
        grid=(m // tm, n // tn),
        in_specs=[pl.BlockSpec((tm, k), lambda i, j: (i, 0)),
                  pl.BlockSpec((k, tn), lambda i, j: (0, j)),
                  pl.BlockSpec((tm, tn), lambda i, j: (i, j))],
        out_specs=pl.BlockSpec((tm, tn), lambda i, j: (i, j)),
        out_shape=jax.ShapeDtypeStruct((m, n), F32),
        compiler_params=_params("parallel", "parallel"),
        name="matmul_res",
    )(a, w, r)


def _ffn_body(x_ref, g_ref, w1_ref, w2_ref, o_ref, h_ref, acc_ref):
    f = pl.program_id(1)

    @pl.when(f == 0)
    def _():
        h_ref[...] = _rms(x_ref[...], g_ref[...]).astype(BF16)
        acc_ref[...] = jnp.zeros_like(acc_ref)

    a = jnp.dot(h_ref[...], w1_ref[...], preferred_element_type=F32)
    a = jnp.square(jnp.maximum(a, 0.0)).astype(BF16)
    acc_ref[...] += jnp.dot(a, w2_ref[...], preferred_element_type=F32)

    @pl.when(f == pl.num_programs(1) - 1)
    def _():
        o_ref[...] = x_ref[...] + acc_ref[...]


def ffn(x, g, w1, w2):
    m, d = x.shape
    ff = w1.shape[1]
    tm, tf = min(m, ROW_TILE), 1024
    return pl.pallas_call(
        _ffn_body,
        grid=(m // tm, ff // tf),
        in_specs=[pl.BlockSpec((tm, d), lambda i, f: (i, 0)),
                  pl.BlockSpec((1, d), lambda i, f: (0, 0)),
                  pl.BlockSpec((d, tf), lambda i, f: (0, f)),
                  pl.BlockSpec((tf, d), lambda i, f: (f, 0))],
        out_specs=pl.BlockSpec((tm, d), lambda i, f: (i, 0)),
        out_shape=jax.ShapeDtypeStruct((m, d), F32),
        scratch_shapes=[pltpu.VMEM((tm, d), BF16), pltpu.VMEM((tm, d), F32)],
        compiler_params=_params("parallel", "arbitrary"),
        name="ffn",
    )(x, g.reshape(1, d), w1, w2)


def _head_norm_body(*refs, head_dim, n_heads, with_copy, with_bf16):
    k_ref, g_ref = refs[0], refs[1]
    v_ref = refs[2] if with_copy else None
    outs = refs[3 if with_copy else 2:]
    width = n_heads * head_dim
    for h in range(n_heads):
        sl = slice(h * head_dim, (h + 1) * head_dim)
        kn = _rms(k_ref[:, sl], g_ref[:, sl])
        for o in outs:
            o[:, sl] = kn.astype(o.dtype)
    if with_copy:
        v = v_ref[...]
        for o in outs:
            o[:, width:] = v.astype(o.dtype)


def head_norm(p, k_block, width, head_dim, gain, *, v_block=None, with_bf16=False):
    m = p.shape[0]
    n_heads = width // head_dim
    tm = min(m, ROW_TILE)
    with_copy = v_block is not None
    ow = 2 * width if with_copy else width
    in_specs = [pl.BlockSpec((tm, width), lambda i: (i, k_block)),
                pl.BlockSpec((1, width), lambda i: (0, 0))]
    args = [p, jnp.tile(gain.reshape(1, head_dim), (1, n_heads))]
    if with_copy:
        in_specs.append(pl.BlockSpec((tm, width), lambda i: (i, v_block)))
        args.append(p)
    out_shape = [jax.ShapeDtypeStruct((m, ow), F32)]
    if with_bf16:
        out_shape.append(jax.ShapeDtypeStruct((m, ow), BF16))
    out = pl.pallas_call(
        functools.partial(_head_norm_body, head_dim=head_dim, n_heads=n_heads, with_copy=with_copy,
                          with_bf16=with_bf16),
        grid=(m // tm,),
        in_specs=in_specs,
        out_specs=[pl.BlockSpec((tm, ow), lambda i: (i, 0)) for _ in out_shape],
        out_shape=out_shape,
        compiler_params=_params("parallel"),
        name="head_norm",
    )(*args)
    return out if with_bf16 else out[0]


def head_norm_groups(p, k_block, width, gain_row):
    m = p.shape[0]
    tm = min(m, ROW_TILE)
    n_heads = width // LANES
    return pl.pallas_call(
        functools.partial(_head_norm_body, head_dim=LANES, n_heads=n_heads, with_copy=False, with_bf16=False),
        grid=(m // tm,),
        in_specs=[pl.BlockSpec((tm, width), lambda i: (i, k_block)),
                  pl.BlockSpec((1, width), lambda i: (0, 0))],
        out_specs=[pl.BlockSpec((tm, width), lambda i: (i, 0))],
        out_shape=[jax.ShapeDtypeStruct((m, width), F32)],
        compiler_params=_params("parallel"),
        name="head_norm_groups",
    )(p, gain_row)[0]


def _mem_attn_body(q_ref, kv_ref, gq_ref, o_ref, *, n_heads, head_dim, n_mem, tiled_rows):
    width = n_heads * head_dim
    scale = head_dim ** -0.5
    n_lt = head_dim // LANES
    rstride = 2 * n_lt * n_heads

    def tiled(b, c, h):
        parts = [kv_ref[b, pl.ds((c * n_lt + lt) * n_heads + h, n_mem, stride=rstride), :] for lt in range(n_lt)]
        return jnp.concatenate(parts, axis=1)

    for b in range(q_ref.shape[0]):
        for h in range(n_heads):
            sl = slice(h * head_dim, (h + 1) * head_dim)
            qn = (_rms(q_ref[b, :, sl], gq_ref[...]) * scale).astype(BF16)
            if tiled_rows:
                k, v = tiled(b, 0, h).astype(BF16), tiled(b, 1, h).astype(BF16)
            else:
                k = kv_ref[b, :, sl].astype(BF16)
                v = kv_ref[b, :, width + h * head_dim:width + (h + 1) * head_dim].astype(BF16)
            lg = lax.dot_general(qn, k, NT_DIMS, preferred_element_type=F32)
            p = jnp.exp(lg - jnp.max(lg, axis=-1, keepdims=True))
            l = jnp.sum(p, axis=-1, keepdims=True)
            o = jnp.dot(p.astype(BF16), v, preferred_element_type=F32) / l
            o_ref[b, :, sl] = o.astype(o_ref.dtype)


def mem_attn(q, kv, g_q, *, n_heads, n_mem, batch_block, q_block, out_dtype, layer=None):
    b, t, w = q.shape
    head_dim = w // n_heads
    if layer is None:
        kv_spec = pl.BlockSpec((batch_block, n_mem, 2 * w), lambda i, j: (i, 0, 0))
    else:
        kv_spec = pl.BlockSpec((None, batch_block) + kv.shape[2:], lambda i, j: (layer, i, 0, 0))
    return pl.pallas_call(
        functools.partial(_mem_attn_body, n_heads=n_heads, head_dim=head_dim, n_mem=n_mem,
                          tiled_rows=layer is not None),
        grid=(b // batch_block, t // q_block),
        in_specs=[pl.BlockSpec((batch_block, q_block, w), lambda i, j: (i, j, 0)),
                  kv_spec,
                  pl.BlockSpec((1, head_dim), lambda i, j: (0, 0))],
        out_specs=pl.BlockSpec((batch_block, q_block, w), lambda i, j: (i, j, 0)),
        out_shape=jax.ShapeDtypeStruct((b, t, w), out_dtype),
        compiler_params=_params("parallel", "parallel"),
        name="mem_attn",
    )(q, kv, g_q.reshape(1, head_dim))


def _fold(x, op, ka):
    if ka == 1:
        parts = [x[:, j * LANES:(j + 1) * LANES] for j in range(x.shape[1] // LANES)]
    else:
        parts = [x[j * SUBLANES:(j + 1) * SUBLANES, :] for j in range(x.shape[0] // SUBLANES)]
    accs = parts[:FOLD_CHAINS]
    for i, part in enumerate(parts[FOLD_CHAINS:]):
        accs[i % FOLD_CHAINS] = op(accs[i % FOLD_CHAINS], part)
    while len(accs) > 1:
        accs = [op(accs[i], accs[i + 1]) if i + 1 < len(accs) else accs[i] for i in range(0, len(accs), 2)]
    return accs[0]


def _fold_lanes(x):
    return _fold(x, jnp.add, 1)


def _folded_shape(chunk_shape, ka):
    return (chunk_shape[0], LANES) if ka == 1 else (SUBLANES, chunk_shape[1])


def _chunk_loop(n_ch, body, init):
    if isinstance(n_ch, int):
        carry = init
        for c in range(n_ch):
            carry = body(c, carry)
        return carry
    return lax.fori_loop(0, n_ch, body, init)


def _count(sc_ref, n_ch, pred, ka):
    def body(c, acc):
        return acc + _fold(jnp.where(pred(sc_ref[c], c), 1.0, 0.0), jnp.add, ka)

    acc = _chunk_loop(n_ch, body, jnp.zeros(_folded_shape(sc_ref.shape[1:], ka), F32))
    return jnp.sum(acc, axis=ka, keepdims=True)


def _minmax_update(carry, s, ka):
    mn, mx = carry
    mn = jnp.minimum(mn, _fold(jnp.where(s > -jnp.inf, s, jnp.inf), jnp.minimum, ka))
    return mn, jnp.maximum(mx, _fold(s, jnp.maximum, ka))


def _minmax_init(chunk_shape, ka):
    shape = _folded_shape(chunk_shape, ka)
    return jnp.full(shape, jnp.inf, F32), jnp.full(shape, -jnp.inf, F32)


def _key_index(c, chunk_shape, ka):
    shape = (1, chunk_shape[1]) if ka == 1 else (chunk_shape[0], 1)
    return (c * chunk_shape[ka] + lax.broadcasted_iota(jnp.int32, shape, ka)).astype(F32)


def _select_threshold(sc_ref, kk, n_ch, minmax, n_finite, ka):
    chunk = sc_ref.shape[1:]
    ch = chunk[ka]
    lo = jnp.min(minmax[0], axis=ka, keepdims=True)
    mx = jnp.max(minmax[1], axis=ka, keepdims=True)
    hi = mx + jnp.maximum(jnp.abs(mx) * 2.0**-22, 1e-36)
    c_lo = n_finite
    done = jnp.where(c_lo == kk, 1.0, 0.0)

    def cond(st):
        return (jnp.min(st[3]) < 0.5) & (st[4] < MAX_BISECTIONS)

    def body(st):
        lo, hi, c_lo, done, it = st
        mid = lo * 0.5 + hi * 0.5
        stuck = (mid <= lo) | (mid >= hi)
        c_mid = _count(sc_ref, n_ch, lambda s, c: s >= mid, ka)
        live = (done < 0.5) & jnp.logical_not(stuck)
        up = live & (c_mid >= kk)
        dn = live & (c_mid < kk)
        lo = jnp.where(up, mid, lo)
        c_lo = jnp.where(up, c_mid, c_lo)
        hi = jnp.where(dn, mid, hi)
        done = jnp.where(stuck | (c_lo == kk), 1.0, done)
        return lo, hi, c_lo, done, it + 1

    theta, _, c_theta, _, _ = lax.while_loop(cond, body, (lo, hi, c_lo, done, jnp.int32(0)))

    n_keys = n_ch * ch
    tie = c_theta > kk
    no_tie_jmax = jnp.full(kk.shape, 2.0**30, F32)

    def resolve(_):
        need = kk - _count(sc_ref, n_ch, lambda s, c: s > theta, ka)
        n_iter = max(1, (sc_ref.shape[0] * ch - 1).bit_length())

        def step(_, st):
            jl, jh = st
            jm = jnp.floor((jl + jh) * 0.5)
            c_m = _count(sc_ref, n_ch, lambda s, c: (s == theta) & (_key_index(c, chunk, ka) <= jm), ka)
            ok = c_m >= need
            return jnp.where(ok, jl, jm + 1.0), jnp.where(ok, jm, jh)

        jl0 = jnp.zeros(kk.shape, F32)
        jh0 = jnp.zeros(kk.shape, F32) + (n_keys - 1)
        _, jh = lax.fori_loop(0, n_iter, step, (jl0, jh0))
        return jnp.where(tie, jh, no_tie_jmax)

    any_tie = jnp.max(jnp.where(tie, 1.0, 0.0)) > 0.5
    jmax = lax.cond(any_tie, resolve, lambda _: no_tie_jmax, 0)
    return theta, jmax


def _scores_to_bias(sc_ref, n_ch, theta, jmax, ka):
    chunk = sc_ref.shape[1:]

    def body(c, carry):
        s = sc_ref[c]
        sel = (s > theta) | ((s == theta) & (_key_index(c, chunk, ka) <= jmax))
        sc_ref[c] = jnp.where(sel, 0.0, NEG)
        return carry

    _chunk_loop(n_ch, body, 0)


def _dsa_prompt_body(q_ref, qi_ref, wi_ref, ki_ref, kn_ref, vt_ref, gq_ref, gk_ref, o_ref,
                     sc_ref, *, tq, n_top, n_idx_heads, idx_dim, n_kv, group, head_dim):
    ka = 0
    ch = sc_ref.shape[1]
    t0 = pl.program_id(1) * tq
    n_ch = (t0 + tq + ch - 1) // ch
    qpos = t0 + lax.broadcasted_iota(jnp.int32, (1, tq), 1)

    qi = qi_ref[0]
    wi_t = wi_ref[0].T
    qi_h = [qi[:, h * idx_dim:(h + 1) * idx_dim].astype(BF16) for h in range(n_idx_heads)]
    w_row = [wi_t[h:h + 1, :] for h in range(n_idx_heads)]

    def score_chunk(c, carry):
        ki = ki_ref[0, pl.ds(pl.multiple_of(c * ch, ch), ch), :]
        acc = jnp.zeros((ch, tq), F32)
        for h in range(n_idx_heads):
            s = lax.dot_general(ki, qi_h[h], NT_DIMS, preferred_element_type=F32)
            acc = acc + jnp.maximum(s, 0.0) * w_row[h]
        kpos = c * ch + lax.broadcasted_iota(jnp.int32, (ch, 1), 0)
        acc = jnp.where(kpos <= qpos, acc, -jnp.inf)
        sc_ref[c] = acc
        return _minmax_update(carry, acc, ka)

    minmax = lax.fori_loop(0, n_ch, score_chunk, _minmax_init((ch, tq), ka))

    n_causal = (qpos + 1).astype(F32)
    kk = jnp.minimum(n_causal, float(n_top))
    theta, jmax = _select_threshold(sc_ref, kk, n_ch, minmax, n_causal, ka)
    _scores_to_bias(sc_ref, n_ch, theta, jmax, ka)

    scale = head_dim ** -0.5 * LOG2E
    q = q_ref[0]
    qf = []
    for h in range(n_kv):
        qs = [_rms(q[:, (h * group + g) * head_dim:(h * group + g + 1) * head_dim], gq_ref[...]) * scale
              for g in range(group)]
        qf.append(jnp.concatenate(qs, axis=0))
    qst = [x.astype(BF16) for x in qf]
    gt = group * tq

    def logits(c, h):
        off = pl.multiple_of(c * ch, ch)
        k = kn_ref[0, pl.ds(off, ch), h * head_dim:(h + 1) * head_dim]
        return (lax.dot_general(k, qst[h], NT_DIMS, preferred_element_type=F32)
                + jnp.concatenate([sc_ref[c]] * group, axis=1))

    def softmax_pv(m_rows):
        def sum_chunk(c, carry):
            lf, acc = carry
            lf_out, acc_out = [], []
            for h in range(n_kv):
                vt = vt_ref[0, c, h * head_dim:(h + 1) * head_dim, :]
                p = jnp.exp2(logits(c, h) - m_rows[h])
                lf_out.append(lf[h] + _fold(p, jnp.add, ka))
                acc_out.append(acc[h] + jnp.dot(vt, p.astype(BF16), preferred_element_type=F32))
            return tuple(lf_out), tuple(acc_out)

        lf, acc = lax.fori_loop(0, n_ch, sum_chunk, (tuple(jnp.zeros((SUBLANES, gt), F32) for _ in range(n_kv)),
                                                      tuple(jnp.zeros((head_dim, gt), F32) for _ in range(n_kv))))
        for h in range(n_kv):
            out = (acc[h] / jnp.sum(lf[h], axis=0, keepdims=True)).T
            for g in range(group):
                o_ref[0, :, (h * group + g) * head_dim:(h * group + g + 1) * head_dim] = (
                    out[g * tq:(g + 1) * tq].astype(o_ref.dtype))

    k_norm = head_dim ** 0.5 * jnp.max(jnp.abs(gk_ref[...]), axis=-1, keepdims=True)
    ones = jnp.ones((SUBLANES, head_dim), F32)
    bound = [jnp.sqrt(lax.dot_general(ones, x * x, NT_DIMS, preferred_element_type=F32)[0:1, :]) * k_norm * 1.01
             for x in qf]
    worst = bound[0].max()
    for x in bound[1:]:
        worst = jnp.maximum(worst, x.max())
    bound_is_safe = worst <= SAFE_SHIFT * LOG2E

    @pl.when(bound_is_safe)
    def _():
        softmax_pv(bound)

    @pl.when(jnp.logical_not(bound_is_safe))
    def _():
        def max_chunk(c, mf):
            return tuple(jnp.maximum(mf[h], _fold(logits(c, h), jnp.maximum, ka)) for h in range(n_kv))

        mf = lax.fori_loop(0, n_ch, max_chunk, tuple(jnp.full((SUBLANES, gt), NEG, F32) for _ in range(n_kv)))
        softmax_pv([jnp.max(x, axis=0, keepdims=True) for x in mf])


def dsa_prompt(p, ki, kv, g_q, g_k, *, q_block, qi_block, wi_block, n_top, n_idx_heads, idx_dim, n_kv, group,
               head_dim, tq=256, ch=512):
    b, s, _ = p.shape
    qw = n_kv * group * head_dim
    kw = n_kv * head_dim
    iw = n_idx_heads * idx_dim
    vt = jnp.swapaxes(kv[:, :, kw:].reshape(b, s // ch, ch, kw), 2, 3)
    body = functools.partial(_dsa_prompt_body, tq=tq, n_top=n_top, n_idx_heads=n_idx_heads, idx_dim=idx_dim,
                             n_kv=n_kv, group=group, head_dim=head_dim)
    return pl.pallas_call(
        body,
        grid=(b, s // tq),
        in_specs=[pl.BlockSpec((1, tq, qw), lambda i, j: (i, j, q_block)),
                  pl.BlockSpec((1, tq, iw), lambda i, j: (i, j, qi_block)),
                  pl.BlockSpec((1, tq, LANES), lambda i, j: (i, j, wi_block)),
                  pl.BlockSpec((1, s, idx_dim), lambda i, j: (i, 0, 0), pipeline_mode=pl.Buffered(1)),
                  pl.BlockSpec((1, s, kw), lambda i, j: (i, 0, 0), pipeline_mode=pl.Buffered(1)),
                  pl.BlockSpec((1, s // ch, kw, ch), lambda i, j: (i, 0, 0, 0), pipeline_mode=pl.Buffered(1)),
                  pl.BlockSpec((1, head_dim), lambda i, j: (0, 0)),
                  pl.BlockSpec((1, head_dim), lambda i, j: (0, 0))],
        out_specs=pl.BlockSpec((1, tq, qw), lambda i, j: (i, j, 0)),
        out_shape=jax.ShapeDtypeStruct((b, s, qw), BF16),
        scratch_shapes=[pltpu.VMEM((s // ch, ch, tq), F32)],
        compiler_params=_params("parallel", "arbitrary"),
        name="dsa_prompt",
    )(p, p, p, ki, kv, vt, g_q.reshape(1, head_dim), g_k.reshape(1, head_dim))


def _dsa_sample_body(pt_ref, p_ref, kvn_ref, gq_ref, *refs, n_pages, page, n_top, n_idx_heads, idx_dim, n_kv,
                     group, head_dim, q_off, qi_off, ki_off, wi_off):
    del pt_ref
    kidx_refs = refs[:n_pages]
    kv_refs = refs[n_pages:2 * n_pages]
    o_ref, sc_ref = refs[2 * n_pages], refs[2 * n_pages + 1]
    t = p_ref.shape[0]
    kw = n_kv * head_dim

    qi = p_ref[:, qi_off:qi_off + n_idx_heads * idx_dim]
    ki_new = p_ref[:, ki_off:ki_off + idx_dim]
    wi = p_ref[:, wi_off:wi_off + LANES]
    qi_st = jnp.concatenate([qi[:, h * idx_dim:(h + 1) * idx_dim] for h in range(n_idx_heads)], axis=0).astype(BF16)
    w_col = jnp.concatenate([wi[:, h:h + 1] for h in range(n_idx_heads)], axis=0)

    def idx_scores(s):
        s = jnp.maximum(s, 0.0) * w_col
        out = s[0:t]
        for h in range(1, n_idx_heads):
            out = out + s[h * t:(h + 1) * t]
        return out

    ka = 1
    minmax = _minmax_init((t, page), ka)
    for c in range(n_pages):
        s = idx_scores(jnp.dot(qi_st, kidx_refs[c][...].astype(BF16), preferred_element_type=F32))
        sc_ref[c] = s
        minmax = _minmax_update(minmax, s, ka)
    own = idx_scores(lax.dot_general(qi_st, _pad_rows(ki_new, page).astype(BF16), NT_DIMS,
                                     preferred_element_type=F32))
    tok = lax.broadcasted_iota(jnp.int32, (t, page), 0)
    col = lax.broadcasted_iota(jnp.int32, (t, page), 1)
    own = jnp.where(col <= tok, own, -jnp.inf)
    sc_ref[n_pages] = own
    minmax = _minmax_update(minmax, own, ka)

    n_ch = n_pages + 1
    n_finite = (n_pages * page + 1 + lax.broadcasted_iota(jnp.int32, (t, 1), 0)).astype(F32)
    theta, jmax = _select_threshold(sc_ref, jnp.full((t, 1), float(n_top), F32), n_ch, minmax, n_finite, ka)
    _scores_to_bias(sc_ref, n_ch, theta, jmax, ka)

    rstride = 2 * n_kv
    scale = head_dim ** -0.5
    for h in range(n_kv):
        hs = slice(h * head_dim, (h + 1) * head_dim)
        vs = slice(kw + h * head_dim, kw + (h + 1) * head_dim)
        qs = [(_rms(p_ref[:, q_off + (h * group + g) * head_dim:q_off + (h * group + g + 1) * head_dim],
                    gq_ref[...]) * scale).astype(BF16) for g in range(group)]
        qst = jnp.concatenate(qs, axis=0)
        lgs = []
        for c in range(n_ch):
            k = (kv_refs[c][pl.ds(h, page, stride=rstride), :] if c < n_pages
                 else _pad_rows(kvn_ref[:, hs], page))
            lg = lax.dot_general(qst, k.astype(BF16), NT_DIMS, preferred_element_type=F32)
            lgs.append(lg + jnp.concatenate([sc_ref[c]] * group, axis=0))
        m = lgs[0].max(axis=-1, keepdims=True)
        for lg in lgs[1:]:
            m = jnp.maximum(m, lg.max(axis=-1, keepdims=True))
        l = jnp.zeros((group * t, 1), F32)
        acc = jnp.zeros((group * t, head_dim), F32)
        for c in range(n_ch):
            pc = jnp.exp(lgs[c] - m)
            l = l + jnp.sum(pc, axis=-1, keepdims=True)
            v = (kv_refs[c][pl.ds(n_kv + h, page, stride=rstride), :] if c < n_pages
                 else _pad_rows(kvn_ref[:, vs], page))
            acc = acc + jnp.dot(pc.astype(BF16), v.astype(BF16), preferred_element_type=F32)
        out = acc / l
        for g in range(group):
            o_ref[:, (h * group + g) * head_dim:(h * group + g + 1) * head_dim] = (
                out[g * t:(g + 1) * t].astype(o_ref.dtype))


def dsa_sample(p, kv_new, kidx_pool, kv_pool, page_table, g_q, *, t, n_top, n_idx_heads, idx_dim, n_kv, group,
               head_dim, q_off, qi_off, ki_off, wi_off):
    bd, n_pages = page_table.shape
    page = kidx_pool.shape[2]
    qw = n_kv * group * head_dim
    kw = n_kv * head_dim
    n = p.shape[1]
    body = functools.partial(_dsa_sample_body, n_pages=n_pages, page=page, n_top=n_top, n_idx_heads=n_idx_heads,
                             idx_dim=idx_dim, n_kv=n_kv, group=group, head_dim=head_dim, q_off=q_off,
                             qi_off=qi_off, ki_off=ki_off, wi_off=wi_off)
    in_specs = [pl.BlockSpec((t, n), lambda b, pt: (b, 0)),
                pl.BlockSpec((t, 2 * kw), lambda b, pt: (b, 0)),
                pl.BlockSpec((1, head_dim), lambda b, pt: (0, 0))]
    in_specs += [pl.BlockSpec((None, idx_dim, page), functools.partial(lambda b, pt, c: (pt[b, c], 0, 0), c=c))
                 for c in range(n_pages)]
    in_specs += [pl.BlockSpec((None, page * 2 * n_kv, head_dim),
                              functools.partial(lambda b, pt, c: (pt[b, c], 0, 0), c=c))
                 for c in range(n_pages)]
    return pl.pallas_call(
        body,
        grid_spec=pltpu.PrefetchScalarGridSpec(
            num_scalar_prefetch=1,
            grid=(bd,),
            in_specs=in_specs,
            out_specs=pl.BlockSpec((t, qw), lambda b, pt: (b, 0)),
            scratch_shapes=[pltpu.VMEM((n_pages + 1, t, page), F32)]),
        out_shape=jax.ShapeDtypeStruct((bd * t, qw), F32),
        compiler_params=_params("arbitrary"),
        name="dsa_sample",
    )(page_table, p, kv_new, g_q.reshape(1, head_dim), *([kidx_pool] * n_pages), *([kv_pool] * n_pages))


def _dilated_mask(rel, window, dil):
    return (rel >= 0) & (rel <= window) & ((rel & (dil - 1)) == 0)


def _dil_prompt_body(q_ref, k_ref, v_ref, gq_ref, o_ref, lse_ref, m_ref, l_ref, acc_ref, *, window, dil, n_heads,
                     head_dim):
    tq = q_ref.shape[1]
    i, kc, nkc = pl.program_id(2), pl.program_id(3), pl.num_programs(3)
    cidx = i - (nkc - 1) + kc

    @pl.when(kc == 0)
    def _():
        m_ref[...] = jnp.full(m_ref.shape, NEG, F32)
        l_ref[...] = jnp.zeros(l_ref.shape, F32)
        acc_ref[...] = jnp.zeros(acc_ref.shape, F32)

    @pl.when(cidx >= 0)
    def _():
        rel = ((i - cidx) * tq + lax.broadcasted_iota(jnp.int32, (tq, tq), 0)
               - lax.broadcasted_iota(jnp.int32, (tq, tq), 1))
        bias = jnp.where(_dilated_mask(rel, window, dil), 0.0, NEG)
        scale = head_dim ** -0.5
        for h in range(n_heads):
            hs = slice(h * head_dim, (h + 1) * head_dim)
            qn = (_rms(q_ref[0, :, hs], gq_ref[...]) * scale).astype(BF16)
            lg = lax.dot_general(qn, k_ref[0, :, hs].astype(BF16), NT_DIMS, preferred_element_type=F32) + bias
            m_prev = m_ref[h]
            m_new = jnp.maximum(m_prev, jnp.max(lg, axis=-1, keepdims=True))
            alpha = jnp.exp(m_prev - m_new)
            p = jnp.exp(lg - m_new)
            l_ref[h] = alpha * l_ref[h] + jnp.sum(p, axis=-1, keepdims=True)
            acc_ref[h] = alpha * acc_ref[h] + jnp.dot(p.astype(BF16), v_ref[0, :, hs].astype(BF16),
                                                      preferred_element_type=F32)
            m_ref[h] = m_new

    @pl.when(kc == nkc - 1)
    def _():
        for h in range(n_heads):
            hs = slice(h * head_dim, (h + 1) * head_dim)
            o_ref[0, :, hs] = acc_ref[h] / l_ref[h]
            lse_ref[0, :, hs] = jnp.broadcast_to(m_ref[h] + jnp.log(l_ref[h]), (tq, head_dim))


def dilated_prompt_group(p, kn, g_q, *, q_block, k_block, v_block, window, dil, n_heads, head_dim, tq):
    b, s, n = p.shape
    nk = kn.shape[2]
    w = n_heads * head_dim
    sd, wd = s // dil, window // dil
    nkc = -(-wd // tq) + 1

    def kv_map(blk, blocks_per_row):
        return lambda bi, r, i, kc: (bi, jnp.maximum(i - (nkc - 1) + kc, 0), r * blocks_per_row + blk)

    o, lse = pl.pallas_call(
        functools.partial(_dil_prompt_body, window=wd, dil=1, n_heads=n_heads, head_dim=head_dim),
        grid=(b, dil, sd // tq, nkc),
        in_specs=[pl.BlockSpec((1, tq, w), lambda bi, r, i, kc: (bi, i, r * (n // w) + q_block)),
                  pl.BlockSpec((1, tq, w), kv_map(k_block, nk // w)),
                  pl.BlockSpec((1, tq, w), kv_map(v_block, n // w)),
                  pl.BlockSpec((1, head_dim), lambda bi, r, i, kc: (0, 0))],
        out_specs=[pl.BlockSpec((1, tq, w), lambda bi, r, i, kc: (bi, i, r)),
                   pl.BlockSpec((1, tq, w), lambda bi, r, i, kc: (bi, i, r))],
        out_shape=[jax.ShapeDtypeStruct((b, sd, dil * w), F32), jax.ShapeDtypeStruct((b, sd, dil * w), F32)],
        scratch_shapes=[pltpu.VMEM((n_heads, tq, 1), F32), pltpu.VMEM((n_heads, tq, 1), F32),
                        pltpu.VMEM((n_heads, tq, head_dim), F32)],
        compiler_params=_params("parallel", "parallel", "parallel", "arbitrary"),
        name="dilated_prompt",
    )(p.reshape(b, sd, dil * n), kn.reshape(b, sd, dil * nk), p.reshape(b, sd, dil * n), g_q.reshape(1, head_dim))
    return o.reshape(b, s, w), lse.reshape(b, s, w)


def _dil_sample_body(p_ref, kn_ref, buf_ref, gq_ref, o_ref, lse_ref, nbuf_ref, *, window, dil, n_heads, head_dim,
                     q_off, k_off, v_off):
    t = p_ref.shape[0]
    rstride = 2 * n_heads
    wb = buf_ref.shape[1] // rstride
    w = n_heads * head_dim
    k_new = kn_ref[:, k_off:k_off + w]
    v_new = p_ref[:, v_off:v_off + w]
    nbuf_ref[0, 0:(wb - t) * rstride, :] = buf_ref[0, t * rstride:wb * rstride, :]
    for c, new in enumerate((k_new, v_new)):
        for h in range(n_heads):
            nbuf_ref[0, pl.ds((wb - t) * rstride + c * n_heads + h, t, stride=rstride), :] = (
                new[:, h * head_dim:(h + 1) * head_dim])

    rel = wb + lax.broadcasted_iota(jnp.int32, (t, wb), 0) - lax.broadcasted_iota(jnp.int32, (t, wb), 1)
    bias = jnp.where(_dilated_mask(rel, window, dil), 0.0, NEG)
    rel_o = lax.broadcasted_iota(jnp.int32, (t, LANES), 0) - lax.broadcasted_iota(jnp.int32, (t, LANES), 1)
    bias_o = jnp.where(_dilated_mask(rel_o, window, dil), 0.0, NEG)
    scale = head_dim ** -0.5
    for h in range(n_heads):
        hs = slice(h * head_dim, (h + 1) * head_dim)
        k_past = buf_ref[0, pl.ds(h, wb, stride=rstride), :]
        v_past = buf_ref[0, pl.ds(n_heads + h, wb, stride=rstride), :]
        qn = (_rms(p_ref[:, q_off + h * head_dim:q_off + (h + 1) * head_dim], gq_ref[...]) * scale).astype(BF16)
        lg = lax.dot_general(qn, k_past.astype(BF16), NT_DIMS, preferred_element_type=F32) + bias
        lg_o = lax.dot_general(qn, _pad_rows(k_new[:, hs], LANES).astype(BF16), NT_DIMS,
                               preferred_element_type=F32) + bias_o
        m = jnp.maximum(jnp.max(lg, axis=-1, keepdims=True), jnp.max(lg_o, axis=-1, keepdims=True))
        pp = jnp.exp(lg - m)
        pp_o = jnp.exp(lg_o - m)
        l = jnp.sum(pp, axis=-1, keepdims=True) + jnp.sum(pp_o, axis=-1, keepdims=True)
        acc = (jnp.dot(pp.astype(BF16), v_past.astype(BF16), preferred_element_type=F32)
               + jnp.dot(pp_o.astype(BF16), _pad_rows(v_new[:, hs], LANES).astype(BF16),
                         preferred_element_type=F32))
        o_ref[:, hs] = acc / l
        lse_ref[:, hs] = jnp.broadcast_to(m + jnp.log(l), (t, head_dim))


def dilated_sample_group(p, kn, buf, g_q, *, t, window, dil, n_heads, head_dim, q_off, k_off, v_off):
    bd, wb, w2 = buf.shape
    w = n_heads * head_dim
    n, nk = p.shape[1], kn.shape[1]
    return pl.pallas_call(
        functools.partial(_dil_sample_body, window=window, dil=dil, n_heads=n_heads, head_dim=head_dim,
                          q_off=q_off, k_off=k_off, v_off=v_off),
        grid=(bd,),
        in_specs=[pl.BlockSpec((t, n), lambda b: (b, 0)),
                  pl.BlockSpec((t, nk), lambda b: (b, 0)),
                  pl.BlockSpec((1, wb, w2), lambda b: (b, 0, 0)),
                  pl.BlockSpec((1, head_dim), lambda b: (0, 0))],
        out_specs=[pl.BlockSpec((t, w), lambda b: (b, 0)),
                   pl.BlockSpec((t, w), lambda b: (b, 0)),
                   pl.BlockSpec((1, wb, w2), lambda b: (b, 0, 0))],
        out_shape=[jax.ShapeDtypeStruct((bd * t, w), F32), jax.ShapeDtypeStruct((bd * t, w), F32),
                   jax.ShapeDtypeStruct((bd, wb, w2), F32)],
        compiler_params=_params("parallel"),
        name="dilated_sample",
    )(p, kn, buf, g_q.reshape(1, head_dim))


def _merge_groups_body(*refs):
    n = (len(refs) - 1) // 2
    o_refs, lse_refs, out_ref = refs[:n], refs[n:2 * n], refs[2 * n]
    lses = [r[...] for r in lse_refs]
    m = lses[0]
    for x in lses[1:]:
        m = jnp.maximum(m, x)
    ws = [jnp.exp(x - m) for x in lses]
    den = ws[0]
    for x in ws[1:]:
        den = den + x
    acc = ws[0] * o_refs[0][...]
    for wgt, o in zip(ws[1:], o_refs[1:]):
        acc = acc + wgt * o[...]
    out_ref[...] = (acc / den).astype(out_ref.dtype)


def merge_groups(outs, lses):
    m, w = outs[0].shape
    tm = min(m, ROW_TILE)
    spec = pl.BlockSpec((tm, w), lambda i: (i, 0))
    return pl.pallas_call(
        _merge_groups_body,
        grid=(m // tm,),
        in_specs=[spec] * (2 * len(outs)),
        out_specs=spec,
        out_shape=jax.ShapeDtypeStruct((m, w), BF16),
        compiler_params=_params("parallel"),
        name="merge_groups",
    )(*outs, *lses)


def _pool_body(u_ref, halo_ref, wg_ref, sc_ref, o_ref, ext_ref, *, n_pre, windows, halo_is_history):
    nb, t, d = u_ref.shape
    gw = d // len(windows)
    i = pl.program_id(1)
    halo = halo_ref[...]
    if halo_is_history:
        halo = jnp.where(i == 0, 0.0, halo)
    ext_ref[:, 0:POOL_HALO, :] = halo
    ext_ref[:, POOL_HALO:POOL_HALO + t, :] = u_ref[...]
    pos = i * t + lax.broadcasted_iota(jnp.int32, (1, t, 1), 1)
    for g, w in enumerate(windows):
        cs = slice(g * gw, (g + 1) * gw)
        acc = ext_ref[:, POOL_HALO:POOL_HALO + t, cs]
        for j in range(1, w):
            acc = acc + ext_ref[:, POOL_HALO - j:POOL_HALO - j + t, cs]
        div = jnp.minimum(w, n_pre + 1 + pos).astype(F32)
        r = acc / div - u_ref[:, :, cs]
        y = jnp.dot(r.reshape(nb * t, gw).astype(BF16), wg_ref[g], preferred_element_type=F32) * sc_ref[:, cs]
        o_ref[:, :, cs] = y.reshape(nb, t, gw).astype(o_ref.dtype)


def pool_mix(u, halo, w_grp, scale, *, n_pre, windows, batch_block, t_block, halo_is_history, out_dtype):
    b, t, d = u.shape
    hb = t_block // POOL_HALO
    if halo_is_history:
        halo_map = lambda bi, i: (bi, jnp.maximum(i * hb - 1, 0), 0)
    else:
        halo_map = lambda bi, i: (bi, 0, 0)
    return pl.pallas_call(
        functools.partial(_pool_body, n_pre=n_pre, windows=windows, halo_is_history=halo_is_history),
        grid=(b // batch_block, t // t_block),
        in_specs=[pl.BlockSpec((batch_block, t_block, d), lambda bi, i: (bi, i, 0)),
                  pl.BlockSpec((batch_block, POOL_HALO, d), halo_map),
                  pl.BlockSpec(w_grp.shape, lambda bi, i: (0, 0, 0)),
                  pl.BlockSpec((1, d), lambda bi, i: (0, 0))],
        out_specs=pl.BlockSpec((batch_block, t_block, d), lambda bi, i: (bi, i, 0)),
        out_shape=jax.ShapeDtypeStruct((b, t, d), out_dtype),
        scratch_shapes=[pltpu.VMEM((batch_block, POOL_HALO + t_block, d), F32)],
        compiler_params=_params("parallel", "parallel"),
        name="pool_mix",
    )(u, halo, w_grp, scale.reshape(1, d))


def _mlstm_body(q_ref, k_ref, v_ref, op_ref, gi_ref, gf_ref, bi_ref, bf_ref, gh_ref, c0_ref, n0_ref, m0_ref,
                y_ref, c_ref, n_ref, m_ref, cs, ns, ms, *, n_heads, head_dim):
    L = q_ref.shape[1]
    ls = max(L, LANES)
    ci = pl.program_id(1)
    hp = lax.Precision.HIGHEST

    @pl.when(ci == 0)
    def _():
        cs[...] = c0_ref[0]
        ns[...] = n0_ref[0]
        ms[...] = m0_ref[0]

    li = gi_ref[0] + bi_ref[...]
    lf = jax.nn.log_sigmoid(gf_ref[0] + bf_ref[...])
    row = lax.broadcasted_iota(jnp.int32, (L, ls), 0)
    col = lax.broadcasted_iota(jnp.int32, (L, ls), 1)
    causal = col <= row
    bt = jnp.dot(jnp.where(causal, 1.0, 0.0), _pad_rows(lf, ls), preferred_element_type=F32, precision=hp)
    lane = lax.broadcasted_iota(jnp.int32, (L, LANES), 1)
    lane1 = lax.broadcasted_iota(jnp.int32, (1, LANES), 1)
    m_all = ms[...]
    for h in range(n_heads):
        hs = slice(h * head_dim, (h + 1) * head_dim)
        bt_h = bt[:, h:h + 1]
        li_h = li[:, h:h + 1]
        m_prev = m_all[:, h:h + 1]
        a_h = jnp.where(lane == 0, bt_h, jnp.where(lane == 1, 1.0, 0.0))
        b_h = jnp.where(lane == 0, 1.0, jnp.where(lane == 1, li_h - bt_h, 0.0))
        dmat = lax.dot_general(a_h, _pad_rows(b_h, ls), NT_DIMS, preferred_element_type=F32, precision=hp)
        dmat = jnp.where(causal, dmat, -jnp.inf)
        g = bt_h + m_prev
        m_t = jnp.maximum(g, jnp.max(dmat, axis=-1, keepdims=True))
        w_intra = jnp.exp(dmat - m_t)
        w_inter = jnp.exp(g - m_t)
        q = q_ref[0, :, hs]
        k = k_ref[0, :, hs] * head_dim ** -0.5
        v = v_ref[0, :, hs]
        qb, kb, vb = q.astype(BF16), _pad_rows(k, ls).astype(BF16), _pad_rows(v, ls).astype(BF16)
        qk = lax.dot_general(qb, kb, NT_DIMS, preferred_element_type=F32) * w_intra
        c_h = cs[h]
        num = (w_inter * jnp.dot(qb, c_h.astype(BF16), preferred_element_type=F32)
               + jnp.dot(qk.astype(BF16), vb, preferred_element_type=F32))
        n_h = ns[h:h + 1, :]
        den = w_inter * jnp.sum(q * n_h, axis=-1, keepdims=True) + jnp.sum(qk, axis=-1, keepdims=True)
        hout = num / jnp.maximum(jnp.abs(den), jnp.exp(-m_t))
        b_last = bt_h[L - 1:L, :]
        decay = b_last - bt_h + li_h
        m_new = jnp.maximum(b_last + m_prev, jnp.max(decay, axis=0, keepdims=True))
        keep = jnp.exp(b_last + m_prev - m_new)
        kw = k * jnp.exp(decay - m_new)
        cs[h] = keep * c_h + lax.dot_general(_pad_rows(kw, ls).astype(BF16), vb, TN_DIMS,
                                             preferred_element_type=F32)
        ns[h:h + 1, :] = keep * n_h + jnp.sum(kw, axis=0, keepdims=True)
        m_all = jnp.where(lane1 == h, m_new, m_all)
        hn = _rms(hout, gh_ref[:, hs])
        y_ref[0, :, hs] = (jax.nn.sigmoid(op_ref[0, :, hs]) * hn).astype(y_ref.dtype)
    ms[...] = m_all

    @pl.when(ci == pl.num_programs(1) - 1)
    def _():
        c_ref[0] = cs[...]
        n_ref[0] = ns[...]
        m_ref[0] = ms[...]


def mlstm(p, gates, b_i, b_f, g_h, c0, n0, m0, *, n_heads, chunk, out_dtype):
    b, t, w4 = p.shape
    w = w4 // 4
    head_dim = w // n_heads
    nc = t // chunk

    def lane_row(x):
        return jnp.pad(x.reshape(1, -1), ((0, 0), (0, LANES - x.size)))

    m0p = jnp.pad(m0.reshape(b, 1, n_heads), ((0, 0), (0, 0), (0, LANES - n_heads)))
    colspec = lambda blk, wd: pl.BlockSpec((1, chunk, wd), lambda bi, ci: (bi, ci, blk))
    const = lambda shape: pl.BlockSpec(shape, lambda bi, ci: (0,) * len(shape))
    state = lambda shape: pl.BlockSpec((1,) + shape, lambda bi, ci: (bi,) + (0,) * len(shape))
    y, c, n, m = pl.pallas_call(
        functools.partial(_mlstm_body, n_heads=n_heads, head_dim=head_dim),
        grid=(b, nc),
        in_specs=[colspec(0, w), colspec(1, w), colspec(2, w), colspec(3, w),
                  colspec(0, LANES), colspec(1, LANES),
                  const((1, LANES)), const((1, LANES)), const((1, w)),
                  state((n_heads, head_dim, head_dim)), state((n_heads, head_dim)), state((1, LANES))],
        out_specs=[colspec(0, w), state((n_heads, head_dim, head_dim)), state((n_heads, head_dim)),
                   state((1, LANES))],
        out_shape=[jax.ShapeDtypeStruct((b, t, w), out_dtype),
                   jax.ShapeDtypeStruct((b, n_heads, head_dim, head_dim), F32),
                   jax.ShapeDtypeStruct((b, n_heads, head_dim), F32),
                   jax.ShapeDtypeStruct((b, 1, LANES), F32)],
        scratch_shapes=[pltpu.VMEM((n_heads, head_dim, head_dim), F32),
                        pltpu.VMEM((n_heads, head_dim), F32),
                        pltpu.VMEM((1, LANES), F32)],
        compiler_params=_params("parallel", "arbitrary"),
        name="mlstm",
    )(p, p, p, p, gates, gates, lane_row(b_i), lane_row(b_f), g_h.reshape(1, w), c0, n0, m0p)
    return y, c, n, m[:, 0, :n_heads]


def kernel(x_prompt, x_sample, cache_kv_a, cache_kidx_a, page_table, cache_kv_b0, cache_kv_b1, cache_kv_b2,
           state_pool_c, state_c_d, state_n_d, state_m_d, cache_mem_kv, mem_prompt,
           g_norm_mix, g_norm_mem, g_norm_ffn, g_mem_src, w_mem_q, w_mem_kv, g_mem_q, g_mem_k, w_mem_o,
           w_ff1, w_ff2, w_in_a, g_q_a, g_k_a, w_out_a, w_in_b, g_q_b, g_k_b, w_out_b,
           w_in_c, w_grp_c, scale_c, w_out_c, w_in_d, b_i_d, b_f_d, g_h_d, w_out_d):
    bp, s, d = x_prompt.shape
    bd, t, _ = x_sample.shape
    depth = g_norm_mix.shape[0]
    n_mixers = 4
    xp = x_prompt.reshape(bp * s, d)
    xs = x_sample.reshape(bd * t, d)
    cache_b = (cache_kv_b0, cache_kv_b1, cache_kv_b2)
    dilated = ((128, 1), (512, 4), (2048, 16))
    pool_windows = (2, 4, 8, 16)
    n_mem, n_heads_mem = mem_prompt.shape[1], cache_mem_kv.shape[4]
    mem2d = mem_prompt.reshape(bp * n_mem, d)
    hd_mem = cache_mem_kv.shape[5]
    mem_cache_rows = jnp.swapaxes(
        cache_mem_kv.reshape(depth, bd, n_mem, 2, n_heads_mem, hd_mem // LANES, LANES), 4, 5
    ).reshape(depth, bd, n_mem * 2 * hd_mem // LANES * n_heads_mem, LANES)

    kv_a_p, ki_a_p, kv_a_s, ki_a_s = [], [], [], []
    kv_b_p = [[] for _ in dilated]
    kv_b_s = [[] for _ in dilated]
    pool_p, pool_s = [], []
    cd_p, nd_p, md_p, cd_s, nd_s, md_s = [], [], [], [], [], []
    mem_p = []

    for i in range(depth):
        kind, j = i % n_mixers, i // n_mixers
        g_mix = g_norm_mix[i]
        if kind == 0:
            n_kv, hd = cache_kv_a.shape[4], cache_kv_a.shape[5]
            idx_dim = cache_kidx_a.shape[3]
            qw = w_out_a.shape[1]
            group = qw // (n_kv * hd)
            kw = n_kv * hd
            n_idx = (w_in_a.shape[2] - qw - 2 * kw - idx_dim) // (idx_dim + 1)
            iw = n_idx * idx_dim
            w = w_in_a[j]
            o3 = qw + 2 * kw
            ki_off = o3 + iw
            wi_off = ki_off + LANES
            n_cols = -(-(wi_off + LANES) // 512) * 512
            wa = jnp.concatenate([w[:, :o3 + iw], _pad_cols(w[:, o3 + iw:o3 + iw + idx_dim], LANES),
                                  _pad_cols(w[:, o3 + iw + idx_dim:], n_cols - wi_off)], axis=1).astype(BF16)
            wo = w_out_a[j].astype(BF16)
            pp = rms_matmul(xp, g_mix, wa)
            ps = rms_matmul(xs, g_mix, wa)
            kvp, kvp_bf = head_norm(pp, qw // kw, kw, hd, g_k_a[j], v_block=qw // kw + 1, with_bf16=True)
            kvs = head_norm(ps, qw // kw, kw, hd, g_k_a[j], v_block=qw // kw + 1)
            kip = pp[:, ki_off:ki_off + idx_dim]
            kis = ps[:, ki_off:ki_off + idx_dim]
            kv_a_p.append(kvp.reshape(bp, s, 2, n_kv, hd))
            ki_a_p.append(kip.reshape(bp, s, idx_dim))
            kv_a_s.append(kvs.reshape(bd, t, 2, n_kv, hd))
            ki_a_s.append(kis.reshape(bd, t, idx_dim))
            dims = dict(n_idx_heads=n_idx, idx_dim=idx_dim, n_kv=n_kv, group=group, head_dim=hd)
            op = dsa_prompt(pp.reshape(bp, s, n_cols), kip.astype(BF16).reshape(bp, s, idx_dim),
                            kvp_bf.reshape(bp, s, 2 * kw), g_q_a[j], g_k_a[j], q_block=0, qi_block=o3 // iw,
                            wi_block=wi_off // LANES, n_top=min(256, s // 4), **dims)
            n_pool, page = cache_kidx_a.shape[1], cache_kidx_a.shape[2]
            past = page_table.shape[1] * page
            kidx_pool = jnp.swapaxes(cache_kidx_a[j], 1, 2)
            kv_pool = cache_kv_a[j].reshape(n_pool, page * 2 * n_kv, hd)
            os_ = dsa_sample(ps, kvs, kidx_pool, kv_pool, page_table,
                             g_q_a[j], t=t, n_top=min(256, (past + t) // 4), q_off=0, qi_off=o3, ki_off=ki_off,
                             wi_off=wi_off, **dims)
            xp = matmul_res(op.reshape(bp * s, qw), wo, xp)
            xs = matmul_res(os_, wo, xs)
        elif kind == 1:
            n_groups = len(dilated)
            hg, hd = cache_kv_b0.shape[4], cache_kv_b0.shape[5]
            gw = hg * hd
            wb_ = w_in_b[j].astype(BF16)
            wo = w_out_b[j].astype(BF16)
            pp = rms_matmul(xp, g_mix, wb_)
            ps = rms_matmul(xs, g_mix, wb_)
            gk_row = jnp.repeat(g_k_b[j], hg, axis=0).reshape(1, n_groups * gw)
            knp = head_norm_groups(pp, 1, n_groups * gw, gk_row)
            kns = head_norm_groups(ps, 1, n_groups * gw, gk_row)
            pp3 = pp.reshape(bp, s, -1)
            knp3 = knp.reshape(bp, s, -1)
            outs_p, lses_p, outs_s, lses_s = [], [], [], []
            for g, (win, dil) in enumerate(dilated):
                o, lse = dilated_prompt_group(pp3, knp3, g_q_b[j, g], q_block=g, k_block=g,
                                              v_block=2 * n_groups + g, window=win, dil=dil, n_heads=hg,
                                              head_dim=hd, tq=min(256, s // dil))
                outs_p.append(o.reshape(bp * s, gw))
                lses_p.append(lse.reshape(bp * s, gw))
                wb = min(win, s)
                kv_b_p[g].append(jnp.stack(
                    [knp3[:, s - wb:, g * gw:(g + 1) * gw].reshape(bp, wb, hg, hd),
                     pp3[:, s - wb:, (2 * n_groups + g) * gw:(2 * n_groups + g + 1) * gw].reshape(bp, wb, hg, hd)],
                    axis=2))
                buf = cache_b[g][j]
                o, lse, nbuf = dilated_sample_group(ps, kns, buf.reshape(bd, buf.shape[1] * 2 * hg, hd), g_q_b[j, g],
                                                    t=t, window=win, dil=dil, n_heads=hg, head_dim=hd,
                                                    q_off=g * gw, k_off=g * gw, v_off=(2 * n_groups + g) * gw)
                outs_s.append(o)
                lses_s.append(lse)
                kv_b_s[g].append(nbuf.reshape(buf.shape))
            xp = matmul_res(merge_groups(outs_p, lses_p), wo, xp)
            xs = matmul_res(merge_groups(outs_s, lses_s), wo, xs)
        elif kind == 2:
            wi_, wo = w_in_c[j].astype(BF16), w_out_c[j].astype(BF16)
            wg = w_grp_c[j].astype(BF16)
            up = rms_matmul(xp, g_mix, wi_).reshape(bp, s, d)
            us = rms_matmul(xs, g_mix, wi_).reshape(bd, t, d)
            n_state = state_pool_c.shape[2]
            yp = pool_mix(up, up, wg, scale_c[j], n_pre=0, windows=pool_windows, batch_block=1, t_block=512,
                          halo_is_history=True, out_dtype=BF16)
            halo = jnp.pad(state_pool_c[j], ((0, 0), (POOL_HALO - n_state, 0), (0, 0)))
            ys = pool_mix(us, halo, wg, scale_c[j], n_pre=n_state, windows=pool_windows, batch_block=min(bd, 32),
                          t_block=t, halo_is_history=False, out_dtype=F32)
            pool_p.append(up[:, s - n_state:])
            pool_s.append(jnp.concatenate([state_pool_c[j], us], axis=1)[:, -n_state:])
            xp = matmul_res(yp.reshape(bp * s, d), wo, xp)
            xs = matmul_res(ys.reshape(bd * t, d), wo, xs)
        else:
            nh, hd = state_c_d.shape[2], state_c_d.shape[3]
            wd = nh * hd
            w = w_in_d[j]
            w_main = w[:, :4 * wd].astype(BF16)
            w_gate = jnp.concatenate([_pad_cols(w[:, 4 * wd:4 * wd + nh], LANES),
                                      _pad_cols(w[:, 4 * wd + nh:], LANES)], axis=1)
            wo = w_out_d[j].astype(BF16)
            pp = rms_matmul(xp, g_mix, w_main).reshape(bp, s, 4 * wd)
            ps = rms_matmul(xs, g_mix, w_main).reshape(bd, t, 4 * wd)
            gp = rms_matmul(xp, g_mix, w_gate, precise=True).reshape(bp, s, 2 * LANES)
            gs = rms_matmul(xs, g_mix, w_gate, precise=True).reshape(bd, t, 2 * LANES)
            zeros = lambda *shape: jnp.zeros(shape, F32)
            yp, c, n, m = mlstm(pp, gp, b_i_d[j], b_f_d[j], g_h_d[j], zeros(bp, nh, hd, hd), zeros(bp, nh, hd),
                                zeros(bp, nh), n_heads=nh, chunk=256, out_dtype=BF16)
            cd_p.append(c)
            nd_p.append(n)
            md_p.append(m)
            ys, c, n, m = mlstm(ps, gs, b_i_d[j], b_f_d[j], g_h_d[j], state_c_d[j], state_n_d[j], state_m_d[j],
                                n_heads=nh, chunk=t, out_dtype=F32)
            cd_s.append(c)
            nd_s.append(n)
            md_s.append(m)
            xp = matmul_res(yp.reshape(bp * s, wd), wo, xp)
            xs = matmul_res(ys.reshape(bd * t, wd), wo, xs)

        wm = n_heads_mem * cache_mem_kv.shape[5]
        mkv_raw = rms_matmul(mem2d, g_mem_src[i], w_mem_kv[i].astype(BF16))
        mkv = head_norm(mkv_raw, 0, wm, wm // n_heads_mem, g_mem_k[i], v_block=1)
        mem_p.append(mkv.reshape(bp, n_mem, 2, n_heads_mem, wm // n_heads_mem))
        wq, wo = w_mem_q[i].astype(BF16), w_mem_o[i].astype(BF16)
        qp = rms_matmul(xp, g_norm_mem[i], wq).reshape(bp, s, wm)
        qs = rms_matmul(xs, g_norm_mem[i], wq).reshape(bd, t, wm)
        ap = mem_attn(qp, mkv.reshape(bp, n_mem, 2 * wm), g_mem_q[i], n_heads=n_heads_mem, n_mem=n_mem,
                      batch_block=1, q_block=512, out_dtype=BF16)
        as_ = mem_attn(qs, mem_cache_rows, g_mem_q[i], n_heads=n_heads_mem, n_mem=n_mem, batch_block=4,
                       q_block=t, out_dtype=F32, layer=i)
        xp = matmul_res(ap.reshape(bp * s, wm), wo, xp)
        xs = matmul_res(as_.reshape(bd * t, wm), wo, xs)

        w1, w2 = w_ff1[i].astype(BF16), w_ff2[i].astype(BF16)
        xp = ffn(xp, g_norm_ffn[i], w1, w2)
        xs = ffn(xs, g_norm_ffn[i], w1, w2)

    return (xp.reshape(bp, s, d), xs.reshape(bd, t, d),
            jnp.stack(kv_a_p), jnp.stack(ki_a_p), jnp.stack(kv_a_s), jnp.stack(ki_a_s),
            jnp.stack(kv_b_p[0]), jnp.stack(kv_b_p[1]), jnp.stack(kv_b_p[2]),
            jnp.stack(kv_b_s[0]), jnp.stack(kv_b_s[1]), jnp.stack(kv_b_s[2]),
            jnp.stack(pool_p), jnp.stack(pool_s),
            jnp.stack(cd_p), jnp.stack(nd_p), jnp.stack(md_p),
            jnp.stack(cd_s), jnp.stack(nd_s), jnp.stack(md_s),
            jnp.stack(mem_p))
```

```python
import functools

import jax
import jax.numpy as jnp
from jax import lax
from jax.experimental import pallas as pl
from jax.experimental.pallas import tpu as pltpu

F32 = jnp.float32
BF16 = jnp.bfloat16
EPS = 1e-6
NEG = -1e30
LANES = 128
SUBLANES = 8
VMEM_LIMIT_BYTES = 56 * 2**20
ROW_TILE = 1024
POOL_HALO = 16
MAX_BISECTIONS = 512
SAFE_SHIFT = 40.0
LOG2E = 1.4426950408889634
BF16_ROWS = 16
COARSE_BISECTIONS = 8
FOLD_CHAINS = 4

NT_DIMS = (((1,), (1,)), ((), ()))
TN_DIMS = (((0,), (0,)), ((), ()))


def _params(*sem):
    return pltpu.CompilerParams(dimension_semantics=sem, vmem_limit_bytes=VMEM_LIMIT_BYTES)


def _rms(x, g):
    return x * lax.rsqrt(jnp.mean(x * x, axis=-1, keepdims=True) + EPS) * g


def _col_tile(n):
    for t in (1536, 1024, 768, 512, 384, 256, 128):
        if n % t == 0:
            return t
    raise ValueError(f"matmul width {n} is not a multiple of {LANES}")


def _pad_rows(x, rows):
    if x.shape[0] == rows:
        return x
    return jnp.concatenate([x, jnp.zeros((rows - x.shape[0],) + x.shape[1:], x.dtype)], axis=0)


def _pad_cols(w, n):
    return jnp.pad(w, ((0, 0), (0, n - w.shape[1])))


def _rms_matmul_body(x_ref, g_ref, w_ref, o_ref, h_ref, *, precision):
    @pl.when(pl.program_id(1) == 0)
    def _():
        h_ref[...] = _rms(x_ref[...], g_ref[...]).astype(h_ref.dtype)

    o_ref[...] = jnp.dot(h_ref[...], w_ref[...], preferred_element_type=F32, precision=precision)


def rms_matmul(x, g, w, *, precise=False):
    m, d = x.shape
    n = w.shape[1]
    tm, tn = min(m, ROW_TILE), _col_tile(n)
    return pl.pallas_call(
        functools.partial(_rms_matmul_body, precision=lax.Precision.HIGHEST if precise else None),
        grid=(m // tm, n // tn),
        in_specs=[pl.BlockSpec((tm, d), lambda i, j: (i, 0)),
                  pl.BlockSpec((1, d), lambda i, j: (0, 0)),
                  pl.BlockSpec((d, tn), lambda i, j: (0, j))],
        out_specs=pl.BlockSpec((tm, tn), lambda i, j: (i, j)),
        out_shape=jax.ShapeDtypeStruct((m, n), F32),
        scratch_shapes=[pltpu.VMEM((tm, d), F32 if precise else BF16)],
        compiler_params=_params("parallel", "arbitrary"),
        name="rms_matmul",
    )(x, g.reshape(1, d), w)


def _matmul_res_body(a_ref, w_ref, r_ref, o_ref):
    o_ref[...] = r_ref[...] + jnp.dot(a_ref[...].astype(BF16), w_ref[...], preferred_element_type=F32)


def matmul_res(a, w, r):
    m, k = a.shape
    n = w.shape[1]
    tm, tn = min(m, ROW_TILE), _col_tile(n)
    return pl.pallas_call(
        _matmul_res_body,
        grid=(m // tm, n // tn),
        in_specs=[pl.BlockSpec((tm, k), lambda i, j: (i, 0)),
                  pl.BlockSpec((k, tn), lambda i, j: (0, j)),
                  pl.BlockSpec((tm, tn), lambda i, j: (i, j))],
        out_specs=pl.BlockSpec((tm, tn), lambda i, j: (i, j)),
        out_shape=jax.ShapeDtypeStruct((m, n), F32),
        compiler_params=_params("parallel", "parallel"),
        name="matmul_res",
    )(a, w, r)


def _ffn_body(x_ref, g_ref, w1_ref, w2_ref, o_ref, h_ref, acc_ref):
    f = pl.program_id(1)

    @pl.when(f == 0)
    def _():
        h_ref[...] = _rms(x_ref[...], g_ref[...]).astype(BF16)
        acc_ref[...] = jnp.zeros_like(acc_ref)

    a = jnp.dot(h_ref[...], w1_ref[...], preferred_element_type=F32)
    a = jnp.square(jnp.maximum(a, 0.0)).astype(BF16)
    acc_ref[...] += jnp.dot(a, w2_ref[...], preferred_element_type=F32)

    @pl.when(f == pl.num_programs(1) - 1)
    def _():
        o_ref[...] = x_ref[...] + acc_ref[...]


def ffn(x, g, w1, w2):
    m, d = x.shape
    ff = w1.shape[1]
    tm, tf = min(m, ROW_TILE), 1024
    return pl.pallas_call(
        _ffn_body,
        grid=(m // tm, ff // tf),
        in_specs=[pl.BlockSpec((tm, d), lambda i, f: (i, 0)),
                  pl.BlockSpec((1, d), lambda i, f: (0, 0)),
                  pl.BlockSpec((d, tf), lambda i, f: (0, f)),
                  pl.BlockSpec((tf, d), lambda i, f: (f, 0))],
        out_specs=pl.BlockSpec((tm, d), lambda i, f: (i, 0)),
        out_shape=jax.ShapeDtypeStruct((m, d), F32),
        scratch_shapes=[pltpu.VMEM((tm, d), BF16), pltpu.VMEM((tm, d), F32)],
        compiler_params=_params("parallel", "arbitrary"),
        name="ffn",
    )(x, g.reshape(1, d), w1, w2)


def _head_norm_body(*refs, head_dim, n_heads, with_copy, with_bf16):
    k_ref, g_ref = refs[0], refs[1]
    v_ref = refs[2] if with_copy else None
    outs = refs[3 if with_copy else 2:]
    width = n_heads * head_dim
    for h in range(n_heads):
        sl = slice(h * head_dim, (h + 1) * head_dim)
        kn = _rms(k_ref[:, sl], g_ref[:, sl])
        for o in outs:
            o[:, sl] = kn.astype(o.dtype)
    if with_copy:
        v = v_ref[...]
        for o in outs:
            o[:, width:] = v.astype(o.dtype)


def head_norm(p, k_block, width, head_dim, gain, *, v_block=None, with_bf16=False):
    m = p.shape[0]
    n_heads = width // head_dim
    tm = min(m, ROW_TILE)
    with_copy = v_block is not None
    ow = 2 * width if with_copy else width
    in_specs = [pl.BlockSpec((tm, width), lambda i: (i, k_block)),
                pl.BlockSpec((1, width), lambda i: (0, 0))]
    args = [p, jnp.tile(gain.reshape(1, head_dim), (1, n_heads))]
    if with_copy:
        in_specs.append(pl.BlockSpec((tm, width), lambda i: (i, v_block)))
        args.append(p)
    out_shape = [jax.ShapeDtypeStruct((m, ow), F32)]
    if with_bf16:
        out_shape.append(jax.ShapeDtypeStruct((m, ow), BF16))
    out = pl.pallas_call(
        functools.partial(_head_norm_body, head_dim=head_dim, n_heads=n_heads, with_copy=with_copy,
                          with_bf16=with_bf16),
        grid=(m // tm,),
        in_specs=in_specs,
        out_specs=[pl.BlockSpec((tm, ow), lambda i: (i, 0)) for _ in out_shape],
        out_shape=out_shape,
        compiler_params=_params("parallel"),
        name="head_norm",
    )(*args)
    return out if with_bf16 else out[0]


def head_norm_groups(p, k_block, width, gain_row):
    m = p.shape[0]
    tm = min(m, ROW_TILE)
    n_heads = width // LANES
    return pl.pallas_call(
        functools.partial(_head_norm_body, head_dim=LANES, n_heads=n_heads, with_copy=False, with_bf16=False),
        grid=(m // tm,),
        in_specs=[pl.BlockSpec((tm, width), lambda i: (i, k_block)),
                  pl.BlockSpec((1, width), lambda i: (0, 0))],
        out_specs=[pl.BlockSpec((tm, width), lambda i: (i, 0))],
        out_shape=[jax.ShapeDtypeStruct((m, width), F32)],
        compiler_params=_params("parallel"),
        name="head_norm_groups",
    )(p, gain_row)[0]


def _mem_attn_body(q_ref, kv_ref, gq_ref, o_ref, *, n_heads, head_dim, n_mem, tiled_rows):
    width = n_heads * head_dim
    scale = head_dim ** -0.5
    n_lt = head_dim // LANES
    rstride = 2 * n_lt * n_heads

    def tiled(b, c, h):
        parts = [kv_ref[b, pl.ds((c * n_lt + lt) * n_heads + h, n_mem, stride=rstride), :] for lt in range(n_lt)]
        return jnp.concatenate(parts, axis=1)

    for b in range(q_ref.shape[0]):
        for h in range(n_heads):
            sl = slice(h * head_dim, (h + 1) * head_dim)
            qn = (_rms(q_ref[b, :, sl], gq_ref[...]) * scale).astype(BF16)
            if tiled_rows:
                k, v = tiled(b, 0, h).astype(BF16), tiled(b, 1, h).astype(BF16)
            else:
                k = kv_ref[b, :, sl].astype(BF16)
                v = kv_ref[b, :, width + h * head_dim:width + (h + 1) * head_dim].astype(BF16)
            lg = lax.dot_general(qn, k, NT_DIMS, preferred_element_type=F32)
            p = jnp.exp(lg - jnp.max(lg, axis=-1, keepdims=True))
            l = jnp.sum(p, axis=-1, keepdims=True)
            o = jnp.dot(p.astype(BF16), v, preferred_element_type=F32) / l
            o_ref[b, :, sl] = o.astype(o_ref.dtype)


def mem_attn(q, kv, g_q, *, n_heads, n_mem, batch_block, q_block, out_dtype, layer=None):
    b, t, w = q.shape
    head_dim = w // n_heads
    if layer is None:
        kv_spec = pl.BlockSpec((batch_block, n_mem, 2 * w), lambda i, j: (i, 0, 0))
    else:
        kv_spec = pl.BlockSpec((None, batch_block) + kv.shape[2:], lambda i, j: (layer, i, 0, 0))
    return pl.pallas_call(
        functools.partial(_mem_attn_body, n_heads=n_heads, head_dim=head_dim, n_mem=n_mem,
                          tiled_rows=layer is not None),
        grid=(b // batch_block, t // q_block),
        in_specs=[pl.BlockSpec((batch_block, q_block, w), lambda i, j: (i, j, 0)),
                  kv_spec,
                  pl.BlockSpec((1, head_dim), lambda i, j: (0, 0))],
        out_specs=pl.BlockSpec((batch_block, q_block, w), lambda i, j: (i, j, 0)),
        out_shape=jax.ShapeDtypeStruct((b, t, w), out_dtype),
        compiler_params=_params("parallel", "parallel"),
        name="mem_attn",
    )(q, kv, g_q.reshape(1, head_dim))


def _fold(x, op, ka):
    if ka == 1:
        parts = [x[:, j * LANES:(j + 1) * LANES] for j in range(x.shape[1] // LANES)]
    else:
        parts = [x[j * SUBLANES:(j + 1) * SUBLANES, :] for j in range(x.shape[0] // SUBLANES)]
    accs = parts[:FOLD_CHAINS]
    for i, part in enumerate(parts[FOLD_CHAINS:]):
        accs[i % FOLD_CHAINS] = op(accs[i % FOLD_CHAINS], part)
    while len(accs) > 1:
        accs = [op(accs[i], accs[i + 1]) if i + 1 < len(accs) else accs[i] for i in range(0, len(accs), 2)]
    return accs[0]


def _fold_lanes(x):
    return _fold(x, jnp.add, 1)


def _folded_shape(chunk_shape, ka):
    return (chunk_shape[0], LANES) if ka == 1 else (SUBLANES, chunk_shape[1])


def _chunk_loop(n_ch, body, init):
    if isinstance(n_ch, int):
        carry = init
        for c in range(n_ch):
            carry = body(c, carry)
        return carry
    return lax.fori_loop(0, n_ch, body, init)


def _count(sc_ref, n_ch, pred, ka):
    def body(c, acc):
        return acc + _fold(jnp.where(pred(sc_ref[c], c), 1.0, 0.0), jnp.add, ka)

    acc = _chunk_loop(n_ch, body, jnp.zeros(_folded_shape(sc_ref.shape[1:], ka), F32))
    return jnp.sum(acc, axis=ka, keepdims=True)


def _minmax_update(carry, s, ka):
    mn, mx = carry
    mn = jnp.minimum(mn, _fold(jnp.where(s > -jnp.inf, s, jnp.inf), jnp.minimum, ka))
    return mn, jnp.maximum(mx, _fold(s, jnp.maximum, ka))


def _minmax_init(chunk_shape, ka):
    shape = _folded_shape(chunk_shape, ka)
    return jnp.full(shape, jnp.inf, F32), jnp.full(shape, -jnp.inf, F32)


def _key_index(c, chunk_shape, ka):
    shape = (1, chunk_shape[1]) if ka == 1 else (chunk_shape[0], 1)
    return (c * chunk_shape[ka] + lax.broadcasted_iota(jnp.int32, shape, ka)).astype(F32)


def _count_ge_bf16(sb_ref, n_ch, theta_b):
    ch, nq = sb_ref.shape[1:]
    th = jnp.broadcast_to(theta_b, (BF16_ROWS, nq))
    one, zero = jnp.ones((BF16_ROWS, nq), BF16), jnp.zeros((BF16_ROWS, nq), BF16)

    def body(c, acc):
        x = sb_ref[c]
        parts = [jnp.where(x[j * BF16_ROWS:(j + 1) * BF16_ROWS, :] >= th, one, zero)
                 for j in range(ch // BF16_ROWS)]
        accs = parts[:FOLD_CHAINS]
        for j, part in enumerate(parts[FOLD_CHAINS:]):
            accs[j % FOLD_CHAINS] = accs[j % FOLD_CHAINS] + part
        total = accs[0].astype(F32)
        for a in accs[1:]:
            total = total + a.astype(F32)
        return acc + total

    acc = lax.fori_loop(0, n_ch, body, jnp.zeros((BF16_ROWS, nq), F32))
    return jnp.sum(acc, axis=0, keepdims=True)


def _select_threshold(sc_ref, kk, n_ch, minmax, n_finite, ka, sb_ref=None):
    chunk = sc_ref.shape[1:]
    ch = chunk[ka]
    lo = jnp.min(minmax[0], axis=ka, keepdims=True)
    mx = jnp.max(minmax[1], axis=ka, keepdims=True)
    hi = mx + jnp.maximum(jnp.abs(mx) * 2.0**-22, 1e-36)
    c_lo = n_finite
    done = jnp.where(c_lo == kk, 1.0, 0.0)

    if sb_ref is not None:
        def coarse(_, st):
            lo, hi = st
            t_b = (lo * 0.5 + hi * 0.5).astype(BF16)
            t = t_b.astype(F32)
            below = t - jnp.maximum(jnp.abs(t) * 2.0**-6, 1e-30)
            c_b = _count_ge_bf16(sb_ref, n_ch, t_b)
            live = done < 0.5
            up = live & (c_b >= kk) & (below > lo) & (below < hi)
            dn = live & (c_b < kk) & (t < hi) & (t > lo)
            return jnp.where(up, below, lo), jnp.where(dn, t, hi)

        lo, hi = lax.fori_loop(0, COARSE_BISECTIONS, coarse, (lo, hi))
        c_lo = jnp.where(done < 0.5, _count(sc_ref, n_ch, lambda s, c: s >= lo, ka), c_lo)
        done = jnp.where(c_lo == kk, 1.0, done)

    def cond(st):
        return (jnp.min(st[3]) < 0.5) & (st[4] < MAX_BISECTIONS)

    def body(st):
        lo, hi, c_lo, done, it = st
        mid = lo * 0.5 + hi * 0.5
        stuck = (mid <= lo) | (mid >= hi)
        c_mid = _count(sc_ref, n_ch, lambda s, c: s >= mid, ka)
        live = (done < 0.5) & jnp.logical_not(stuck)
        up = live & (c_mid >= kk)
        dn = live & (c_mid < kk)
        lo = jnp.where(up, mid, lo)
        c_lo = jnp.where(up, c_mid, c_lo)
        hi = jnp.where(dn, mid, hi)
        done = jnp.where(stuck | (c_lo == kk), 1.0, done)
        return lo, hi, c_lo, done, it + 1

    theta, _, c_theta, _, _ = lax.while_loop(cond, body, (lo, hi, c_lo, done, jnp.int32(0)))

    n_keys = n_ch * ch
    tie = c_theta > kk
    no_tie_jmax = jnp.full(kk.shape, 2.0**30, F32)

    def resolve(_):
        need = kk - _count(sc_ref, n_ch, lambda s, c: s > theta, ka)
        n_iter = max(1, (sc_ref.shape[0] * ch - 1).bit_length())

        def step(_, st):
            jl, jh = st
            jm = jnp.floor((jl + jh) * 0.5)
            c_m = _count(sc_ref, n_ch, lambda s, c: (s == theta) & (_key_index(c, chunk, ka) <= jm), ka)
            ok = c_m >= need
            return jnp.where(ok, jl, jm + 1.0), jnp.where(ok, jm, jh)

        jl0 = jnp.zeros(kk.shape, F32)
        jh0 = jnp.zeros(kk.shape, F32) + (n_keys - 1)
        _, jh = lax.fori_loop(0, n_iter, step, (jl0, jh0))
        return jnp.where(tie, jh, no_tie_jmax)

    any_tie = jnp.max(jnp.where(tie, 1.0, 0.0)) > 0.5
    jmax = lax.cond(any_tie, resolve, lambda _: no_tie_jmax, 0)
    return theta, jmax


def _scores_to_bias(sc_ref, n_ch, theta, jmax, ka):
    chunk = sc_ref.shape[1:]

    def body(c, carry):
        s = sc_ref[c]
        sel = (s > theta) | ((s == theta) & (_key_index(c, chunk, ka) <= jmax))
        sc_ref[c] = jnp.where(sel, 0.0, NEG)
        return carry

    _chunk_loop(n_ch, body, 0)


def _dsa_prompt_body(q_ref, qi_ref, wi_ref, ki_ref, kn_ref, vt_ref, gq_ref, gk_ref, o_ref,
                     sc_ref, sb_ref, *, tq, n_top, n_idx_heads, idx_dim, n_kv, group, head_dim):
    ka = 0
    ch = sc_ref.shape[1]
    t0 = pl.program_id(1) * tq
    n_ch = (t0 + tq + ch - 1) // ch
    qpos = t0 + lax.broadcasted_iota(jnp.int32, (1, tq), 1)

    qi = qi_ref[0]
    wi_t = wi_ref[0].T
    qi_h = [qi[:, h * idx_dim:(h + 1) * idx_dim].astype(BF16) for h in range(n_idx_heads)]
    w_row = [wi_t[h:h + 1, :] for h in range(n_idx_heads)]

    def score_chunk(c, carry):
        ki = ki_ref[0, pl.ds(pl.multiple_of(c * ch, ch), ch), :]
        acc = jnp.zeros((ch, tq), F32)
        for h in range(n_idx_heads):
            s = lax.dot_general(ki, qi_h[h], NT_DIMS, preferred_element_type=F32)
            acc = acc + jnp.maximum(s, 0.0) * w_row[h]
        kpos = c * ch + lax.broadcasted_iota(jnp.int32, (ch, 1), 0)
        acc = jnp.where(kpos <= qpos, acc, -jnp.inf)
        sc_ref[c] = acc
        sb_ref[c] = acc.astype(BF16)
        return _minmax_update(carry, acc, ka)

    minmax = lax.fori_loop(0, n_ch, score_chunk, _minmax_init((ch, tq), ka))

    n_causal = (qpos + 1).astype(F32)
    kk = jnp.minimum(n_causal, float(n_top))
    theta, jmax = _select_threshold(sc_ref, kk, n_ch, minmax, n_causal, ka, sb_ref)
    _scores_to_bias(sc_ref, n_ch, theta, jmax, ka)

    scale = head_dim ** -0.5 * LOG2E
    q = q_ref[0]
    qf = []
    for h in range(n_kv):
        qs = [_rms(q[:, (h * group + g) * head_dim:(h * group + g + 1) * head_dim], gq_ref[...]) * scale
              for g in range(group)]
        qf.append(jnp.concatenate(qs, axis=0))
    qst = [x.astype(BF16) for x in qf]
    gt = group * tq

    def logits(c, h):
        off = pl.multiple_of(c * ch, ch)
        k = kn_ref[0, pl.ds(off, ch), h * head_dim:(h + 1) * head_dim]
        return (lax.dot_general(k, qst[h], NT_DIMS, preferred_element_type=F32)
                + jnp.concatenate([sc_ref[c]] * group, axis=1))

    def softmax_pv(m_rows):
        def sum_chunk(c, carry):
            lf, acc = carry
            lf_out, acc_out = [], []
            for h in range(n_kv):
                vt = vt_ref[0, c, h * head_dim:(h + 1) * head_dim, :]
                p = jnp.exp2(logits(c, h) - m_rows[h])
                lf_out.append(lf[h] + _fold(p, jnp.add, ka))
                acc_out.append(acc[h] + jnp.dot(vt, p.astype(BF16), preferred_element_type=F32))
            return tuple(lf_out), tuple(acc_out)

        lf, acc = lax.fori_loop(0, n_ch, sum_chunk, (tuple(jnp.zeros((SUBLANES, gt), F32) for _ in range(n_kv)),
                                                      tuple(jnp.zeros((head_dim, gt), F32) for _ in range(n_kv))))
        for h in range(n_kv):
            out = (acc[h] / jnp.sum(lf[h], axis=0, keepdims=True)).T
            for g in range(group):
                o_ref[0, :, (h * group + g) * head_dim:(h * group + g + 1) * head_dim] = (
                    out[g * tq:(g + 1) * tq].astype(o_ref.dtype))

    k_norm = head_dim ** 0.5 * jnp.max(jnp.abs(gk_ref[...]), axis=-1, keepdims=True)
    ones = jnp.ones((SUBLANES, head_dim), F32)
    bound = [jnp.sqrt(lax.dot_general(ones, x * x, NT_DIMS, preferred_element_type=F32)[0:1, :]) * k_norm * 1.01
             for x in qf]
    worst = bound[0].max()
    for x in bound[1:]:
        worst = jnp.maximum(worst, x.max())
    bound_is_safe = worst <= SAFE_SHIFT * LOG2E

    @pl.when(bound_is_safe)
    def _():
        softmax_pv(bound)

    @pl.when(jnp.logical_not(bound_is_safe))
    def _():
        def max_chunk(c, mf):
            return tuple(jnp.maximum(mf[h], _fold(logits(c, h), jnp.maximum, ka)) for h in range(n_kv))

        mf = lax.fori_loop(0, n_ch, max_chunk, tuple(jnp.full((SUBLANES, gt), NEG, F32) for _ in range(n_kv)))
        softmax_pv([jnp.max(x, axis=0, keepdims=True) for x in mf])


def dsa_prompt(p, ki, kv, g_q, g_k, *, q_block, qi_block, wi_block, n_top, n_idx_heads, idx_dim, n_kv, group,
               head_dim, tq=256, ch=512):
    b, s, _ = p.shape
    qw = n_kv * group * head_dim
    kw = n_kv * head_dim
    iw = n_idx_heads * idx_dim
    vt = jnp.swapaxes(kv[:, :, kw:].reshape(b, s // ch, ch, kw), 2, 3)
    body = functools.partial(_dsa_prompt_body, tq=tq, n_top=n_top, n_idx_heads=n_idx_heads, idx_dim=idx_dim,
                             n_kv=n_kv, group=group, head_dim=head_dim)
    return pl.pallas_call(
        body,
        grid=(b, s // tq),
        in_specs=[pl.BlockSpec((1, tq, qw), lambda i, j: (i, j, q_block)),
                  pl.BlockSpec((1, tq, iw), lambda i, j: (i, j, qi_block)),
                  pl.BlockSpec((1, tq, LANES), lambda i, j: (i, j, wi_block)),
                  pl.BlockSpec((1, s, idx_dim), lambda i, j: (i, 0, 0), pipeline_mode=pl.Buffered(1)),
                  pl.BlockSpec((1, s, kw), lambda i, j: (i, 0, 0), pipeline_mode=pl.Buffered(1)),
                  pl.BlockSpec((1, s // ch, kw, ch), lambda i, j: (i, 0, 0, 0), pipeline_mode=pl.Buffered(1)),
                  pl.BlockSpec((1, head_dim), lambda i, j: (0, 0)),
                  pl.BlockSpec((1, head_dim), lambda i, j: (0, 0))],
        out_specs=pl.BlockSpec((1, tq, qw), lambda i, j: (i, j, 0)),
        out_shape=jax.ShapeDtypeStruct((b, s, qw), BF16),
        scratch_shapes=[pltpu.VMEM((s // ch, ch, tq), F32), pltpu.VMEM((s // ch, ch, tq), BF16)],
        compiler_params=_params("parallel", "arbitrary"),
        name="dsa_prompt",
    )(p, p, p, ki, kv, vt, g_q.reshape(1, head_dim), g_k.reshape(1, head_dim))


def _dsa_sample_body(pt_ref, p_ref, kvn_ref, gq_ref, *refs, n_pages, page, n_top, n_idx_heads, idx_dim, n_kv,
                     group, head_dim, q_off, qi_off, ki_off, wi_off):
    del pt_ref
    kidx_refs = refs[:n_pages]
    kv_refs = refs[n_pages:2 * n_pages]
    o_ref, sc_ref = refs[2 * n_pages], refs[2 * n_pages + 1]
    t = p_ref.shape[0]
    kw = n_kv * head_dim

    qi = p_ref[:, qi_off:qi_off + n_idx_heads * idx_dim]
    ki_new = p_ref[:, ki_off:ki_off + idx_dim]
    wi = p_ref[:, wi_off:wi_off + LANES]
    qi_st = jnp.concatenate([qi[:, h * idx_dim:(h + 1) * idx_dim] for h in range(n_idx_heads)], axis=0).astype(BF16)
    w_col = jnp.concatenate([wi[:, h:h + 1] for h in range(n_idx_heads)], axis=0)

    def idx_scores(s):
        s = jnp.maximum(s, 0.0) * w_col
        out = s[0:t]
        for h in range(1, n_idx_heads):
            out = out + s[h * t:(h + 1) * t]
        return out

    ka = 1
    minmax = _minmax_init((t, page), ka)
    for c in range(n_pages):
        s = idx_scores(jnp.dot(qi_st, kidx_refs[c][...].astype(BF16), preferred_element_type=F32))
        sc_ref[c] = s
        minmax = _minmax_update(minmax, s, ka)
    own = idx_scores(lax.dot_general(qi_st, _pad_rows(ki_new, page).astype(BF16), NT_DIMS,
                                     preferred_element_type=F32))
    tok = lax.broadcasted_iota(jnp.int32, (t, page), 0)
    col = lax.broadcasted_iota(jnp.int32, (t, page), 1)
    own = jnp.where(col <= tok, own, -jnp.inf)
    sc_ref[n_pages] = own
    minmax = _minmax_update(minmax, own, ka)

    n_ch = n_pages + 1
    n_finite = (n_pages * page + 1 + lax.broadcasted_iota(jnp.int32, (t, 1), 0)).astype(F32)
    theta, jmax = _select_threshold(sc_ref, jnp.full((t, 1), float(n_top), F32), n_ch, minmax, n_finite, ka)
    _scores_to_bias(sc_ref, n_ch, theta, jmax, ka)

    rstride = 2 * n_kv
    scale = head_dim ** -0.5
    for h in range(n_kv):
        hs = slice(h * head_dim, (h + 1) * head_dim)
        vs = slice(kw + h * head_dim, kw + (h + 1) * head_dim)
        qs = [(_rms(p_ref[:, q_off + (h * group + g) * head_dim:q_off + (h * group + g + 1) * head_dim],
                    gq_ref[...]) * scale).astype(BF16) for g in range(group)]
        qst = jnp.concatenate(qs, axis=0)
        lgs = []
        for c in range(n_ch):
            k = (kv_refs[c][pl.ds(h, page, stride=rstride), :] if c < n_pages
                 else _pad_rows(kvn_ref[:, hs], page))
            lg = lax.dot_general(qst, k.astype(BF16), NT_DIMS, preferred_element_type=F32)
            lgs.append(lg + jnp.concatenate([sc_ref[c]] * group, axis=0))
        m = lgs[0].max(axis=-1, keepdims=True)
        for lg in lgs[1:]:
            m = jnp.maximum(m, lg.max(axis=-1, keepdims=True))
        l = jnp.zeros((group * t, 1), F32)
        acc = jnp.zeros((group * t, head_dim), F32)
        for c in range(n_ch):
            pc = jnp.exp(lgs[c] - m)
            l = l + jnp.sum(pc, axis=-1, keepdims=True)
            v = (kv_refs[c][pl.ds(n_kv + h, page, stride=rstride), :] if c < n_pages
                 else _pad_rows(kvn_ref[:, vs], page))
            acc = acc + jnp.dot(pc.astype(BF16), v.astype(BF16), preferred_element_type=F32)
        out = acc / l
        for g in range(group):
            o_ref[:, (h * group + g) * head_dim:(h * group + g + 1) * head_dim] = (
                out[g * t:(g + 1) * t].astype(o_ref.dtype))


def dsa_sample(p, kv_new, kidx_pool, kv_pool, page_table, g_q, *, t, n_top, n_idx_heads, idx_dim, n_kv, group,
               head_dim, q_off, qi_off, ki_off, wi_off):
    bd, n_pages = page_table.shape
    page = kidx_pool.shape[2]
    qw = n_kv * group * head_dim
    kw = n_kv * head_dim
    n = p.shape[1]
    body = functools.partial(_dsa_sample_body, n_pages=n_pages, page=page, n_top=n_top, n_idx_heads=n_idx_heads,
                             idx_dim=idx_dim, n_kv=n_kv, group=group, head_dim=head_dim, q_off=q_off,
                             qi_off=qi_off, ki_off=ki_off, wi_off=wi_off)
    in_specs = [pl.BlockSpec((t, n), lambda b, pt: (b, 0)),
                pl.BlockSpec((t, 2 * kw), lambda b, pt: (b, 0)),
                pl.BlockSpec((1, head_dim), lambda b, pt: (0, 0))]
    in_specs += [pl.BlockSpec((None, idx_dim, page), functools.partial(lambda b, pt, c: (pt[b, c], 0, 0), c=c))
                 for c in range(n_pages)]
    in_specs += [pl.BlockSpec((None, page * 2 * n_kv, head_dim),
                              functools.partial(lambda b, pt, c: (pt[b, c], 0, 0), c=c))
                 for c in range(n_pages)]
    return pl.pallas_call(
        body,
        grid_spec=pltpu.PrefetchScalarGridSpec(
            num_scalar_prefetch=1,
            grid=(bd,),
            in_specs=in_specs,
            out_specs=pl.BlockSpec((t, qw), lambda b, pt: (b, 0)),
            scratch_shapes=[pltpu.VMEM((n_pages + 1, t, page), F32)]),
        out_shape=jax.ShapeDtypeStruct((bd * t, qw), F32),
        compiler_params=_params("arbitrary"),
        name="dsa_sample",
    )(page_table, p, kv_new, g_q.reshape(1, head_dim), *([kidx_pool] * n_pages), *([kv_pool] * n_pages))


def _dilated_mask(rel, window, dil):
    return (rel >= 0) & (rel <= window) & ((rel & (dil - 1)) == 0)


def _dil_prompt_body(q_ref, k_ref, v_ref, gq_ref, o_ref, lse_ref, m_ref, l_ref, acc_ref, *, window, dil, n_heads,
                     head_dim):
    tq = q_ref.shape[1]
    i, kc, nkc = pl.program_id(1), pl.program_id(2), pl.num_programs(2)
    cidx = i - (nkc - 1) + kc

    @pl.when(kc == 0)
    def _():
        m_ref[...] = jnp.full(m_ref.shape, NEG, F32)
        l_ref[...] = jnp.zeros(l_ref.shape, F32)
        acc_ref[...] = jnp.zeros(acc_ref.shape, F32)

    @pl.when(cidx >= 0)
    def _():
        rel = ((i - cidx) * tq + lax.broadcasted_iota(jnp.int32, (tq, tq), 0)
               - lax.broadcasted_iota(jnp.int32, (tq, tq), 1))
        bias = jnp.where(_dilated_mask(rel, window, dil), 0.0, NEG)
        scale = head_dim ** -0.5
        for h in range(n_heads):
            hs = slice(h * head_dim, (h + 1) * head_dim)
            qn = (_rms(q_ref[0, :, hs], gq_ref[...]) * scale).astype(BF16)
            lg = lax.dot_general(qn, k_ref[0, :, hs].astype(BF16), NT_DIMS, preferred_element_type=F32) + bias
            m_prev = m_ref[h]
            m_new = jnp.maximum(m_prev, jnp.max(lg, axis=-1, keepdims=True))
            alpha = jnp.exp(m_prev - m_new)
            p = jnp.exp(lg - m_new)
            l_ref[h] = alpha * l_ref[h] + jnp.sum(p, axis=-1, keepdims=True)
            acc_ref[h] = alpha * acc_ref[h] + jnp.dot(p.astype(BF16), v_ref[0, :, hs].astype(BF16),
                                                      preferred_element_type=F32)
            m_ref[h] = m_new

    @pl.when(kc == nkc - 1)
    def _():
        for h in range(n_heads):
            hs = slice(h * head_dim, (h + 1) * head_dim)
            o_ref[0, :, hs] = acc_ref[h] / l_ref[h]
            lse_ref[0, :, hs] = jnp.broadcast_to(m_ref[h] + jnp.log(l_ref[h]), (tq, head_dim))


def dilated_prompt_group(p, kn, g_q, *, q_block, k_block, v_block, window, dil, n_heads, head_dim, tq):
    b, s, _ = p.shape
    w = n_heads * head_dim
    nkc = -(-window // tq) + 1

    def kv_map(blk):
        return lambda bi, i, kc: (bi, jnp.maximum(i - (nkc - 1) + kc, 0), blk)

    return pl.pallas_call(
        functools.partial(_dil_prompt_body, window=window, dil=dil, n_heads=n_heads, head_dim=head_dim),
        grid=(b, s // tq, nkc),
        in_specs=[pl.BlockSpec((1, tq, w), lambda bi, i, kc: (bi, i, q_block)),
                  pl.BlockSpec((1, tq, w), kv_map(k_block)),
                  pl.BlockSpec((1, tq, w), kv_map(v_block)),
                  pl.BlockSpec((1, head_dim), lambda bi, i, kc: (0, 0))],
        out_specs=[pl.BlockSpec((1, tq, w), lambda bi, i, kc: (bi, i, 0)),
                   pl.BlockSpec((1, tq, w), lambda bi, i, kc: (bi, i, 0))],
        out_shape=[jax.ShapeDtypeStruct((b, s, w), F32), jax.ShapeDtypeStruct((b, s, w), F32)],
        scratch_shapes=[pltpu.VMEM((n_heads, tq, 1), F32), pltpu.VMEM((n_heads, tq, 1), F32),
                        pltpu.VMEM((n_heads, tq, head_dim), F32)],
        compiler_params=_params("parallel", "parallel", "arbitrary"),
        name="dilated_prompt",
    )(p, kn, p, g_q.reshape(1, head_dim))


def _dil_sample_body(p_ref, kn_ref, buf_ref, gq_ref, o_ref, lse_ref, nbuf_ref, *, window, dil, n_heads, head_dim,
                     q_off, k_off, v_off):
    t = p_ref.shape[0]
    rstride = 2 * n_heads
    wb = buf_ref.shape[1] // rstride
    w = n_heads * head_dim
    k_new = kn_ref[:, k_off:k_off + w]
    v_new = p_ref[:, v_off:v_off + w]
    nbuf_ref[0, 0:(wb - t) * rstride, :] = buf_ref[0, t * rstride:wb * rstride, :]
    for c, new in enumerate((k_new, v_new)):
        for h in range(n_heads):
            nbuf_ref[0, pl.ds((wb - t) * rstride + c * n_heads + h, t, stride=rstride), :] = (
                new[:, h * head_dim:(h + 1) * head_dim])

    rel = wb + lax.broadcasted_iota(jnp.int32, (t, wb), 0) - lax.broadcasted_iota(jnp.int32, (t, wb), 1)
    bias = jnp.where(_dilated_mask(rel, window, dil), 0.0, NEG)
    rel_o = lax.broadcasted_iota(jnp.int32, (t, LANES), 0) - lax.broadcasted_iota(jnp.int32, (t, LANES), 1)
    bias_o = jnp.where(_dilated_mask(rel_o, window, dil), 0.0, NEG)
    scale = head_dim ** -0.5
    for h in range(n_heads):
        hs = slice(h * head_dim, (h + 1) * head_dim)
        k_past = buf_ref[0, pl.ds(h, wb, stride=rstride), :]
        v_past = buf_ref[0, pl.ds(n_heads + h, wb, stride=rstride), :]
        qn = (_rms(p_ref[:, q_off + h * head_dim:q_off + (h + 1) * head_dim], gq_ref[...]) * scale).astype(BF16)
        lg = lax.dot_general(qn, k_past.astype(BF16), NT_DIMS, preferred_element_type=F32) + bias
        lg_o = lax.dot_general(qn, _pad_rows(k_new[:, hs], LANES).astype(BF16), NT_DIMS,
                               preferred_element_type=F32) + bias_o
        m = jnp.maximum(jnp.max(lg, axis=-1, keepdims=True), jnp.max(lg_o, axis=-1, keepdims=True))
        pp = jnp.exp(lg - m)
        pp_o = jnp.exp(lg_o - m)
        l = jnp.sum(pp, axis=-1, keepdims=True) + jnp.sum(pp_o, axis=-1, keepdims=True)
        acc = (jnp.dot(pp.astype(BF16), v_past.astype(BF16), preferred_element_type=F32)
               + jnp.dot(pp_o.astype(BF16), _pad_rows(v_new[:, hs], LANES).astype(BF16),
                         preferred_element_type=F32))
        o_ref[:, hs] = acc / l
        lse_ref[:, hs] = jnp.broadcast_to(m + jnp.log(l), (t, head_dim))


def dilated_sample_group(p, kn, buf, g_q, *, t, window, dil, n_heads, head_dim, q_off, k_off, v_off):
    bd, wb, w2 = buf.shape
    w = n_heads * head_dim
    n, nk = p.shape[1], kn.shape[1]
    return pl.pallas_call(
        functools.partial(_dil_sample_body, window=window, dil=dil, n_heads=n_heads, head_dim=head_dim,
                          q_off=q_off, k_off=k_off, v_off=v_off),
        grid=(bd,),
        in_specs=[pl.BlockSpec((t, n), lambda b: (b, 0)),
                  pl.BlockSpec((t, nk), lambda b: (b, 0)),
                  pl.BlockSpec((1, wb, w2), lambda b: (b, 0, 0)),
                  pl.BlockSpec((1, head_dim), lambda b: (0, 0))],
        out_specs=[pl.BlockSpec((t, w), lambda b: (b, 0)),
                   pl.BlockSpec((t, w), lambda b: (b, 0)),
                   pl.BlockSpec((1, wb, w2), lambda b: (b, 0, 0))],
        out_shape=[jax.ShapeDtypeStruct((bd * t, w), F32), jax.ShapeDtypeStruct((bd * t, w), F32),
                   jax.ShapeDtypeStruct((bd, wb, w2), F32)],
        compiler_params=_params("parallel"),
        name="dilated_sample",
    )(p, kn, buf, g_q.reshape(1, head_dim))


def _merge_groups_body(*refs):
    n = (len(refs) - 1) // 2
    o_refs, lse_refs, out_ref = refs[:n], refs[n:2 * n], refs[2 * n]
    lses = [r[...] for r in lse_refs]
    m = lses[0]
    for x in lses[1:]:
        m = jnp.maximum(m, x)
    ws = [jnp.exp(x - m) for x in lses]
    den = ws[0]
    for x in ws[1:]:
        den = den + x
    acc = ws[0] * o_refs[0][...]
    for wgt, o in zip(ws[1:], o_refs[1:]):
        acc = acc + wgt * o[...]
    out_ref[...] = (acc / den).astype(out_ref.dtype)


def merge_groups(outs, lses):
    m, w = outs[0].shape
    tm = min(m, ROW_TILE)
    spec = pl.BlockSpec((tm, w), lambda i: (i, 0))
    return pl.pallas_call(
        _merge_groups_body,
        grid=(m // tm,),
        in_specs=[spec] * (2 * len(outs)),
        out_specs=spec,
        out_shape=jax.ShapeDtypeStruct((m, w), BF16),
        compiler_params=_params("parallel"),
        name="merge_groups",
    )(*outs, *lses)


def _pool_body(u_ref, halo_ref, wg_ref, sc_ref, o_ref, ext_ref, *, n_pre, windows, halo_is_history):
    nb, t, d = u_ref.shape
    gw = d // len(windows)
    i = pl.program_id(1)
    halo = halo_ref[...]
    if halo_is_history:
        halo = jnp.where(i == 0, 0.0, halo)
    ext_ref[:, 0:POOL_HALO, :] = halo
    ext_ref[:, POOL_HALO:POOL_HALO + t, :] = u_ref[...]
    pos = i * t + lax.broadcasted_iota(jnp.int32, (1, t, 1), 1)
    for g, w in enumerate(windows):
        cs = slice(g * gw, (g + 1) * gw)
        acc = ext_ref[:, POOL_HALO:POOL_HALO + t, cs]
        for j in range(1, w):
            acc = acc + ext_ref[:, POOL_HALO - j:POOL_HALO - j + t, cs]
        div = jnp.minimum(w, n_pre + 1 + pos).astype(F32)
        r = acc / div - u_ref[:, :, cs]
        y = jnp.dot(r.reshape(nb * t, gw).astype(BF16), wg_ref[g], preferred_element_type=F32) * sc_ref[:, cs]
        o_ref[:, :, cs] = y.reshape(nb, t, gw).astype(o_ref.dtype)


def pool_mix(u, halo, w_grp, scale, *, n_pre, windows, batch_block, t_block, halo_is_history, out_dtype):
    b, t, d = u.shape
    hb = t_block // POOL_HALO
    if halo_is_history:
        halo_map = lambda bi, i: (bi, jnp.maximum(i * hb - 1, 0), 0)
    else:
        halo_map = lambda bi, i: (bi, 0, 0)
    return pl.pallas_call(
        functools.partial(_pool_body, n_pre=n_pre, windows=windows, halo_is_history=halo_is_history),
        grid=(b // batch_block, t // t_block),
        in_specs=[pl.BlockSpec((batch_block, t_block, d), lambda bi, i: (bi, i, 0)),
                  pl.BlockSpec((batch_block, POOL_HALO, d), halo_map),
                  pl.BlockSpec(w_grp.shape, lambda bi, i: (0, 0, 0)),
                  pl.BlockSpec((1, d), lambda bi, i: (0, 0))],
        out_specs=pl.BlockSpec((batch_block, t_block, d), lambda bi, i: (bi, i, 0)),
        out_shape=jax.ShapeDtypeStruct((b, t, d), out_dtype),
        scratch_shapes=[pltpu.VMEM((batch_block, POOL_HALO + t_block, d), F32)],
        compiler_params=_params("parallel", "parallel"),
        name="pool_mix",
    )(u, halo, w_grp, scale.reshape(1, d))


def _mlstm_body(q_ref, k_ref, v_ref, op_ref, gi_ref, gf_ref, bi_ref, bf_ref, gh_ref, c0_ref, n0_ref, m0_ref,
                y_ref, c_ref, n_ref, m_ref, cs, ns, ms, *, n_heads, head_dim):
    L = q_ref.shape[1]
    ls = max(L, LANES)
    ci = pl.program_id(1)
    hp = lax.Precision.HIGHEST

    @pl.when(ci == 0)
    def _():
        cs[...] = c0_ref[0]
        ns[...] = n0_ref[0]
        ms[...] = m0_ref[0]

    li = gi_ref[0] + bi_ref[...]
    lf = jax.nn.log_sigmoid(gf_ref[0] + bf_ref[...])
    row = lax.broadcasted_iota(jnp.int32, (L, ls), 0)
    col = lax.broadcasted_iota(jnp.int32, (L, ls), 1)
    causal = col <= row
    bt = jnp.dot(jnp.where(causal, 1.0, 0.0), _pad_rows(lf, ls), preferred_element_type=F32, precision=hp)
    lane = lax.broadcasted_iota(jnp.int32, (L, LANES), 1)
    lane1 = lax.broadcasted_iota(jnp.int32, (1, LANES), 1)
    m_all = ms[...]
    for h in range(n_heads):
        hs = slice(h * head_dim, (h + 1) * head_dim)
        bt_h = bt[:, h:h + 1]
        li_h = li[:, h:h + 1]
        m_prev = m_all[:, h:h + 1]
        a_h = jnp.where(lane == 0, bt_h, jnp.where(lane == 1, 1.0, 0.0))
        b_h = jnp.where(lane == 0, 1.0, jnp.where(lane == 1, li_h - bt_h, 0.0))
        dmat = lax.dot_general(a_h, _pad_rows(b_h, ls), NT_DIMS, preferred_element_type=F32, precision=hp)
        dmat = jnp.where(causal, dmat, -jnp.inf)
        g = bt_h + m_prev
        m_t = jnp.maximum(g, jnp.max(dmat, axis=-1, keepdims=True))
        w_intra = jnp.exp(dmat - m_t)
        w_inter = jnp.exp(g - m_t)
        q = q_ref[0, :, hs]
        k = k_ref[0, :, hs] * head_dim ** -0.5
        v = v_ref[0, :, hs]
        qb, kb, vb = q.astype(BF16), _pad_rows(k, ls).astype(BF16), _pad_rows(v, ls).astype(BF16)
        qk = lax.dot_general(qb, kb, NT_DIMS, preferred_element_type=F32) * w_intra
        c_h = cs[h]
        num = (w_inter * jnp.dot(qb, c_h.astype(BF16), preferred_element_type=F32)
               + jnp.dot(qk.astype(BF16), vb, preferred_element_type=F32))
        n_h = ns[h:h + 1, :]
        den = w_inter * jnp.sum(q * n_h, axis=-1, keepdims=True) + jnp.sum(qk, axis=-1, keepdims=True)
        hout = num / jnp.maximum(jnp.abs(den), jnp.exp(-m_t))
        b_last = bt_h[L - 1:L, :]
        decay = b_last - bt_h + li_h
        m_new = jnp.maximum(b_last + m_prev, jnp.max(decay, axis=0, keepdims=True))
        keep = jnp.exp(b_last + m_prev - m_new)
        kw = k * jnp.exp(decay - m_new)
        cs[h] = keep * c_h + lax.dot_general(_pad_rows(kw, ls).astype(BF16), vb, TN_DIMS,
                                             preferred_element_type=F32)
        ns[h:h + 1, :] = keep * n_h + jnp.sum(kw, axis=0, keepdims=True)
        m_all = jnp.where(lane1 == h, m_new, m_all)
        hn = _rms(hout, gh_ref[:, hs])
        y_ref[0, :, hs] = (jax.nn.sigmoid(op_ref[0, :, hs]) * hn).astype(y_ref.dtype)
    ms[...] = m_all

    @pl.when(ci == pl.num_programs(1) - 1)
    def _():
        c_ref[0] = cs[...]
        n_ref[0] = ns[...]
        m_ref[0] = ms[...]


def mlstm(p, gates, b_i, b_f, g_h, c0, n0, m0, *, n_heads, chunk, out_dtype):
    b, t, w4 = p.shape
    w = w4 // 4
    head_dim = w // n_heads
    nc = t // chunk

    def lane_row(x):
        return jnp.pad(x.reshape(1, -1), ((0, 0), (0, LANES - x.size)))

    m0p = jnp.pad(m0.reshape(b, 1, n_heads), ((0, 0), (0, 0), (0, LANES - n_heads)))
    colspec = lambda blk, wd: pl.BlockSpec((1, chunk, wd), lambda bi, ci: (bi, ci, blk))
    const = lambda shape: pl.BlockSpec(shape, lambda bi, ci: (0,) * len(shape))
    state = lambda shape: pl.BlockSpec((1,) + shape, lambda bi, ci: (bi,) + (0,) * len(shape))
    y, c, n, m = pl.pallas_call(
        functools.partial(_mlstm_body, n_heads=n_heads, head_dim=head_dim),
        grid=(b, nc),
        in_specs=[colspec(0, w), colspec(1, w), colspec(2, w), colspec(3, w),
                  colspec(0, LANES), colspec(1, LANES),
                  const((1, LANES)), const((1, LANES)), const((1, w)),
                  state((n_heads, head_dim, head_dim)), state((n_heads, head_dim)), state((1, LANES))],
        out_specs=[colspec(0, w), state((n_heads, head_dim, head_dim)), state((n_heads, head_dim)),
                   state((1, LANES))],
        out_shape=[jax.ShapeDtypeStruct((b, t, w), out_dtype),
                   jax.ShapeDtypeStruct((b, n_heads, head_dim, head_dim), F32),
                   jax.ShapeDtypeStruct((b, n_heads, head_dim), F32),
                   jax.ShapeDtypeStruct((b, 1, LANES), F32)],
        scratch_shapes=[pltpu.VMEM((n_heads, head_dim, head_dim), F32),
                        pltpu.VMEM((n_heads, head_dim), F32),
                        pltpu.VMEM((1, LANES), F32)],
        compiler_params=_params("parallel", "arbitrary"),
        name="mlstm",
    )(p, p, p, p, gates, gates, lane_row(b_i), lane_row(b_f), g_h.reshape(1, w), c0, n0, m0p)
    return y, c, n, m[:, 0, :n_heads]


def kernel(x_prompt, x_sample, cache_kv_a, cache_kidx_a, page_table, cache_kv_b0, cache_kv_b1, cache_kv_b2,
           state_pool_c, state_c_d, state_n_d, state_m_d, cache_mem_kv, mem_prompt,
           g_norm_mix, g_norm_mem, g_norm_ffn, g_mem_src, w_mem_q, w_mem_kv, g_mem_q, g_mem_k, w_mem_o,
           w_ff1, w_ff2, w_in_a, g_q_a, g_k_a, w_out_a, w_in_b, g_q_b, g_k_b, w_out_b,
           w_in_c, w_grp_c, scale_c, w_out_c, w_in_d, b_i_d, b_f_d, g_h_d, w_out_d):
    bp, s, d = x_prompt.shape
    bd, t, _ = x_sample.shape
    depth = g_norm_mix.shape[0]
    n_mixers = 4
    xp = x_prompt.reshape(bp * s, d)
    xs = x_sample.reshape(bd * t, d)
    cache_b = (cache_kv_b0, cache_kv_b1, cache_kv_b2)
    dilated = ((128, 1), (512, 4), (2048, 16))
    pool_windows = (2, 4, 8, 16)
    n_mem, n_heads_mem = mem_prompt.shape[1], cache_mem_kv.shape[4]
    mem2d = mem_prompt.reshape(bp * n_mem, d)
    hd_mem = cache_mem_kv.shape[5]
    mem_cache_rows = jnp.swapaxes(
        cache_mem_kv.reshape(depth, bd, n_mem, 2, n_heads_mem, hd_mem // LANES, LANES), 4, 5
    ).reshape(depth, bd, n_mem * 2 * hd_mem // LANES * n_heads_mem, LANES)

    kv_a_p, ki_a_p, kv_a_s, ki_a_s = [], [], [], []
    kv_b_p = [[] for _ in dilated]
    kv_b_s = [[] for _ in dilated]
    pool_p, pool_s = [], []
    cd_p, nd_p, md_p, cd_s, nd_s, md_s = [], [], [], [], [], []
    mem_p = []

    for i in range(depth):
        kind, j = i % n_mixers, i // n_mixers
        g_mix = g_norm_mix[i]
        if kind == 0:
            n_kv, hd = cache_kv_a.shape[4], cache_kv_a.shape[5]
            idx_dim = cache_kidx_a.shape[3]
            qw = w_out_a.shape[1]
            group = qw // (n_kv * hd)
            kw = n_kv * hd
            n_idx = (w_in_a.shape[2] - qw - 2 * kw - idx_dim) // (idx_dim + 1)
            iw = n_idx * idx_dim
            w = w_in_a[j]
            o3 = qw + 2 * kw
            ki_off = o3 + iw
            wi_off = ki_off + LANES
            n_cols = -(-(wi_off + LANES) // 512) * 512
            wa = jnp.concatenate([w[:, :o3 + iw], _pad_cols(w[:, o3 + iw:o3 + iw + idx_dim], LANES),
                                  _pad_cols(w[:, o3 + iw + idx_dim:], n_cols - wi_off)], axis=1).astype(BF16)
            wo = w_out_a[j].astype(BF16)
            pp = rms_matmul(xp, g_mix, wa)
            ps = rms_matmul(xs, g_mix, wa)
            kvp, kvp_bf = head_norm(pp, qw // kw, kw, hd, g_k_a[j], v_block=qw // kw + 1, with_bf16=True)
            kvs = head_norm(ps, qw // kw, kw, hd, g_k_a[j], v_block=qw // kw + 1)
            kip = pp[:, ki_off:ki_off + idx_dim]
            kis = ps[:, ki_off:ki_off + idx_dim]
            kv_a_p.append(kvp.reshape(bp, s, 2, n_kv, hd))
            ki_a_p.append(kip.reshape(bp, s, idx_dim))
            kv_a_s.append(kvs.reshape(bd, t, 2, n_kv, hd))
            ki_a_s.append(kis.reshape(bd, t, idx_dim))
            dims = dict(n_idx_heads=n_idx, idx_dim=idx_dim, n_kv=n_kv, group=group, head_dim=hd)
            op = dsa_prompt(pp.reshape(bp, s, n_cols), kip.astype(BF16).reshape(bp, s, idx_dim),
                            kvp_bf.reshape(bp, s, 2 * kw), g_q_a[j], g_k_a[j], q_block=0, qi_block=o3 // iw,
                            wi_block=wi_off // LANES, n_top=min(256, s // 4), **dims)
            n_pool, page = cache_kidx_a.shape[1], cache_kidx_a.shape[2]
            past = page_table.shape[1] * page
            kidx_pool = jnp.swapaxes(cache_kidx_a[j], 1, 2)
            kv_pool = cache_kv_a[j].reshape(n_pool, page * 2 * n_kv, hd)
            os_ = dsa_sample(ps, kvs, kidx_pool, kv_pool, page_table,
                             g_q_a[j], t=t, n_top=min(256, (past + t) // 4), q_off=0, qi_off=o3, ki_off=ki_off,
                             wi_off=wi_off, **dims)
            xp = matmul_res(op.reshape(bp * s, qw), wo, xp)
            xs = matmul_res(os_, wo, xs)
        elif kind == 1:
            n_groups = len(dilated)
            hg, hd = cache_kv_b0.shape[4], cache_kv_b0.shape[5]
            gw = hg * hd
            wb_ = w_in_b[j].astype(BF16)
            wo = w_out_b[j].astype(BF16)
            pp = rms_matmul(xp, g_mix, wb_)
            ps = rms_matmul(xs, g_mix, wb_)
            gk_row = jnp.repeat(g_k_b[j], hg, axis=0).reshape(1, n_groups * gw)
            knp = head_norm_groups(pp, 1, n_groups * gw, gk_row)
            kns = head_norm_groups(ps, 1, n_groups * gw, gk_row)
            pp3 = pp.reshape(bp, s, -1)
            knp3 = knp.reshape(bp, s, -1)
            outs_p, lses_p, outs_s, lses_s = [], [], [], []
            for g, (win, dil) in enumerate(dilated):
                o, lse = dilated_prompt_group(pp3, knp3, g_q_b[j, g], q_block=g, k_block=g,
                                              v_block=2 * n_groups + g, window=win, dil=dil, n_heads=hg,
                                              head_dim=hd, tq=min(s, 256 if win <= 512 else 512))
                outs_p.append(o.reshape(bp * s, gw))
                lses_p.append(lse.reshape(bp * s, gw))
                wb = min(win, s)
                kv_b_p[g].append(jnp.stack(
                    [knp3[:, s - wb:, g * gw:(g + 1) * gw].reshape(bp, wb, hg, hd),
                     pp3[:, s - wb:, (2 * n_groups + g) * gw:(2 * n_groups + g + 1) * gw].reshape(bp, wb, hg, hd)],
                    axis=2))
                buf = cache_b[g][j]
                o, lse, nbuf = dilated_sample_group(ps, kns, buf.reshape(bd, buf.shape[1] * 2 * hg, hd), g_q_b[j, g],
                                                    t=t, window=win, dil=dil, n_heads=hg, head_dim=hd,
                                                    q_off=g * gw, k_off=g * gw, v_off=(2 * n_groups + g) * gw)
                outs_s.append(o)
                lses_s.append(lse)
                kv_b_s[g].append(nbuf.reshape(buf.shape))
            xp = matmul_res(merge_groups(outs_p, lses_p), wo, xp)
            xs = matmul_res(merge_groups(outs_s, lses_s), wo, xs)
        elif kind == 2:
            wi_, wo = w_in_c[j].astype(BF16), w_out_c[j].astype(BF16)
            wg = w_grp_c[j].astype(BF16)
            up = rms_matmul(xp, g_mix, wi_).reshape(bp, s, d)
            us = rms_matmul(xs, g_mix, wi_).reshape(bd, t, d)
            n_state = state_pool_c.shape[2]
            yp = pool_mix(up, up, wg, scale_c[j], n_pre=0, windows=pool_windows, batch_block=1, t_block=512,
                          halo_is_history=True, out_dtype=BF16)
            halo = jnp.pad(state_pool_c[j], ((0, 0), (POOL_HALO - n_state, 0), (0, 0)))
            ys = pool_mix(us, halo, wg, scale_c[j], n_pre=n_state, windows=pool_windows, batch_block=min(bd, 32),
                          t_block=t, halo_is_history=False, out_dtype=F32)
            pool_p.append(up[:, s - n_state:])
            pool_s.append(jnp.concatenate([state_pool_c[j], us], axis=1)[:, -n_state:])
            xp = matmul_res(yp.reshape(bp * s, d), wo, xp)
            xs = matmul_res(ys.reshape(bd * t, d), wo, xs)
        else:
            nh, hd = state_c_d.shape[2], state_c_d.shape[3]
            wd = nh * hd
            w = w_in_d[j]
            w_main = w[:, :4 * wd].astype(BF16)
            w_gate = jnp.concatenate([_pad_cols(w[:, 4 * wd:4 * wd + nh], LANES),
                                      _pad_cols(w[:, 4 * wd + nh:], LANES)], axis=1)
            wo = w_out_d[j].astype(BF16)
            pp = rms_matmul(xp, g_mix, w_main).reshape(bp, s, 4 * wd)
            ps = rms_matmul(xs, g_mix, w_main).reshape(bd, t, 4 * wd)
            gp = rms_matmul(xp, g_mix, w_gate, precise=True).reshape(bp, s, 2 * LANES)
            gs = rms_matmul(xs, g_mix, w_gate, precise=True).reshape(bd, t, 2 * LANES)
            zeros = lambda *shape: jnp.zeros(shape, F32)
            yp, c, n, m = mlstm(pp, gp, b_i_d[j], b_f_d[j], g_h_d[j], zeros(bp, nh, hd, hd), zeros(bp, nh, hd),
                                zeros(bp, nh), n_heads=nh, chunk=256, out_dtype=BF16)
            cd_p.append(c)
            nd_p.append(n)
            md_p.append(m)
            ys, c, n, m = mlstm(ps, gs, b_i_d[j], b_f_d[j], g_h_d[j], state_c_d[j], state_n_d[j], state_m_d[j],
                                n_heads=nh, chunk=t, out_dtype=F32)
            cd_s.append(c)
            nd_s.append(n)
            md_s.append(m)
            xp = matmul_res(yp.reshape(bp * s, wd), wo, xp)
            xs = matmul_res(ys.reshape(bd * t, wd), wo, xs)

        wm = n_heads_mem * cache_mem_kv.shape[5]
        mkv_raw = rms_matmul(mem2d, g_mem_src[i], w_mem_kv[i].astype(BF16))
        mkv = head_norm(mkv_raw, 0, wm, wm // n_heads_mem, g_mem_k[i], v_block=1)
        mem_p.append(mkv.reshape(bp, n_mem, 2, n_heads_mem, wm // n_heads_mem))
        wq, wo = w_mem_q[i].astype(BF16), w_mem_o[i].astype(BF16)
        qp = rms_matmul(xp, g_norm_mem[i], wq).reshape(bp, s, wm)
        qs = rms_matmul(xs, g_norm_mem[i], wq).reshape(bd, t, wm)
        ap = mem_attn(qp, mkv.reshape(bp, n_mem, 2 * wm), g_mem_q[i], n_heads=n_heads_mem, n_mem=n_mem,
                      batch_block=1, q_block=512, out_dtype=BF16)
        as_ = mem_attn(qs, mem_cache_rows, g_mem_q[i], n_heads=n_heads_mem, n_mem=n_mem, batch_block=4,
                       q_block=t, out_dtype=F32, layer=i)
        xp = matmul_res(ap.reshape(bp * s, wm), wo, xp)
        xs = matmul_res(as_.reshape(bd * t, wm), wo, xs)

        w1, w2 = w_ff1[i].astype(BF16), w_ff2[i].astype(BF16)
        xp = ffn(xp, g_norm_ffn[i], w1, w2)
        xs = ffn(xs, g_norm_ffn[i], w1, w2)

    return (xp.reshape(bp, s, d), xs.reshape(bd, t, d),
            jnp.stack(kv_a_p), jnp.stack(ki_a_p), jnp.stack(kv_a_s), jnp.stack(ki_a_s),
            jnp.stack(kv_b_p[0]), jnp.stack(kv_b_p[1]), jnp.stack(kv_b_p[2]),
            jnp.stack(kv_b_s[0]), jnp.stack(kv_b_s[1]), jnp.stack(kv_b_s[2]),
            jnp.stack(pool_p), jnp.stack(pool_s),
            jnp.stack(cd_p), jnp.stack(nd_p), jnp.stack(md_p),
            jnp.stack(cd_s), jnp.stack(nd_s), jnp.stack(md_s),
            jnp.stack(mem_p))
```

```python
import functools

import jax
import jax.numpy as jnp
from jax import lax
from jax.experimental import pallas as pl
from jax.experimental.pallas import tpu as pltpu

F32 = jnp.float32
BF16 = jnp.bfloat16
EPS = 1e-6
NEG = -1e30
LANES = 128
SUBLANES = 8
VMEM_LIMIT_BYTES = 56 * 2**20
ROW_TILE = 1024
POOL_HALO = 16
MAX_BISECTIONS = 512
SAFE_SHIFT = 40.0
LOG2E = 1.4426950408889634
BF16_ROWS = 16
COARSE_BISECTIONS = 8
FOLD_CHAINS = 4

NT_DIMS = (((1,), (1,)), ((), ()))
TN_DIMS = (((0,), (0,)), ((), ()))


def _params(*sem):
    return pltpu.CompilerParams(dimension_semantics=sem, vmem_limit_bytes=VMEM_LIMIT_BYTES)


def _rms(x, g):
    return x * lax.rsqrt(jnp.mean(x * x, axis=-1, keepdims=True) + EPS) * g


def _col_tile(n):
    for t in (1536, 1024, 768, 512, 384, 256, 128):
        if n % t == 0:
            return t
    raise ValueError(f"matmul width {n} is not a multiple of {LANES}")


def _pad_rows(x, rows):
    if x.shape[0] == rows:
        return x
    return jnp.concatenate([x, jnp.zeros((rows - x.shape[0],) + x.shape[1:], x.dtype)], axis=0)


def _pad_cols(w, n):
    return jnp.pad(w, ((0, 0), (0, n - w.shape[1])))


def _rms_matmul_body(x_ref, g_ref, w_ref, o_ref, h_ref, *, precision):
    @pl.when(pl.program_id(1) == 0)
    def _():
        h_ref[...] = _rms(x_ref[...], g_ref[...]).astype(h_ref.dtype)

    o_ref[...] = jnp.dot(h_ref[...], w_ref[...], preferred_element_type=F32, precision=precision)


def rms_matmul(x, g, w, *, precise=False):
    m, d = x.shape
    n = w.shape[1]
    tm, tn = min(m, ROW_TILE), _col_tile(n)
    return pl.pallas_call(
        functools.partial(_rms_matmul_body, precision=lax.Precision.HIGHEST if precise else None),
        grid=(m // tm, n // tn),
        in_specs=[pl.BlockSpec((tm, d), lambda i, j: (i, 0)),
                  pl.BlockSpec((1, d), lambda i, j: (0, 0)),
                  pl.BlockSpec((d, tn), lambda i, j: (0, j))],
        out_specs=pl.BlockSpec((tm, tn), lambda i, j: (i, j)),
        out_shape=jax.ShapeDtypeStruct((m, n), F32),
        scratch_shapes=[pltpu.VMEM((tm, d), F32 if precise else BF16)],
        compiler_params=_params("parallel", "arbitrary"),
        name="rms_matmul",
    )(x, g.reshape(1, d), w)


def _matmul_res_body(a_ref, w_ref, r_ref, o_ref):
    o_ref[...] = r_ref[...] + jnp.dot(a_ref[...].astype(BF16), w_ref[...], preferred_element_type=F32)


def matmul_res(a, w, r):
    m, k = a.shape
    n = w.shape[1]
    tm, tn = min(m, ROW_TILE), _col_tile(n)
    return pl.pallas_call(
        _matmul_res_body,
        grid=(m // tm, n // tn),
        in_specs=[pl.BlockSpec((tm, k), lambda i, j: (i, 0)),
                  pl.BlockSpec((k, tn), lambda i, j: (0, j)),
                  pl.BlockSpec((tm, tn), lambda i, j: (i, j))],
        out_specs=pl.BlockSpec((tm, tn), lambda i, j: (i, j)),
        out_shape=jax.ShapeDtypeStruct((m, n), F32),
        compiler_params=_params("parallel", "parallel"),
        name="matmul_res",
    )(a, w, r)


def _ffn_body(x_ref, g_ref, w1_ref, w2_ref, o_ref, h_ref, acc_ref):
    f = pl.program_id(1)

    @pl.when(f == 0)
    def _():
        h_ref[...] = _rms(x_ref[...], g_ref[...]).astype(BF16)
        acc_ref[...] = jnp.zeros_like(acc_ref)

    a = jnp.dot(h_ref[...], w1_ref[...], preferred_element_type=F32)
    a = jnp.square(jnp.maximum(a, 0.0)).astype(BF16)
    acc_ref[...] += jnp.dot(a, w2_ref[...], preferred_element_type=F32)

    @pl.when(f == pl.num_programs(1) - 1)
    def _():
        o_ref[...] = x_ref[...] + acc_ref[...]


def ffn(x, g, w1, w2):
    m, d = x.shape
    ff = w1.shape[1]
    tm, tf = min(m, ROW_TILE), 1024
    return pl.pallas_call(
        _ffn_body,
        grid=(m // tm, ff // tf),
        in_specs=[pl.BlockSpec((tm, d), lambda i, f: (i, 0)),
                  pl.BlockSpec((1, d), lambda i, f: (0, 0)),
                  pl.BlockSpec((d, tf), lambda i, f: (0, f)),
                  pl.BlockSpec((tf, d), lambda i, f: (f, 0))],
        out_specs=pl.BlockSpec((tm, d), lambda i, f: (i, 0)),
        out_shape=jax.ShapeDtypeStruct((m, d), F32),
        scratch_shapes=[pltpu.VMEM((tm, d), BF16), pltpu.VMEM((tm, d), F32)],
        compiler_params=_params("parallel", "arbitrary"),
        name="ffn",
    )(x, g.reshape(1, d), w1, w2)


def _head_norm_body(*refs, head_dim, n_heads, with_copy, with_bf16):
    k_ref, g_ref = refs[0], refs[1]
    v_ref = refs[2] if with_copy else None
    outs = refs[3 if with_copy else 2:]
    width = n_heads * head_dim
    for h in range(n_heads):
        sl = slice(h * head_dim, (h + 1) * head_dim)
        kn = _rms(k_ref[:, sl], g_ref[:, sl])
        for o in outs:
            o[:, sl] = kn.astype(o.dtype)
    if with_copy:
        v = v_ref[...]
        for o in outs:
            o[:, width:] = v.astype(o.dtype)


def head_norm(p, k_block, width, head_dim, gain, *, v_block=None, with_bf16=False):
    m = p.shape[0]
    n_heads = width // head_dim
    tm = min(m, ROW_TILE)
    with_copy = v_block is not None
    ow = 2 * width if with_copy else width
    in_specs = [pl.BlockSpec((tm, width), lambda i: (i, k_block)),
                pl.BlockSpec((1, width), lambda i: (0, 0))]
    args = [p, jnp.tile(gain.reshape(1, head_dim), (1, n_heads))]
    if with_copy:
        in_specs.append(pl.BlockSpec((tm, width), lambda i: (i, v_block)))
        args.append(p)
    out_shape = [jax.ShapeDtypeStruct((m, ow), F32)]
    if with_bf16:
        out_shape.append(jax.ShapeDtypeStruct((m, ow), BF16))
    out = pl.pallas_call(
        functools.partial(_head_norm_body, head_dim=head_dim, n_heads=n_heads, with_copy=with_copy,
                          with_bf16=with_bf16),
        grid=(m // tm,),
        in_specs=in_specs,
        out_specs=[pl.BlockSpec((tm, ow), lambda i: (i, 0)) for _ in out_shape],
        out_shape=out_shape,
        compiler_params=_params("parallel"),
        name="head_norm",
    )(*args)
    return out if with_bf16 else out[0]


def head_norm_groups(p, k_block, width, gain_row):
    m = p.shape[0]
    tm = min(m, ROW_TILE)
    n_heads = width // LANES
    return pl.pallas_call(
        functools.partial(_head_norm_body, head_dim=LANES, n_heads=n_heads, with_copy=False, with_bf16=False),
        grid=(m // tm,),
        in_specs=[pl.BlockSpec((tm, width), lambda i: (i, k_block)),
                  pl.BlockSpec((1, width), lambda i: (0, 0))],
        out_specs=[pl.BlockSpec((tm, width), lambda i: (i, 0))],
        out_shape=[jax.ShapeDtypeStruct((m, width), F32)],
        compiler_params=_params("parallel"),
        name="head_norm_groups",
    )(p, gain_row)[0]


def _mem_attn_body(q_ref, kv_ref, gq_ref, o_ref, *, n_heads, head_dim, n_mem, tiled_rows):
    width = n_heads * head_dim
    scale = head_dim ** -0.5
    n_lt = head_dim // LANES
    rstride = 2 * n_lt * n_heads

    def tiled(b, c, h):
        parts = [kv_ref[b, pl.ds((c * n_lt + lt) * n_heads + h, n_mem, stride=rstride), :] for lt in range(n_lt)]
        return jnp.concatenate(parts, axis=1)

    for b in range(q_ref.shape[0]):
        for h in range(n_heads):
            sl = slice(h * head_dim, (h + 1) * head_dim)
            qn = (_rms(q_ref[b, :, sl], gq_ref[...]) * scale).astype(BF16)
            if tiled_rows:
                k, v = tiled(b, 0, h).astype(BF16), tiled(b, 1, h).astype(BF16)
            else:
                k = kv_ref[b, :, sl].astype(BF16)
                v = kv_ref[b, :, width + h * head_dim:width + (h + 1) * head_dim].astype(BF16)
            lg = lax.dot_general(qn, k, NT_DIMS, preferred_element_type=F32)
            p = jnp.exp(lg - jnp.max(lg, axis=-1, keepdims=True))
            l = jnp.sum(p, axis=-1, keepdims=True)
            o = jnp.dot(p.astype(BF16), v, preferred_element_type=F32) / l
            o_ref[b, :, sl] = o.astype(o_ref.dtype)


def mem_attn(q, kv, g_q, *, n_heads, n_mem, batch_block, q_block, out_dtype, layer=None):
    b, t, w = q.shape
    head_dim = w // n_heads
    if layer is None:
        kv_spec = pl.BlockSpec((batch_block, n_mem, 2 * w), lambda i, j: (i, 0, 0))
    else:
        kv_spec = pl.BlockSpec((None, batch_block) + kv.shape[2:], lambda i, j: (layer, i, 0, 0))
    return pl.pallas_call(
        functools.partial(_mem_attn_body, n_heads=n_heads, head_dim=head_dim, n_mem=n_mem,
                          tiled_rows=layer is not None),
        grid=(b // batch_block, t // q_block),
        in_specs=[pl.BlockSpec((batch_block, q_block, w), lambda i, j: (i, j, 0)),
                  kv_spec,
                  pl.BlockSpec((1, head_dim), lambda i, j: (0, 0))],
        out_specs=pl.BlockSpec((batch_block, q_block, w), lambda i, j: (i, j, 0)),
        out_shape=jax.ShapeDtypeStruct((b, t, w), out_dtype),
        compiler_params=_params("parallel", "parallel"),
        name="mem_attn",
    )(q, kv, g_q.reshape(1, head_dim))


def _fold(x, op, ka):
    if ka == 1:
        parts = [x[:, j * LANES:(j + 1) * LANES] for j in range(x.shape[1] // LANES)]
    else:
        parts = [x[j * SUBLANES:(j + 1) * SUBLANES, :] for j in range(x.shape[0] // SUBLANES)]
    accs = parts[:FOLD_CHAINS]
    for i, part in enumerate(parts[FOLD_CHAINS:]):
        accs[i % FOLD_CHAINS] = op(accs[i % FOLD_CHAINS], part)
    while len(accs) > 1:
        accs = [op(accs[i], accs[i + 1]) if i + 1 < len(accs) else accs[i] for i in range(0, len(accs), 2)]
    return accs[0]


def _fold_lanes(x):
    return _fold(x, jnp.add, 1)


def _folded_shape(chunk_shape, ka):
    return (chunk_shape[0], LANES) if ka == 1 else (SUBLANES, chunk_shape[1])


def _chunk_loop(n_ch, body, init):
    if isinstance(n_ch, int):
        carry = init
        for c in range(n_ch):
            carry = body(c, carry)
        return carry
    return lax.fori_loop(0, n_ch, body, init)


def _count(sc_ref, n_ch, pred, ka):
    def body(c, acc):
        return acc + _fold(jnp.where(pred(sc_ref[c], c), 1.0, 0.0), jnp.add, ka)

    acc = _chunk_loop(n_ch, body, jnp.zeros(_folded_shape(sc_ref.shape[1:], ka), F32))
    return jnp.sum(acc, axis=ka, keepdims=True)


def _minmax_update(carry, s, ka):
    mn, mx = carry
    mn = jnp.minimum(mn, _fold(jnp.where(s > -jnp.inf, s, jnp.inf), jnp.minimum, ka))
    return mn, jnp.maximum(mx, _fold(s, jnp.maximum, ka))


def _minmax_init(chunk_shape, ka):
    shape = _folded_shape(chunk_shape, ka)
    return jnp.full(shape, jnp.inf, F32), jnp.full(shape, -jnp.inf, F32)


def _key_index(c, chunk_shape, ka):
    shape = (1, chunk_shape[1]) if ka == 1 else (chunk_shape[0], 1)
    return (c * chunk_shape[ka] + lax.broadcasted_iota(jnp.int32, shape, ka)).astype(F32)


def _count_ge_bf16(sb_ref, n_ch, theta_b):
    ch, nq = sb_ref.shape[1:]
    th = jnp.broadcast_to(theta_b, (BF16_ROWS, nq))
    one, zero = jnp.ones((BF16_ROWS, nq), BF16), jnp.zeros((BF16_ROWS, nq), BF16)

    def body(c, acc):
        x = sb_ref[c]
        parts = [jnp.where(x[j * BF16_ROWS:(j + 1) * BF16_ROWS, :] >= th, one, zero)
                 for j in range(ch // BF16_ROWS)]
        accs = parts[:FOLD_CHAINS]
        for j, part in enumerate(parts[FOLD_CHAINS:]):
            accs[j % FOLD_CHAINS] = accs[j % FOLD_CHAINS] + part
        total = accs[0].astype(F32)
        for a in accs[1:]:
            total = total + a.astype(F32)
        return acc + total

    acc = lax.fori_loop(0, n_ch, body, jnp.zeros((BF16_ROWS, nq), F32))
    return jnp.sum(acc, axis=0, keepdims=True)


def _select_threshold(sc_ref, kk, n_ch, minmax, n_finite, ka, sb_ref=None):
    chunk = sc_ref.shape[1:]
    ch = chunk[ka]
    lo = jnp.min(minmax[0], axis=ka, keepdims=True)
    mx = jnp.max(minmax[1], axis=ka, keepdims=True)
    hi = mx + jnp.maximum(jnp.abs(mx) * 2.0**-22, 1e-36)
    c_lo = n_finite
    done = jnp.where(c_lo == kk, 1.0, 0.0)

    if sb_ref is not None:
        def coarse(_, st):
            lo, hi = st
            t_b = (lo * 0.5 + hi * 0.5).astype(BF16)
            t = t_b.astype(F32)
            below = t - jnp.maximum(jnp.abs(t) * 2.0**-6, 1e-30)
            c_b = _count_ge_bf16(sb_ref, n_ch, t_b)
            live = done < 0.5
            up = live & (c_b >= kk) & (below > lo) & (below < hi)
            dn = live & (c_b < kk) & (t < hi) & (t > lo)
            return jnp.where(up, below, lo), jnp.where(dn, t, hi)

        lo, hi = lax.fori_loop(0, COARSE_BISECTIONS, coarse, (lo, hi))
        c_lo = jnp.where(done < 0.5, _count(sc_ref, n_ch, lambda s, c: s >= lo, ka), c_lo)
        done = jnp.where(c_lo == kk, 1.0, done)

    def cond(st):
        return (jnp.min(st[3]) < 0.5) & (st[4] < MAX_BISECTIONS)

    def body(st):
        lo, hi, c_lo, done, it = st
        mid = lo * 0.5 + hi * 0.5
        stuck = (mid <= lo) | (mid >= hi)
        c_mid = _count(sc_ref, n_ch, lambda s, c: s >= mid, ka)
        live = (done < 0.5) & jnp.logical_not(stuck)
        up = live & (c_mid >= kk)
        dn = live & (c_mid < kk)
        lo = jnp.where(up, mid, lo)
        c_lo = jnp.where(up, c_mid, c_lo)
        hi = jnp.where(dn, mid, hi)
        done = jnp.where(stuck | (c_lo == kk), 1.0, done)
        return lo, hi, c_lo, done, it + 1

    theta, _, c_theta, _, _ = lax.while_loop(cond, body, (lo, hi, c_lo, done, jnp.int32(0)))

    n_keys = n_ch * ch
    tie = c_theta > kk
    no_tie_jmax = jnp.full(kk.shape, 2.0**30, F32)

    def resolve(_):
        need = kk - _count(sc_ref, n_ch, lambda s, c: s > theta, ka)
        n_iter = max(1, (sc_ref.shape[0] * ch - 1).bit_length())

        def step(_, st):
            jl, jh = st
            jm = jnp.floor((jl + jh) * 0.5)
            c_m = _count(sc_ref, n_ch, lambda s, c: (s == theta) & (_key_index(c, chunk, ka) <= jm), ka)
            ok = c_m >= need
            return jnp.where(ok, jl, jm + 1.0), jnp.where(ok, jm, jh)

        jl0 = jnp.zeros(kk.shape, F32)
        jh0 = jnp.zeros(kk.shape, F32) + (n_keys - 1)
        _, jh = lax.fori_loop(0, n_iter, step, (jl0, jh0))
        return jnp.where(tie, jh, no_tie_jmax)

    any_tie = jnp.max(jnp.where(tie, 1.0, 0.0)) > 0.5
    jmax = lax.cond(any_tie, resolve, lambda _: no_tie_jmax, 0)
    return theta, jmax


def _scores_to_bias(sc_ref, n_ch, theta, jmax, ka):
    chunk = sc_ref.shape[1:]

    def body(c, carry):
        s = sc_ref[c]
        sel = (s > theta) | ((s == theta) & (_key_index(c, chunk, ka) <= jmax))
        sc_ref[c] = jnp.where(sel, 0.0, NEG)
        return carry

    _chunk_loop(n_ch, body, 0)


def _dsa_prompt_body(q_ref, qi_ref, wi_ref, ki_ref, kn_ref, vt_ref, gq_ref, gk_ref, o_ref,
                     sc_ref, sb_ref, *, tq, n_top, n_idx_heads, idx_dim, n_kv, group, head_dim):
    ka = 0
    ch = sc_ref.shape[1]
    t0 = pl.program_id(1) * tq
    n_ch = (t0 + tq + ch - 1) // ch
    qpos = t0 + lax.broadcasted_iota(jnp.int32, (1, tq), 1)

    qi = qi_ref[0]
    wi_t = wi_ref[0].T
    qi_h = [qi[:, h * idx_dim:(h + 1) * idx_dim].astype(BF16) for h in range(n_idx_heads)]
    w_row = [wi_t[h:h + 1, :] for h in range(n_idx_heads)]

    def score_chunk(c, carry):
        ki = ki_ref[0, pl.ds(pl.multiple_of(c * ch, ch), ch), :]
        acc = jnp.zeros((ch, tq), F32)
        for h in range(n_idx_heads):
            s = lax.dot_general(ki, qi_h[h], NT_DIMS, preferred_element_type=F32)
            acc = acc + jnp.maximum(s, 0.0) * w_row[h]
        kpos = c * ch + lax.broadcasted_iota(jnp.int32, (ch, 1), 0)
        acc = jnp.where(kpos <= qpos, acc, -jnp.inf)
        sc_ref[c] = acc
        sb_ref[c] = acc.astype(BF16)
        return _minmax_update(carry, acc, ka)

    minmax = lax.fori_loop(0, n_ch, score_chunk, _minmax_init((ch, tq), ka))

    n_causal = (qpos + 1).astype(F32)
    kk = jnp.minimum(n_causal, float(n_top))
    theta, jmax = _select_threshold(sc_ref, kk, n_ch, minmax, n_causal, ka, sb_ref)
    _scores_to_bias(sc_ref, n_ch, theta, jmax, ka)

    scale = head_dim ** -0.5 * LOG2E
    q = q_ref[0]
    qf = []
    for h in range(n_kv):
        qs = [_rms(q[:, (h * group + g) * head_dim:(h * group + g + 1) * head_dim], gq_ref[...]) * scale
              for g in range(group)]
        qf.append(jnp.concatenate(qs, axis=0))
    qst = [x.astype(BF16) for x in qf]
    gt = group * tq

    def logits(c, h):
        off = pl.multiple_of(c * ch, ch)
        k = kn_ref[0, pl.ds(off, ch), h * head_dim:(h + 1) * head_dim]
        return (lax.dot_general(k, qst[h], NT_DIMS, preferred_element_type=F32)
                + jnp.concatenate([sc_ref[c]] * group, axis=1))

    def softmax_pv(m_rows):
        def sum_chunk(c, carry):
            lf, acc = carry
            lf_out, acc_out = [], []
            for h in range(n_kv):
                vt = vt_ref[0, c, h * head_dim:(h + 1) * head_dim, :]
                p = jnp.exp2(logits(c, h) - m_rows[h])
                lf_out.append(lf[h] + _fold(p, jnp.add, ka))
                acc_out.append(acc[h] + jnp.dot(vt, p.astype(BF16), preferred_element_type=F32))
            return tuple(lf_out), tuple(acc_out)

        lf, acc = lax.fori_loop(0, n_ch, sum_chunk, (tuple(jnp.zeros((SUBLANES, gt), F32) for _ in range(n_kv)),
                                                      tuple(jnp.zeros((head_dim, gt), F32) for _ in range(n_kv))))
        for h in range(n_kv):
            out = (acc[h] / jnp.sum(lf[h], axis=0, keepdims=True)).T
            for g in range(group):
                o_ref[0, :, (h * group + g) * head_dim:(h * group + g + 1) * head_dim] = (
                    out[g * tq:(g + 1) * tq].astype(o_ref.dtype))

    k_norm = head_dim ** 0.5 * jnp.max(jnp.abs(gk_ref[...]), axis=-1, keepdims=True)
    ones = jnp.ones((SUBLANES, head_dim), F32)
    bound = [jnp.sqrt(lax.dot_general(ones, x * x, NT_DIMS, preferred_element_type=F32)[0:1, :]) * k_norm * 1.01
             for x in qf]
    worst = bound[0].max()
    for x in bound[1:]:
        worst = jnp.maximum(worst, x.max())
    bound_is_safe = worst <= SAFE_SHIFT * LOG2E

    @pl.when(bound_is_safe)
    def _():
        softmax_pv(bound)

    @pl.when(jnp.logical_not(bound_is_safe))
    def _():
        def max_chunk(c, mf):
            return tuple(jnp.maximum(mf[h], _fold(logits(c, h), jnp.maximum, ka)) for h in range(n_kv))

        mf = lax.fori_loop(0, n_ch, max_chunk, tuple(jnp.full((SUBLANES, gt), NEG, F32) for _ in range(n_kv)))
        softmax_pv([jnp.max(x, axis=0, keepdims=True) for x in mf])


def dsa_prompt(p, ki, kv, g_q, g_k, *, q_block, qi_block, wi_block, n_top, n_idx_heads, idx_dim, n_kv, group,
               head_dim, tq=256, ch=512):
    b, s, _ = p.shape
    qw = n_kv * group * head_dim
    kw = n_kv * head_dim
    iw = n_idx_heads * idx_dim
    vt = jnp.swapaxes(kv[:, :, kw:].reshape(b, s // ch, ch, kw), 2, 3)
    body = functools.partial(_dsa_prompt_body, tq=tq, n_top=n_top, n_idx_heads=n_idx_heads, idx_dim=idx_dim,
                             n_kv=n_kv, group=group, head_dim=head_dim)
    return pl.pallas_call(
        body,
        grid=(b, s // tq),
        in_specs=[pl.BlockSpec((1, tq, qw), lambda i, j: (i, j, q_block)),
                  pl.BlockSpec((1, tq, iw), lambda i, j: (i, j, qi_block)),
                  pl.BlockSpec((1, tq, LANES), lambda i, j: (i, j, wi_block)),
                  pl.BlockSpec((1, s, idx_dim), lambda i, j: (i, 0, 0), pipeline_mode=pl.Buffered(1)),
                  pl.BlockSpec((1, s, kw), lambda i, j: (i, 0, 0), pipeline_mode=pl.Buffered(1)),
                  pl.BlockSpec((1, s // ch, kw, ch), lambda i, j: (i, 0, 0, 0), pipeline_mode=pl.Buffered(1)),
                  pl.BlockSpec((1, head_dim), lambda i, j: (0, 0)),
                  pl.BlockSpec((1, head_dim), lambda i, j: (0, 0))],
        out_specs=pl.BlockSpec((1, tq, qw), lambda i, j: (i, j, 0)),
        out_shape=jax.ShapeDtypeStruct((b, s, qw), BF16),
        scratch_shapes=[pltpu.VMEM((s // ch, ch, tq), F32), pltpu.VMEM((s // ch, ch, tq), BF16)],
        compiler_params=_params("parallel", "arbitrary"),
        name="dsa_prompt",
    )(p, p, p, ki, kv, vt, g_q.reshape(1, head_dim), g_k.reshape(1, head_dim))


def _dsa_sample_body(pt_ref, p_ref, kvn_ref, gq_ref, *refs, n_pages, page, n_top, n_idx_heads, idx_dim, n_kv,
                     group, head_dim, q_off, qi_off, ki_off, wi_off):
    del pt_ref
    kidx_refs = refs[:n_pages]
    kv_refs = refs[n_pages:2 * n_pages]
    o_ref, sc_ref = refs[2 * n_pages], refs[2 * n_pages + 1]
    t = p_ref.shape[0]
    kw = n_kv * head_dim

    qi = p_ref[:, qi_off:qi_off + n_idx_heads * idx_dim]
    ki_new = p_ref[:, ki_off:ki_off + idx_dim]
    wi = p_ref[:, wi_off:wi_off + LANES]
    qi_st = jnp.concatenate([qi[:, h * idx_dim:(h + 1) * idx_dim] for h in range(n_idx_heads)], axis=0).astype(BF16)
    w_col = jnp.concatenate([wi[:, h:h + 1] for h in range(n_idx_heads)], axis=0)

    def idx_scores(s):
        s = jnp.maximum(s, 0.0) * w_col
        out = s[0:t]
        for h in range(1, n_idx_heads):
            out = out + s[h * t:(h + 1) * t]
        return out

    ka = 1
    minmax = _minmax_init((t, page), ka)
    for c in range(n_pages):
        s = idx_scores(jnp.dot(qi_st, kidx_refs[c][...].astype(BF16), preferred_element_type=F32))
        sc_ref[c] = s
        minmax = _minmax_update(minmax, s, ka)
    own = idx_scores(lax.dot_general(qi_st, _pad_rows(ki_new, page).astype(BF16), NT_DIMS,
                                     preferred_element_type=F32))
    tok = lax.broadcasted_iota(jnp.int32, (t, page), 0)
    col = lax.broadcasted_iota(jnp.int32, (t, page), 1)
    own = jnp.where(col <= tok, own, -jnp.inf)
    sc_ref[n_pages] = own
    minmax = _minmax_update(minmax, own, ka)

    n_ch = n_pages + 1
    n_finite = (n_pages * page + 1 + lax.broadcasted_iota(jnp.int32, (t, 1), 0)).astype(F32)
    theta, jmax = _select_threshold(sc_ref, jnp.full((t, 1), float(n_top), F32), n_ch, minmax, n_finite, ka)
    _scores_to_bias(sc_ref, n_ch, theta, jmax, ka)

    rstride = 2 * n_kv
    scale = head_dim ** -0.5
    for h in range(n_kv):
        hs = slice(h * head_dim, (h + 1) * head_dim)
        vs = slice(kw + h * head_dim, kw + (h + 1) * head_dim)
        qs = [(_rms(p_ref[:, q_off + (h * group + g) * head_dim:q_off + (h * group + g + 1) * head_dim],
                    gq_ref[...]) * scale).astype(BF16) for g in range(group)]
        qst = jnp.concatenate(qs, axis=0)
        lgs = []
        for c in range(n_ch):
            k = (kv_refs[c][pl.ds(h, page, stride=rstride), :] if c < n_pages
                 else _pad_rows(kvn_ref[:, hs], page))
            lg = lax.dot_general(qst, k.astype(BF16), NT_DIMS, preferred_element_type=F32)
            lgs.append(lg + jnp.concatenate([sc_ref[c]] * group, axis=0))
        m = lgs[0].max(axis=-1, keepdims=True)
        for lg in lgs[1:]:
            m = jnp.maximum(m, lg.max(axis=-1, keepdims=True))
        l = jnp.zeros((group * t, 1), F32)
        acc = jnp.zeros((group * t, head_dim), F32)
        for c in range(n_ch):
            pc = jnp.exp(lgs[c] - m)
            l = l + jnp.sum(pc, axis=-1, keepdims=True)
            v = (kv_refs[c][pl.ds(n_kv + h, page, stride=rstride), :] if c < n_pages
                 else _pad_rows(kvn_ref[:, vs], page))
            acc = acc + jnp.dot(pc.astype(BF16), v.astype(BF16), preferred_element_type=F32)
        out = acc / l
        for g in range(group):
            o_ref[:, (h * group + g) * head_dim:(h * group + g + 1) * head_dim] = (
                out[g * t:(g + 1) * t].astype(o_ref.dtype))


def dsa_sample(p, kv_new, kidx_pool, kv_pool, page_table, g_q, *, t, n_top, n_idx_heads, idx_dim, n_kv, group,
               head_dim, q_off, qi_off, ki_off, wi_off):
    bd, n_pages = page_table.shape
    page = kidx_pool.shape[2]
    qw = n_kv * group * head_dim
    kw = n_kv * head_dim
    n = p.shape[1]
    body = functools.partial(_dsa_sample_body, n_pages=n_pages, page=page, n_top=n_top, n_idx_heads=n_idx_heads,
                             idx_dim=idx_dim, n_kv=n_kv, group=group, head_dim=head_dim, q_off=q_off,
                             qi_off=qi_off, ki_off=ki_off, wi_off=wi_off)
    in_specs = [pl.BlockSpec((t, n), lambda b, pt: (b, 0)),
                pl.BlockSpec((t, 2 * kw), lambda b, pt: (b, 0)),
                pl.BlockSpec((1, head_dim), lambda b, pt: (0, 0))]
    in_specs += [pl.BlockSpec((None, idx_dim, page), functools.partial(lambda b, pt, c: (pt[b, c], 0, 0), c=c))
                 for c in range(n_pages)]
    in_specs += [pl.BlockSpec((None, page * 2 * n_kv, head_dim),
                              functools.partial(lambda b, pt, c: (pt[b, c], 0, 0), c=c))
                 for c in range(n_pages)]
    return pl.pallas_call(
        body,
        grid_spec=pltpu.PrefetchScalarGridSpec(
            num_scalar_prefetch=1,
            grid=(bd,),
            in_specs=in_specs,
            out_specs=pl.BlockSpec((t, qw), lambda b, pt: (b, 0)),
            scratch_shapes=[pltpu.VMEM((n_pages + 1, t, page), F32)]),
        out_shape=jax.ShapeDtypeStruct((bd * t, qw), F32),
        compiler_params=_params("arbitrary"),
        name="dsa_sample",
    )(page_table, p, kv_new, g_q.reshape(1, head_dim), *([kidx_pool] * n_pages), *([kv_pool] * n_pages))


def _dilated_mask(rel, window, dil):
    return (rel >= 0) & (rel <= window) & ((rel & (dil - 1)) == 0)


def _dil_prompt_body(q_ref, k_ref, v_ref, gq_ref, o_ref, lse_ref, m_ref, l_ref, acc_ref, *, window, dil, n_heads,
                     head_dim):
    tq = q_ref.shape[1]
    i, kc, nkc = pl.program_id(1), pl.program_id(2), pl.num_programs(2)
    cidx = i - (nkc - 1) + kc

    @pl.when(kc == 0)
    def _():
        m_ref[...] = jnp.full(m_ref.shape, NEG, F32)
        l_ref[...] = jnp.zeros(l_ref.shape, F32)
        acc_ref[...] = jnp.zeros(acc_ref.shape, F32)

    @pl.when(cidx >= 0)
    def _():
        rel = ((i - cidx) * tq + lax.broadcasted_iota(jnp.int32, (tq, tq), 0)
               - lax.broadcasted_iota(jnp.int32, (tq, tq), 1))
        bias = jnp.where(_dilated_mask(rel, window, dil), 0.0, NEG)
        scale = head_dim ** -0.5
        for h in range(n_heads):
            hs = slice(h * head_dim, (h + 1) * head_dim)
            qn = (_rms(q_ref[0, :, hs], gq_ref[...]) * scale).astype(BF16)
            lg = lax.dot_general(qn, k_ref[0, :, hs].astype(BF16), NT_DIMS, preferred_element_type=F32) + bias
            m_prev = m_ref[h]
            m_new = jnp.maximum(m_prev, jnp.max(lg, axis=-1, keepdims=True))
            alpha = jnp.exp(m_prev - m_new)
            p = jnp.exp(lg - m_new)
            l_ref[h] = alpha * l_ref[h] + jnp.sum(p, axis=-1, keepdims=True)
            acc_ref[h] = alpha * acc_ref[h] + jnp.dot(p.astype(BF16), v_ref[0, :, hs].astype(BF16),
                                                      preferred_element_type=F32)
            m_ref[h] = m_new

    @pl.when(kc == nkc - 1)
    def _():
        for h in range(n_heads):
            hs = slice(h * head_dim, (h + 1) * head_dim)
            o_ref[0, :, hs] = acc_ref[h] / l_ref[h]
            lse_ref[0, :, hs] = jnp.broadcast_to(m_ref[h] + jnp.log(l_ref[h]), (tq, head_dim))


def dilated_prompt_group(p, kn, g_q, *, q_block, k_block, v_block, window, dil, n_heads, head_dim, tq):
    b, s, _ = p.shape
    w = n_heads * head_dim
    nkc = -(-window // tq) + 1

    def kv_map(blk):
        return lambda bi, i, kc: (bi, jnp.maximum(i - (nkc - 1) + kc, 0), blk)

    return pl.pallas_call(
        functools.partial(_dil_prompt_body, window=window, dil=dil, n_heads=n_heads, head_dim=head_dim),
        grid=(b, s // tq, nkc),
        in_specs=[pl.BlockSpec((1, tq, w), lambda bi, i, kc: (bi, i, q_block)),
                  pl.BlockSpec((1, tq, w), kv_map(k_block)),
                  pl.BlockSpec((1, tq, w), kv_map(v_block)),
                  pl.BlockSpec((1, head_dim), lambda bi, i, kc: (0, 0))],
        out_specs=[pl.BlockSpec((1, tq, w), lambda bi, i, kc: (bi, i, 0)),
                   pl.BlockSpec((1, tq, w), lambda bi, i, kc: (bi, i, 0))],
        out_shape=[jax.ShapeDtypeStruct((b, s, w), F32), jax.ShapeDtypeStruct((b, s, w), F32)],
        scratch_shapes=[pltpu.VMEM((n_heads, tq, 1), F32), pltpu.VMEM((n_heads, tq, 1), F32),
                        pltpu.VMEM((n_heads, tq, head_dim), F32)],
        compiler_params=_params("parallel", "parallel", "arbitrary"),
        name="dilated_prompt",
    )(p, kn, p, g_q.reshape(1, head_dim))


def _dil_sample_body(p_ref, kn_ref, buf_ref, gq_ref, o_ref, lse_ref, nbuf_ref, *, window, dil, n_heads, head_dim,
                     q_off, k_off, v_off):
    t = p_ref.shape[0]
    rstride = 2 * n_heads
    wb = buf_ref.shape[1] // rstride
    w = n_heads * head_dim
    k_new = kn_ref[:, k_off:k_off + w]
    v_new = p_ref[:, v_off:v_off + w]
    nbuf_ref[0, 0:(wb - t) * rstride, :] = buf_ref[0, t * rstride:wb * rstride, :]
    for c, new in enumerate((k_new, v_new)):
        for h in range(n_heads):
            nbuf_ref[0, pl.ds((wb - t) * rstride + c * n_heads + h, t, stride=rstride), :] = (
                new[:, h * head_dim:(h + 1) * head_dim])

    rel = wb + lax.broadcasted_iota(jnp.int32, (t, wb), 0) - lax.broadcasted_iota(jnp.int32, (t, wb), 1)
    bias = jnp.where(_dilated_mask(rel, window, dil), 0.0, NEG)
    rel_o = lax.broadcasted_iota(jnp.int32, (t, LANES), 0) - lax.broadcasted_iota(jnp.int32, (t, LANES), 1)
    bias_o = jnp.where(_dilated_mask(rel_o, window, dil), 0.0, NEG)
    scale = head_dim ** -0.5
    for h in range(n_heads):
        hs = slice(h * head_dim, (h + 1) * head_dim)
        k_past = buf_ref[0, pl.ds(h, wb, stride=rstride), :]
        v_past = buf_ref[0, pl.ds(n_heads + h, wb, stride=rstride), :]
        qn = (_rms(p_ref[:, q_off + h * head_dim:q_off + (h + 1) * head_dim], gq_ref[...]) * scale).astype(BF16)
        lg = lax.dot_general(qn, k_past.astype(BF16), NT_DIMS, preferred_element_type=F32) + bias
        lg_o = lax.dot_general(qn, _pad_rows(k_new[:, hs], LANES).astype(BF16), NT_DIMS,
                               preferred_element_type=F32) + bias_o
        m = jnp.maximum(jnp.max(lg, axis=-1, keepdims=True), jnp.max(lg_o, axis=-1, keepdims=True))
        pp = jnp.exp(lg - m)
        pp_o = jnp.exp(lg_o - m)
        l = jnp.sum(pp, axis=-1, keepdims=True) + jnp.sum(pp_o, axis=-1, keepdims=True)
        acc = (jnp.dot(pp.astype(BF16), v_past.astype(BF16), preferred_element_type=F32)
               + jnp.dot(pp_o.astype(BF16), _pad_rows(v_new[:, hs], LANES).astype(BF16),
                         preferred_element_type=F32))
        o_ref[:, hs] = acc / l
        lse_ref[:, hs] = jnp.broadcast_to(m + jnp.log(l), (t, head_dim))


def dilated_sample_group(p, kn, buf, g_q, *, t, window, dil, n_heads, head_dim, q_off, k_off, v_off):
    bd, wb, w2 = buf.shape
    w = n_heads * head_dim
    n, nk = p.shape[1], kn.shape[1]
    return pl.pallas_call(
        functools.partial(_dil_sample_body, window=window, dil=dil, n_heads=n_heads, head_dim=head_dim,
                          q_off=q_off, k_off=k_off, v_off=v_off),
        grid=(bd,),
        in_specs=[pl.BlockSpec((t, n), lambda b: (b, 0)),
                  pl.BlockSpec((t, nk), lambda b: (b, 0)),
                  pl.BlockSpec((1, wb, w2), lambda b: (b, 0, 0)),
                  pl.BlockSpec((1, head_dim), lambda b: (0, 0))],
        out_specs=[pl.BlockSpec((t, w), lambda b: (b, 0)),
                   pl.BlockSpec((t, w), lambda b: (b, 0)),
                   pl.BlockSpec((1, wb, w2), lambda b: (b, 0, 0))],
        out_shape=[jax.ShapeDtypeStruct((bd * t, w), F32), jax.ShapeDtypeStruct((bd * t, w), F32),
                   jax.ShapeDtypeStruct((bd, wb, w2), F32)],
        compiler_params=_params("parallel"),
        name="dilated_sample",
    )(p, kn, buf, g_q.reshape(1, head_dim))


def _merge_groups_body(*refs):
    n = (len(refs) - 1) // 2
    o_refs, lse_refs, out_ref = refs[:n], refs[n:2 * n], refs[2 * n]
    lses = [r[...] for r in lse_refs]
    m = lses[0]
    for x in lses[1:]:
        m = jnp.maximum(m, x)
    ws = [jnp.exp(x - m) for x in lses]
    den = ws[0]
    for x in ws[1:]:
        den = den + x
    acc = ws[0] * o_refs[0][...]
    for wgt, o in zip(ws[1:], o_refs[1:]):
        acc = acc + wgt * o[...]
    out_ref[...] = (acc / den).astype(out_ref.dtype)


def merge_groups(outs, lses):
    m, w = outs[0].shape
    tm = min(m, ROW_TILE)
    spec = pl.BlockSpec((tm, w), lambda i: (i, 0))
    return pl.pallas_call(
        _merge_groups_body,
        grid=(m // tm,),
        in_specs=[spec] * (2 * len(outs)),
        out_specs=spec,
        out_shape=jax.ShapeDtypeStruct((m, w), BF16),
        compiler_params=_params("parallel"),
        name="merge_groups",
    )(*outs, *lses)


def _pool_body(u_ref, halo_ref, wg_ref, sc_ref, o_ref, ext_ref, *, n_pre, windows, halo_is_history):
    nb, t, d = u_ref.shape
    gw = d // len(windows)
    i = pl.program_id(1)
    halo = halo_ref[...]
    if halo_is_history:
        halo = jnp.where(i == 0, 0.0, halo)
    ext_ref[:, 0:POOL_HALO, :] = halo
    ext_ref[:, POOL_HALO:POOL_HALO + t, :] = u_ref[...]
    pos = i * t + lax.broadcasted_iota(jnp.int32, (1, t, 1), 1)
    for g, w in enumerate(windows):
        cs = slice(g * gw, (g + 1) * gw)
        acc = ext_ref[:, POOL_HALO:POOL_HALO + t, cs]
        for j in range(1, w):
            acc = acc + ext_ref[:, POOL_HALO - j:POOL_HALO - j + t, cs]
        div = jnp.minimum(w, n_pre + 1 + pos).astype(F32)
        r = acc / div - u_ref[:, :, cs]
        y = jnp.dot(r.reshape(nb * t, gw).astype(BF16), wg_ref[g], preferred_element_type=F32) * sc_ref[:, cs]
        o_ref[:, :, cs] = y.reshape(nb, t, gw).astype(o_ref.dtype)


def pool_mix(u, halo, w_grp, scale, *, n_pre, windows, batch_block, t_block, halo_is_history, out_dtype):
    b, t, d = u.shape
    hb = t_block // POOL_HALO
    if halo_is_history:
        halo_map = lambda bi, i: (bi, jnp.maximum(i * hb - 1, 0), 0)
    else:
        halo_map = lambda bi, i: (bi, 0, 0)
    return pl.pallas_call(
        functools.partial(_pool_body, n_pre=n_pre, windows=windows, halo_is_history=halo_is_history),
        grid=(b // batch_block, t // t_block),
        in_specs=[pl.BlockSpec((batch_block, t_block, d), lambda bi, i: (bi, i, 0)),
                  pl.BlockSpec((batch_block, POOL_HALO, d), halo_map),
                  pl.BlockSpec(w_grp.shape, lambda bi, i: (0, 0, 0)),
                  pl.BlockSpec((1, d), lambda bi, i: (0, 0))],
        out_specs=pl.BlockSpec((batch_block, t_block, d), lambda bi, i: (bi, i, 0)),
        out_shape=jax.ShapeDtypeStruct((b, t, d), out_dtype),
        scratch_shapes=[pltpu.VMEM((batch_block, POOL_HALO + t_block, d), F32)],
        compiler_params=_params("parallel", "parallel"),
        name="pool_mix",
    )(u, halo, w_grp, scale.reshape(1, d))


def _mlstm_body(q_ref, k_ref, v_ref, op_ref, gi_ref, gf_ref, bi_ref, bf_ref, gh_ref, c0_ref, n0_ref, m0_ref,
                y_ref, c_ref, n_ref, m_ref, cs, ns, ms, *, n_heads, head_dim):
    L = q_ref.shape[1]
    ls = max(L, LANES)
    ci = pl.program_id(1)
    hp = lax.Precision.HIGHEST

    @pl.when(ci == 0)
    def _():
        cs[...] = c0_ref[0]
        ns[...] = n0_ref[0]
        ms[...] = m0_ref[0]

    li = gi_ref[0] + bi_ref[...]
    lf = jax.nn.log_sigmoid(gf_ref[0] + bf_ref[...])
    row = lax.broadcasted_iota(jnp.int32, (L, ls), 0)
    col = lax.broadcasted_iota(jnp.int32, (L, ls), 1)
    causal = col <= row
    bt = jnp.dot(jnp.where(causal, 1.0, 0.0), _pad_rows(lf, ls), preferred_element_type=F32, precision=hp)
    lane = lax.broadcasted_iota(jnp.int32, (L, LANES), 1)
    lane1 = lax.broadcasted_iota(jnp.int32, (1, LANES), 1)
    m_all = ms[...]
    for h in range(n_heads):
        hs = slice(h * head_dim, (h + 1) * head_dim)
        bt_h = bt[:, h:h + 1]
        li_h = li[:, h:h + 1]
        m_prev = m_all[:, h:h + 1]
        a_h = jnp.where(lane == 0, bt_h, jnp.where(lane == 1, 1.0, 0.0))
        b_h = jnp.where(lane == 0, 1.0, jnp.where(lane == 1, li_h - bt_h, 0.0))
        dmat = lax.dot_general(a_h, _pad_rows(b_h, ls), NT_DIMS, preferred_element_type=F32, precision=hp)
        dmat = jnp.where(causal, dmat, -jnp.inf)
        g = bt_h + m_prev
        m_t = jnp.maximum(g, jnp.max(dmat, axis=-1, keepdims=True))
        w_intra = jnp.exp(dmat - m_t)
        w_inter = jnp.exp(g - m_t)
        q = q_ref[0, :, hs]
        k = k_ref[0, :, hs] * head_dim ** -0.5
        v = v_ref[0, :, hs]
        qb, kb, vb = q.astype(BF16), _pad_rows(k, ls).astype(BF16), _pad_rows(v, ls).astype(BF16)
        qk = lax.dot_general(qb, kb, NT_DIMS, preferred_element_type=F32) * w_intra
        c_h = cs[h]
        num = (w_inter * jnp.dot(qb, c_h.astype(BF16), preferred_element_type=F32)
               + jnp.dot(qk.astype(BF16), vb, preferred_element_type=F32))
        n_h = ns[h:h + 1, :]
        den = w_inter * jnp.sum(q * n_h, axis=-1, keepdims=True) + jnp.sum(qk, axis=-1, keepdims=True)
        hout = num / jnp.maximum(jnp.abs(den), jnp.exp(-m_t))
        b_last = bt_h[L - 1:L, :]
        decay = b_last - bt_h + li_h
        m_new = jnp.maximum(b_last + m_prev, jnp.max(decay, axis=0, keepdims=True))
        keep = jnp.exp(b_last + m_prev - m_new)
        kw = k * jnp.exp(decay - m_new)
        cs[h] = keep * c_h + lax.dot_general(_pad_rows(kw, ls).astype(BF16), vb, TN_DIMS,
                                             preferred_element_type=F32)
        ns[h:h + 1, :] = keep * n_h + jnp.sum(kw, axis=0, keepdims=True)
        m_all = jnp.where(lane1 == h, m_new, m_all)
        hn = _rms(hout, gh_ref[:, hs])
        y_ref[0, :, hs] = (jax.nn.sigmoid(op_ref[0, :, hs]) * hn).astype(y_ref.dtype)
    ms[...] = m_all

    @pl.when(ci == pl.num_programs(1) - 1)
    def _():
        c_ref[0] = cs[...]
        n_ref[0] = ns[...]
        m_ref[0] = ms[...]


def mlstm(p, gates, b_i, b_f, g_h, c0, n0, m0, *, n_heads, chunk, out_dtype):
    b, t, w4 = p.shape
    w = w4 // 4
    head_dim = w // n_heads
    nc = t // chunk

    def lane_row(x):
        return jnp.pad(x.reshape(1, -1), ((0, 0), (0, LANES - x.size)))

    m0p = jnp.pad(m0.reshape(b, 1, n_heads), ((0, 0), (0, 0), (0, LANES - n_heads)))
    colspec = lambda blk, wd: pl.BlockSpec((1, chunk, wd), lambda bi, ci: (bi, ci, blk))
    const = lambda shape: pl.BlockSpec(shape, lambda bi, ci: (0,) * len(shape))
    state = lambda shape: pl.BlockSpec((1,) + shape, lambda bi, ci: (bi,) + (0,) * len(shape))
    y, c, n, m = pl.pallas_call(
        functools.partial(_mlstm_body, n_heads=n_heads, head_dim=head_dim),
        grid=(b, nc),
        in_specs=[colspec(0, w), colspec(1, w), colspec(2, w), colspec(3, w),
                  colspec(0, LANES), colspec(1, LANES),
                  const((1, LANES)), const((1, LANES)), const((1, w)),
                  state((n_heads, head_dim, head_dim)), state((n_heads, head_dim)), state((1, LANES))],
        out_specs=[colspec(0, w), state((n_heads, head_dim, head_dim)), state((n_heads, head_dim)),
                   state((1, LANES))],
        out_shape=[jax.ShapeDtypeStruct((b, t, w), out_dtype),
                   jax.ShapeDtypeStruct((b, n_heads, head_dim, head_dim), F32),
                   jax.ShapeDtypeStruct((b, n_heads, head_dim), F32),
                   jax.ShapeDtypeStruct((b, 1, LANES), F32)],
        scratch_shapes=[pltpu.VMEM((n_heads, head_dim, head_dim), F32),
                        pltpu.VMEM((n_heads, head_dim), F32),
                        pltpu.VMEM((1, LANES), F32)],
        compiler_params=_params("parallel", "arbitrary"),
        name="mlstm",
    )(p, p, p, p, gates, gates, lane_row(b_i), lane_row(b_f), g_h.reshape(1, w), c0, n0, m0p)
    return y, c, n, m[:, 0, :n_heads]


def kernel(x_prompt, x_sample, cache_kv_a, cache_kidx_a, page_table, cache_kv_b0, cache_kv_b1, cache_kv_b2,
           state_pool_c, state_c_d, state_n_d, state_m_d, cache_mem_kv, mem_prompt,
           g_norm_mix, g_norm_mem, g_norm_ffn, g_mem_src, w_mem_q, w_mem_kv, g_mem_q, g_mem_k, w_mem_o,
           w_ff1, w_ff2, w_in_a, g_q_a, g_k_a, w_out_a, w_in_b, g_q_b, g_k_b, w_out_b,
           w_in_c, w_grp_c, scale_c, w_out_c, w_in_d, b_i_d, b_f_d, g_h_d, w_out_d):
    bp, s, d = x_prompt.shape
    bd, t, _ = x_sample.shape
    depth = g_norm_mix.shape[0]
    n_mixers = 4
    xp = x_prompt.reshape(bp * s, d)
    xs = x_sample.reshape(bd * t, d)
    cache_b = (cache_kv_b0, cache_kv_b1, cache_kv_b2)
    dilated = ((128, 1), (512, 4), (2048, 16))
    pool_windows = (2, 4, 8, 16)
    n_mem, n_heads_mem = mem_prompt.shape[1], cache_mem_kv.shape[4]
    mem2d = mem_prompt.reshape(bp * n_mem, d)
    hd_mem = cache_mem_kv.shape[5]
    mem_cache_rows = jnp.swapaxes(
        cache_mem_kv.reshape(depth, bd, n_mem, 2, n_heads_mem, hd_mem // LANES, LANES), 4, 5
    ).reshape(depth, bd, n_mem * 2 * hd_mem // LANES * n_heads_mem, LANES)

    kv_a_p, ki_a_p, kv_a_s, ki_a_s = [], [], [], []
    kv_b_p = [[] for _ in dilated]
    kv_b_s = [[] for _ in dilated]
    pool_p, pool_s = [], []
    cd_p, nd_p, md_p, cd_s, nd_s, md_s = [], [], [], [], [], []
    mem_p = []

    for i in range(depth):
        kind, j = i % n_mixers, i // n_mixers
        g_mix = g_norm_mix[i]
        if kind == 0:
            n_kv, hd = cache_kv_a.shape[4], cache_kv_a.shape[5]
            idx_dim = cache_kidx_a.shape[3]
            qw = w_out_a.shape[1]
            group = qw // (n_kv * hd)
            kw = n_kv * hd
            n_idx = (w_in_a.shape[2] - qw - 2 * kw - idx_dim) // (idx_dim + 1)
            iw = n_idx * idx_dim
            w = w_in_a[j]
            o3 = qw + 2 * kw
            ki_off = o3 + iw
            wi_off = ki_off + LANES
            n_cols = -(-(wi_off + LANES) // 512) * 512
            wa = jnp.concatenate([w[:, :o3 + iw], _pad_cols(w[:, o3 + iw:o3 + iw + idx_dim], LANES),
                                  _pad_cols(w[:, o3 + iw + idx_dim:], n_cols - wi_off)], axis=1).astype(BF16)
            wo = w_out_a[j].astype(BF16)
            pp = rms_matmul(xp, g_mix, wa)
            ps = rms_matmul(xs, g_mix, wa)
            kvp, kvp_bf = head_norm(pp, qw // kw, kw, hd, g_k_a[j], v_block=qw // kw + 1, with_bf16=True)
            kvs = head_norm(ps, qw // kw, kw, hd, g_k_a[j], v_block=qw // kw + 1)
            kip = pp[:, ki_off:ki_off + idx_dim]
            kis = ps[:, ki_off:ki_off + idx_dim]
            kv_a_p.append(kvp.reshape(bp, s, 2, n_kv, hd))
            ki_a_p.append(kip.reshape(bp, s, idx_dim))
            kv_a_s.append(kvs.reshape(bd, t, 2, n_kv, hd))
            ki_a_s.append(kis.reshape(bd, t, idx_dim))
            dims = dict(n_idx_heads=n_idx, idx_dim=idx_dim, n_kv=n_kv, group=group, head_dim=hd)
            op = dsa_prompt(pp.reshape(bp, s, n_cols), kip.astype(BF16).reshape(bp, s, idx_dim),
                            kvp_bf.reshape(bp, s, 2 * kw), g_q_a[j], g_k_a[j], q_block=0, qi_block=o3 // iw,
                            wi_block=wi_off // LANES, n_top=min(256, s // 4), **dims)
            n_pool, page = cache_kidx_a.shape[1], cache_kidx_a.shape[2]
            past = page_table.shape[1] * page
            kidx_pool = jnp.swapaxes(cache_kidx_a[j], 1, 2)
            kv_pool = cache_kv_a[j].reshape(n_pool, page * 2 * n_kv, hd)
            os_ = dsa_sample(ps, kvs, kidx_pool, kv_pool, page_table,
                             g_q_a[j], t=t, n_top=min(256, (past + t) // 4), q_off=0, qi_off=o3, ki_off=ki_off,
                             wi_off=wi_off, **dims)
            xp = matmul_res(op.reshape(bp * s, qw), wo, xp)
            xs = matmul_res(os_, wo, xs)
        elif kind == 1:
            n_groups = len(dilated)
            hg, hd = cache_kv_b0.shape[4], cache_kv_b0.shape[5]
            gw = hg * hd
            wb_ = w_in_b[j].astype(BF16)
            wo = w_out_b[j].astype(BF16)
            pp = rms_matmul(xp, g_mix, wb_)
            ps = rms_matmul(xs, g_mix, wb_)
            gk_row = jnp.repeat(g_k_b[j], hg, axis=0).reshape(1, n_groups * gw)
            knp = head_norm_groups(pp, 1, n_groups * gw, gk_row)
            kns = head_norm_groups(ps, 1, n_groups * gw, gk_row)
            pp3 = pp.reshape(bp, s, -1)
            knp3 = knp.reshape(bp, s, -1)
            outs_p, lses_p, outs_s, lses_s = [], [], [], []
            for g, (win, dil) in enumerate(dilated):
                o, lse = dilated_prompt_group(pp3, knp3, g_q_b[j, g], q_block=g, k_block=g,
                                              v_block=2 * n_groups + g, window=win, dil=dil, n_heads=hg,
                                              head_dim=hd, tq=min(s, 256 if win <= 256 else 512))
                outs_p.append(o.reshape(bp * s, gw))
                lses_p.append(lse.reshape(bp * s, gw))
                wb = min(win, s)
                kv_b_p[g].append(jnp.stack(
                    [knp3[:, s - wb:, g * gw:(g + 1) * gw].reshape(bp, wb, hg, hd),
                     pp3[:, s - wb:, (2 * n_groups + g) * gw:(2 * n_groups + g + 1) * gw].reshape(bp, wb, hg, hd)],
                    axis=2))
                buf = cache_b[g][j]
                o, lse, nbuf = dilated_sample_group(ps, kns, buf.reshape(bd, buf.shape[1] * 2 * hg, hd), g_q_b[j, g],
                                                    t=t, window=win, dil=dil, n_heads=hg, head_dim=hd,
                                                    q_off=g * gw, k_off=g * gw, v_off=(2 * n_groups + g) * gw)
                outs_s.append(o)
                lses_s.append(lse)
                kv_b_s[g].append(nbuf.reshape(buf.shape))
            xp = matmul_res(merge_groups(outs_p, lses_p), wo, xp)
            xs = matmul_res(merge_groups(outs_s, lses_s), wo, xs)
        elif kind == 2:
            wi_, wo = w_in_c[j].astype(BF16), w_out_c[j].astype(BF16)
            wg = w_grp_c[j].astype(BF16)
            up = rms_matmul(xp, g_mix, wi_).reshape(bp, s, d)
            us = rms_matmul(xs, g_mix, wi_).reshape(bd, t, d)
            n_state = state_pool_c.shape[2]
            yp = pool_mix(up, up, wg, scale_c[j], n_pre=0, windows=pool_windows, batch_block=1, t_block=512,
                          halo_is_history=True, out_dtype=BF16)
            halo = jnp.pad(state_pool_c[j], ((0, 0), (POOL_HALO - n_state, 0), (0, 0)))
            ys = pool_mix(us, halo, wg, scale_c[j], n_pre=n_state, windows=pool_windows, batch_block=min(bd, 32),
                          t_block=t, halo_is_history=False, out_dtype=F32)
            pool_p.append(up[:, s - n_state:])
            pool_s.append(jnp.concatenate([state_pool_c[j], us], axis=1)[:, -n_state:])
            xp = matmul_res(yp.reshape(bp * s, d), wo, xp)
            xs = matmul_res(ys.reshape(bd * t, d), wo, xs)
        else:
            nh, hd = state_c_d.shape[2], state_c_d.shape[3]
            wd = nh * hd
            w = w_in_d[j]
            w_main = w[:, :4 * wd].astype(BF16)
            w_gate = jnp.concatenate([_pad_cols(w[:, 4 * wd:4 * wd + nh], LANES),
                                      _pad_cols(w[:, 4 * wd + nh:], LANES)], axis=1)
            wo = w_out_d[j].astype(BF16)
            pp = rms_matmul(xp, g_mix, w_main).reshape(bp, s, 4 * wd)
            ps = rms_matmul(xs, g_mix, w_main).reshape(bd, t, 4 * wd)
            gp = rms_matmul(xp, g_mix, w_gate, precise=True).reshape(bp, s, 2 * LANES)
            gs = rms_matmul(xs, g_mix, w_gate, precise=True).reshape(bd, t, 2 * LANES)
            zeros = lambda *shape: jnp.zeros(shape, F32)
            yp, c, n, m = mlstm(pp, gp, b_i_d[j], b_f_d[j], g_h_d[j], zeros(bp, nh, hd, hd), zeros(bp, nh, hd),
                                zeros(bp, nh), n_heads=nh, chunk=256, out_dtype=BF16)
            cd_p.append(c)
            nd_p.append(n)
            md_p.append(m)
            ys, c, n, m = mlstm(ps, gs, b_i_d[j], b_f_d[j], g_h_d[j], state_c_d[j], state_n_d[j], state_m_d[j],
                                n_heads=nh, chunk=t, out_dtype=F32)
            cd_s.append(c)
            nd_s.append(n)
            md_s.append(m)
            xp = matmul_res(yp.reshape(bp * s, wd), wo, xp)
            xs = matmul_res(ys.reshape(bd * t, wd), wo, xs)

        wm = n_heads_mem * cache_mem_kv.shape[5]
        mkv_raw = rms_matmul(mem2d, g_mem_src[i], w_mem_kv[i].astype(BF16))
        mkv = head_norm(mkv_raw, 0, wm, wm // n_heads_mem, g_mem_k[i], v_block=1)
        mem_p.append(mkv.reshape(bp, n_mem, 2, n_heads_mem, wm // n_heads_mem))
        wq, wo = w_mem_q[i].astype(BF16), w_mem_o[i].astype(BF16)
        qp = rms_matmul(xp, g_norm_mem[i], wq).reshape(bp, s, wm)
        qs = rms_matmul(xs, g_norm_mem[i], wq).reshape(bd, t, wm)
        ap = mem_attn(qp, mkv.reshape(bp, n_mem, 2 * wm), g_mem_q[i], n_heads=n_heads_mem, n_mem=n_mem,
                      batch_block=1, q_block=min(s, 1024), out_dtype=BF16)
        as_ = mem_attn(qs, mem_cache_rows, g_mem_q[i], n_heads=n_heads_mem, n_mem=n_mem, batch_block=4,
                       q_block=t, out_dtype=F32, layer=i)
        xp = matmul_res(ap.reshape(bp * s, wm), wo, xp)
        xs = matmul_res(as_.reshape(bd * t, wm), wo, xs)

        w1, w2 = w_ff1[i].astype(BF16), w_ff2[i].astype(BF16)
        xp = ffn(xp, g_norm_ffn[i], w1, w2)
        xs = ffn(xs, g_norm_ffn[i], w1, w2)

    return (xp.reshape(bp, s, d), xs.reshape(bd, t, d),
            jnp.stack(kv_a_p), jnp.stack(ki_a_p), jnp.stack(kv_a_s), jnp.stack(ki_a_s),
            jnp.stack(kv_b_p[0]), jnp.stack(kv_b_p[1]), jnp.stack(kv_b_p[2]),
            jnp.stack(kv_b_s[0]), jnp.stack(kv_b_s[1]), jnp.stack(kv_b_s[2]),
            jnp.stack(pool_p), jnp.stack(pool_s),
            jnp.stack(cd_p), jnp.stack(nd_p), jnp.stack(md_p),
            jnp.stack(cd_s), jnp.stack(nd_s), jnp.stack(md_s),
            jnp.stack(mem_p))
```
